```python
import jax, jax.numpy as jnp
from jax import lax
import numpy as np

D_MODEL = 1024
BATCH = 4
SEQ = 8192
DEPTH = 4

GRID_W = 64
CTX_LEN = 256
N_MIXERS = 2
N_A = (DEPTH + N_MIXERS - 1) // N_MIXERS
N_B = DEPTH // N_MIXERS
D_INNER = 2 * D_MODEL
M_HEADS = 4
M_HEAD_DIM = D_INNER // M_HEADS
CHUNK = 64
CONV_K = 3
F_GROUPS = 4
D_FF = 4 * D_MODEL
ALPHA = float((2 * DEPTH) ** 0.25)
BETA = float((8 * DEPTH) ** -0.25)
LN_EPS = 1e-5

kernel_name = "hybrid_mlstm_fourier_prefix_dit"


def layer_norm(x):
    xf = x.astype(jnp.float32)
    mu = xf.mean(-1, keepdims=True)
    var = jnp.square(xf - mu).mean(-1, keepdims=True)
    return ((xf - mu) * lax.rsqrt(var + LN_EPS)).astype(x.dtype)


def modulate(x, shift, scale):
    return layer_norm(x) * (1 + scale) + shift


def post_norm(x, g, b):
    return layer_norm(x) * g + b


def conv_grid(x, w):
    B, T, C = x.shape
    rows = T // GRID_W
    y = lax.conv_general_dilated(
        x.reshape(B, rows, GRID_W, C), w[:, :, None, :], window_strides=(1, 1), padding="SAME",
        dimension_numbers=("NHWC", "HWIO", "NHWC"), feature_group_count=C)
    return y.reshape(B, T, C)


def conv_seq(x, w):
    C = x.shape[-1]
    return lax.conv_general_dilated(
        x, w[CONV_K // 2][:, None, :], window_strides=(1,), padding="SAME",
        dimension_numbers=("NWC", "WIO", "NWC"), feature_group_count=C)


def mlstm_chunkwise(q, k, v, log_i, log_f, state):
    B, H, T, dh = q.shape
    nc = T // CHUNK

    def to_chunks(a):
        return jnp.moveaxis(a.reshape(a.shape[:2] + (nc, CHUNK) + a.shape[3:]), 2, 0)

    tril = jnp.tril(jnp.ones((CHUNK, CHUNK), dtype=bool))

    def step(carry, inp):
        C, n, m = carry
        qc, kc, vc, li, lf = inp
        b = jnp.cumsum(lf, axis=-1)
        a_inter = b + m[..., None]
        log_w = jnp.where(tril, b[..., :, None] - b[..., None, :] + li[..., None, :], -jnp.inf)
        m_row = jnp.maximum(a_inter, log_w.max(-1))
        w_inter = jnp.exp(a_inter - m_row)
        s = jnp.einsum("bhld,bhsd->bhls", qc, kc).astype(jnp.float32) * jnp.exp(log_w - m_row[..., None])
        num = (w_inter[..., None] * jnp.einsum("bhed,bhld->bhle", C, qc)
               + jnp.einsum("bhls,bhse->bhle", s, vc))
        den = w_inter * jnp.einsum("bhd,bhld->bhl", n, qc) + s.sum(-1)
        h = num / jnp.maximum(jnp.abs(den), jnp.exp(-m_row))[..., None]
        m_new = m_row[..., -1]
        w_k = jnp.exp(b[..., -1:] - b + li - m_new[..., None])
        decay = jnp.exp(b[..., -1] + m - m_new)
        C = decay[..., None, None] * C + jnp.einsum("bhs,bhse,bhsd->bhed", w_k, vc, kc)
        n = decay[..., None] * n + jnp.einsum("bhs,bhsd->bhd", w_k, kc)
        return (C, n, m_new), h

    state, h = lax.scan(step, state, (to_chunks(q), to_chunks(k), to_chunks(v),
                                      to_chunks(log_i), to_chunks(log_f)))
    return jnp.moveaxis(h, 0, 2).reshape(B, H, T, dh), state


def mlstm_mixer(h_x, h_c, need_ctx, w_in, w_conv, b_conv, w_q, w_k, w_v, w_gate, b_gate, gn_w, skip, w_out):
    wg = w_gate.reshape(3, M_HEADS, M_HEAD_DIM, 4 * M_HEADS)

    def branch(h, conv_fn):
        B, T, _ = h.shape
        x_m, z = jnp.split(h @ w_in, 2, axis=-1)
        x_c = jax.nn.silu(conv_fn(x_m, w_conv) + b_conv)
        xh_c = x_c.reshape(B, T, M_HEADS, M_HEAD_DIM)
        q = jnp.einsum("bthd,hde->bhte", xh_c, w_q)
        k = jnp.einsum("bthd,hde->bhte", xh_c, w_k) * (M_HEAD_DIM ** -0.5)
        v = jnp.einsum("bthd,hde->bhte", x_m.reshape(B, T, M_HEADS, M_HEAD_DIM), w_v)
        pre = (jnp.einsum("bhte,hen->btn", q, wg[0]) + jnp.einsum("bhte,hen->btn", k, wg[1])
               + jnp.einsum("bhte,hen->btn", v, wg[2]) + b_gate).astype(jnp.float32)
        pre = jnp.transpose(pre.reshape(B, T, 2, 2, M_HEADS), (2, 3, 0, 4, 1))
        return x_c, z, q, k, v, pre[:, 0], jax.nn.log_sigmoid(pre[:, 1])

    def readout(h, x_c, z):
        mu = h.mean(-1, keepdims=True)
        var = jnp.square(h - mu).mean(-1, keepdims=True)
        hn = (h - mu) * lax.rsqrt(var + LN_EPS)
        B, H, T, dh = h.shape
        hn = jnp.transpose(hn, (0, 2, 1, 3)).reshape(B, T, H * dh).astype(x_c.dtype) * gn_w
        return ((hn + skip * x_c) * jax.nn.silu(z)) @ w_out

    cb = branch(h_c, conv_seq)
    xb = branch(h_x, conv_grid)
    B = h_c.shape[0]
    outs_c, outs_x = [], []
    for d in range(2):
        if d == 0:
            flip = lambda a: a
        else:
            flip = lambda a: jnp.flip(a, axis=2)
        state0 = (jnp.zeros((B, M_HEADS, M_HEAD_DIM, M_HEAD_DIM), jnp.float32),
                  jnp.zeros((B, M_HEADS, M_HEAD_DIM), jnp.float32),
                  jnp.zeros((B, M_HEADS), jnp.float32))
        hc, st = mlstm_chunkwise(flip(cb[2]), flip(cb[3]), flip(cb[4]), flip(cb[5][d]), flip(cb[6][d]), state0)
        hx, _ = mlstm_chunkwise(flip(xb[2]), flip(xb[3]), flip(xb[4]), flip(xb[5][d]), flip(xb[6][d]), st)
        outs_c.append(flip(hc))
        outs_x.append(flip(hx))
    y_x = readout(outs_x[0] + outs_x[1], xb[0], xb[1])
    y_c = readout(outs_c[0] + outs_c[1], cb[0], cb[1]) if need_ctx else None
    return y_x, y_c


def fourier_mixer(h, w_out):
    B, T, D = h.shape
    hg = h.astype(jnp.float32).reshape(B, T, F_GROUPS, D // F_GROUPS)
    mixed = jnp.fft.fft2(hg, axes=(1, 3), norm="ortho").real
    return mixed.reshape(B, T, D).astype(h.dtype) @ w_out


def sq_relu_mlp(h, w1, w2):
    return jnp.square(jax.nn.relu(h @ w1)) @ w2


def setup_inputs(seed: int = 0) -> dict:
    key = jax.random.key(seed)
    ks = jax.random.split(key, 24)
    nrm = lambda k, shape, s: jax.random.normal(k, shape, jnp.float32) * s
    b_gate = nrm(ks[14], (N_A, 2, 2, M_HEADS), 0.1)
    b_gate = b_gate.at[:, :, 1, :].add(jnp.linspace(3.0, 6.0, M_HEADS, dtype=jnp.float32))
    return {
        "x": nrm(ks[0], (BATCH, SEQ, D_MODEL), 1.0),
        "c": nrm(ks[1], (BATCH, D_MODEL), 1.0),
        "ctx": nrm(ks[2], (BATCH, CTX_LEN, D_MODEL), 1.0),
        "c_ctx": nrm(ks[3], (D_MODEL,), 1.0),
        "ada_w": nrm(ks[4], (DEPTH, D_MODEL, 6 * D_MODEL), D_MODEL ** -0.5),
        "ada_b": nrm(ks[5], (DEPTH, 6 * D_MODEL), 0.02),
        "post_g": 1.0 + nrm(ks[6], (DEPTH, 2, D_MODEL), 0.02),
        "post_b": nrm(ks[7], (DEPTH, 2, D_MODEL), 0.02),
        "m_w_in": nrm(ks[8], (N_A, D_MODEL, 2 * D_INNER), D_MODEL ** -0.5),
        "m_w_conv": nrm(ks[9], (N_A, CONV_K, CONV_K, D_INNER), 1.0 / CONV_K),
        "m_b_conv": nrm(ks[10], (N_A, D_INNER), 0.02),
        "m_w_q": nrm(ks[11], (N_A, M_HEADS, M_HEAD_DIM, M_HEAD_DIM), M_HEAD_DIM ** -0.5),
        "m_w_k": nrm(ks[12], (N_A, M_HEADS, M_HEAD_DIM, M_HEAD_DIM), M_HEAD_DIM ** -0.5),
        "m_w_v": nrm(ks[13], (N_A, M_HEADS, M_HEAD_DIM, M_HEAD_DIM), M_HEAD_DIM ** -0.5),
        "m_w_gate": nrm(ks[15], (N_A, 3, D_INNER, 4 * M_HEADS), (3 * D_INNER) ** -0.5),
        "m_b_gate": b_gate.reshape(N_A, 4 * M_HEADS),
        "m_gn_w": 1.0 + nrm(ks[16], (N_A, D_INNER), 0.02),
        "m_skip": 1.0 + nrm(ks[17], (N_A, D_INNER), 0.02),
        "m_w_out": nrm(ks[18], (N_A, D_INNER, D_MODEL), BETA * D_INNER ** -0.5),
        "f_w_out": nrm(ks[19], (N_B, D_MODEL, D_MODEL), BETA * D_MODEL ** -0.5),
        "mlp_w1": nrm(ks[20], (DEPTH, D_MODEL, D_FF), D_MODEL ** -0.5),
        "mlp_w2": nrm(ks[21], (DEPTH, D_FF, D_MODEL), BETA * D_FF ** -0.5),
    }


def reference(x, c, ctx, c_ctx, ada_w, ada_b, post_g, post_b, m_w_in, m_w_conv, m_b_conv, m_w_q, m_w_k,
              m_w_v, m_w_gate, m_b_gate, m_gn_w, m_skip, m_w_out, f_w_out, mlp_w1, mlp_w2):
    s_lat = jax.nn.silu(c)
    s_ctx = jax.nn.silu(c_ctx)
    for i in range(DEPTH):
        is_mlstm = i % N_MIXERS == 0
        j = i // N_MIXERS
        need_ctx = i < DEPTH - 1
        use_ctx = need_ctx or is_mlstm
        sh1, sc1, g1, sh2, sc2, g2 = jnp.split((s_lat @ ada_w[i] + ada_b[i])[:, None, :], 6, axis=-1)
        h_x = modulate(x, sh1, sc1)
        h_c = None
        if use_ctx:
            csh1, csc1, cg1, csh2, csc2, cg2 = jnp.split(s_ctx @ ada_w[i] + ada_b[i], 6, axis=-1)
            h_c = modulate(ctx, csh1, csc1)
        if is_mlstm:
            y_x, y_c = mlstm_mixer(h_x, h_c, need_ctx, m_w_in[j], m_w_conv[j], m_b_conv[j], m_w_q[j], m_w_k[j],
                                   m_w_v[j], m_w_gate[j], m_b_gate[j], m_gn_w[j], m_skip[j], m_w_out[j])
        else:
            y_x = fourier_mixer(h_x, f_w_out[j])
            y_c = fourier_mixer(h_c, f_w_out[j]) if need_ctx else None
        x = post_norm(ALPHA * x + g1 * y_x, post_g[i, 0], post_b[i, 0])
        x = post_norm(ALPHA * x + g2 * sq_relu_mlp(modulate(x, sh2, sc2), mlp_w1[i], mlp_w2[i]),
                      post_g[i, 1], post_b[i, 1])
        if need_ctx:
            ctx = post_norm(ALPHA * ctx + cg1 * y_c, post_g[i, 0], post_b[i, 0])
            ctx = post_norm(ALPHA * ctx + cg2 * sq_relu_mlp(modulate(ctx, csh2, csc2), mlp_w1[i], mlp_w2[i]),
                            post_g[i, 1], post_b[i, 1])
    return x
```

```python
import functools
import math

import numpy as np
import jax
import jax.numpy as jnp
from jax import lax
from jax.experimental import pallas as pl
from jax.experimental.pallas import tpu as pltpu

D_MODEL = 1024
DEPTH = 4
GRID_W = 64
CTX_LEN = 256
N_MIXERS = 2
D_INNER = 2 * D_MODEL
M_HEADS = 4
M_HEAD_DIM = D_INNER // M_HEADS
CONV_K = 3
F_GROUPS = 4
F_GROUP_DIM = D_MODEL // F_GROUPS
D_FF = 4 * D_MODEL
ALPHA = float((2 * DEPTH) ** 0.25)
LN_EPS = 1e-5

TM = 256
N_GATES = 4 * M_HEADS
MOD_ROWS = 8
FFT_T1 = 64
FFT_T2 = 128
FFT_CB = 128
VMEM_LIMIT = 52 * 1024 * 1024

F32 = jnp.float32
BF16 = jnp.bfloat16


def _params(*sem):
    return pltpu.CompilerParams(dimension_semantics=sem, vmem_limit_bytes=VMEM_LIMIT)


def _ln(x):
    mu = jnp.mean(x, axis=-1, keepdims=True)
    xc = x - mu
    var = jnp.mean(xc * xc, axis=-1, keepdims=True)
    return xc * lax.rsqrt(var + LN_EPS)


def _silu(x):
    return x / (1.0 + jnp.exp(-x))


def _dot(a, b):
    return jnp.dot(a, b, preferred_element_type=F32)


def _mod_chunk(m, j):
    return m[:, j * D_MODEL:(j + 1) * D_MODEL]


def _const_spec(shape):
    nd = len(shape)
    return pl.BlockSpec(shape, lambda *_: (0,) * nd, pipeline_mode=pl.Buffered(1))


def _mod_spec(n_batch, ctx_tile):
    if ctx_tile:
        return pl.BlockSpec((1, 1, 6 * D_MODEL), lambda b, t: (jnp.where(t == 0, n_batch, b), 0, 0))
    return pl.BlockSpec((1, 1, 6 * D_MODEL), lambda b, t: (b, 0, 0))


def _tile_spec(width, off=0):
    return pl.BlockSpec((1, TM, width), lambda b, t: (b, t + off, 0))


def _ada_kernel(s_ref, w_ref, b_ref, o_ref):
    s = _silu(s_ref[...])
    o_ref[0] = jnp.dot(s, w_ref[0], preferred_element_type=F32, precision=lax.Precision.HIGHEST) + b_ref[0]


def _ada_table(s_in, ada_w, ada_b):
    n_chunks = 6
    return pl.pallas_call(
        _ada_kernel,
        out_shape=jax.ShapeDtypeStruct((DEPTH, MOD_ROWS, 6 * D_MODEL), F32),
        grid=(DEPTH, n_chunks),
        in_specs=[pl.BlockSpec((MOD_ROWS, D_MODEL), lambda l, j: (0, 0)),
                  pl.BlockSpec((1, D_MODEL, D_MODEL), lambda l, j: (l, 0, j)),
                  pl.BlockSpec((1, 1, D_MODEL), lambda l, j: (l, 0, j))],
        out_specs=pl.BlockSpec((1, MOD_ROWS, D_MODEL), lambda l, j: (l, 0, j)),
        compiler_params=_params("parallel", "parallel"),
        name="ada_table",
    )(s_in, ada_w, ada_b.reshape(DEPTH, 1, 6 * D_MODEL))


def _in_kernel(x_ref, mod_ref, w_ref, xm_ref, z_ref):
    m = mod_ref[0]
    h = (_ln(x_ref[0]) * (1.0 + _mod_chunk(m, 1)) + _mod_chunk(m, 0)).astype(BF16)
    for j in range(2):
        sl = slice(j * D_MODEL, (j + 1) * D_MODEL)
        xm_ref[0, :, sl] = _dot(h, w_ref[:, sl]).astype(BF16)
        z_ref[0, :, sl] = _dot(h, w_ref[:, D_INNER + j * D_MODEL:D_INNER + (j + 1) * D_MODEL]).astype(BF16)


def _in_proj(xs, mod_l, w_in):
    n_batch, rows, _ = xs.shape
    out = jax.ShapeDtypeStruct((n_batch, rows, D_INNER), BF16)
    return pl.pallas_call(
        _in_kernel,
        out_shape=(out, out),
        grid=(n_batch, rows // TM),
        in_specs=[_tile_spec(D_MODEL), _mod_spec(n_batch, True), _const_spec((D_MODEL, 2 * D_INNER))],
        out_specs=(_tile_spec(D_INNER), _tile_spec(D_INNER)),
        compiler_params=_params("parallel", "parallel"),
        name="mlstm_in_proj",
    )(xs, mod_l, w_in)


def _conv_qkv_kernel(xm_ref, prev_ref, next_ref, wc_ref, bc_ref, wq_ref, wk_ref, wv_ref, wg_ref, bg_ref,
                     xc_ref, q_ref, k_ref, v_ref, pre_ref):
    t = pl.program_id(1)
    n_t = pl.num_programs(1)
    is_ctx = t == 0
    ext = TM + 2 * GRID_W
    row = lax.broadcasted_iota(jnp.int32, (ext, 1), 0) - GRID_W
    period = jnp.where(is_ctx, CTX_LEN - 1, GRID_W - 1)
    pos = row & period
    has_left = pos != 0
    has_right = pos != period
    prev_ok = t >= 2
    next_ok = jnp.logical_and(t >= 1, t <= n_t - 2)
    row_w = jnp.where(is_ctx, 0.0, 1.0)
    pre = jnp.zeros((TM, N_GATES), F32) + bg_ref[...]
    for h in range(M_HEADS):
        sl = slice(h * M_HEAD_DIM, (h + 1) * M_HEAD_DIM)
        xm = xm_ref[0, :, sl]
        e = jnp.concatenate([jnp.where(prev_ok, prev_ref[0, :, sl], 0).astype(F32),
                             xm.astype(F32),
                             jnp.where(next_ok, next_ref[0, :, sl], 0).astype(F32)], axis=0)
        el = jnp.where(has_left, pltpu.roll(e, 1, axis=0), 0.0)
        er = jnp.where(has_right, pltpu.roll(e, ext - 1, axis=0), 0.0)
        acc = jnp.zeros((TM, M_HEAD_DIM), F32) + bc_ref[:, sl]
        for dr in range(CONV_K):
            lo = dr * GRID_W
            scale = 1.0 if dr == CONV_K // 2 else row_w
            w0 = wc_ref[3 * dr + 0:3 * dr + 1, sl] * scale
            w1 = wc_ref[3 * dr + 1:3 * dr + 2, sl] * scale
            w2 = wc_ref[3 * dr + 2:3 * dr + 3, sl] * scale
            acc = acc + w0 * el[lo:lo + TM] + w1 * e[lo:lo + TM] + w2 * er[lo:lo + TM]
        xc = _silu(acc).astype(BF16)
        xc_ref[0, :, sl] = xc
        q = _dot(xc, wq_ref[h]).astype(BF16)
        k = (_dot(xc, wk_ref[h]) * (M_HEAD_DIM ** -0.5)).astype(BF16)
        v = _dot(xm, wv_ref[h]).astype(BF16)
        q_ref[0, :, sl] = q
        k_ref[0, :, sl] = k
        v_ref[0, :, sl] = v
        pre = pre + _dot(jnp.concatenate([q, k, v], axis=1), wg_ref[h])
    pre_ref[0] = pre


def _conv_qkv(xm, w_conv, b_conv, w_q, w_k, w_v, w_g, b_g):
    n_batch, rows, _ = xm.shape
    n_halo = rows // GRID_W
    per = TM // GRID_W
    wide = jax.ShapeDtypeStruct((n_batch, rows, D_INNER), BF16)
    return pl.pallas_call(
        _conv_qkv_kernel,
        out_shape=(wide, wide, wide, wide, jax.ShapeDtypeStruct((n_batch, rows, N_GATES), F32)),
        grid=(n_batch, rows // TM),
        in_specs=[_tile_spec(D_INNER),
                  pl.BlockSpec((1, GRID_W, D_INNER), lambda b, t: (b, jnp.maximum(per * t - 1, 0), 0)),
                  pl.BlockSpec((1, GRID_W, D_INNER), lambda b, t: (b, jnp.minimum(per * t + per, n_halo - 1), 0)),
                  _const_spec((CONV_K * CONV_K, D_INNER)),
                  _const_spec((1, D_INNER)),
                  _const_spec((M_HEADS, M_HEAD_DIM, M_HEAD_DIM)),
                  _const_spec((M_HEADS, M_HEAD_DIM, M_HEAD_DIM)),
                  _const_spec((M_HEADS, M_HEAD_DIM, M_HEAD_DIM)),
                  _const_spec((M_HEADS, 3 * M_HEAD_DIM, N_GATES)),
                  _const_spec((1, N_GATES))],
        out_specs=(_tile_spec(D_INNER), _tile_spec(D_INNER), _tile_spec(D_INNER), _tile_spec(D_INNER),
                   _tile_spec(N_GATES)),
        compiler_params=_params("parallel", "parallel"),
        name="mlstm_conv_qkv",
    )(xm, xm, xm, w_conv, b_conv, w_q, w_k, w_v, w_g, b_g)


def _scan_lanes(x, combine, fill, reverse):
    n = x.shape[-1]
    lane = lax.broadcasted_iota(jnp.int32, x.shape, 1)
    s = 1
    while s < n:
        if reverse:
            sh = jnp.where(lane < n - s, pltpu.roll(x, n - s, axis=1), fill)
        else:
            sh = jnp.where(lane >= s, pltpu.roll(x, s, axis=1), fill)
        x = combine(x, sh)
        s *= 2
    return x


def _gate_kernel(pre_ref, o_ref):
    for d in range(2):
        blk = pre_ref[0, 2 * M_HEADS * d:2 * M_HEADS * (d + 1)]
        log_f = jnp.minimum(blk, 0.0) - jnp.log1p(jnp.exp(-jnp.abs(blk)))
        b = pltpu.roll(_scan_lanes(log_f, jnp.add, 0.0, reverse=d == 1), M_HEADS, axis=0)
        g = blk - b
        o_ref[0, 0, d] = b
        o_ref[0, 1, d] = g
        o_ref[0, 2, d] = _scan_lanes(g, jnp.maximum, -jnp.inf, reverse=d == 1)


def _gate_scans(pre_t):
    n_batch, _, rows = pre_t.shape
    out = pl.pallas_call(
        _gate_kernel,
        out_shape=jax.ShapeDtypeStruct((n_batch, 3, 2, 2 * M_HEADS, rows), F32),
        grid=(n_batch, rows // TM),
        in_specs=[pl.BlockSpec((1, N_GATES, TM), lambda b, t: (b, 0, t))],
        out_specs=pl.BlockSpec((1, 3, 2, 2 * M_HEADS, TM), lambda b, t: (b, 0, 0, 0, t)),
        compiler_params=_params("parallel", "parallel"),
        name="mlstm_gate_scans",
    )(pre_t)
    return out[:, :, :, :M_HEADS, :].reshape(n_batch, 3, 2 * M_HEADS, rows)


def _scan_kernel(q_ref, k_ref, v_ref, grow_ref, col_ref, h_ref, ct_ref, n_ref, m_ref):
    d = pl.program_id(2)
    c = pl.program_id(3)

    @pl.when(c == 0)
    def _():
        ct_ref[...] = jnp.zeros_like(ct_ref)
        n_ref[...] = jnp.zeros_like(n_ref)
        m_ref[...] = jnp.zeros_like(m_ref)

    q = q_ref[0]
    k = k_ref[0]
    v = v_ref[0]
    g_row = grow_ref[0, 0]
    cols = col_ref[0, 0]
    b_col = cols[:, 0:1]
    g_col = cols[:, 1:2]
    big_col = cols[:, 2:3]
    m_prev = m_ref[0:1, 0:1]
    sign = 1 - 2 * d
    ii = lax.broadcasted_iota(jnp.int32, (TM, TM), 0)
    jj = lax.broadcasted_iota(jnp.int32, (TM, TM), 1)
    visible = (ii - jj) * sign >= 0

    m_col = jnp.maximum(m_prev, big_col)
    dmat = jnp.where(visible, jnp.exp(g_row - m_col), 0.0)
    w_inter = jnp.exp(m_prev - m_col)
    s = lax.dot_general(q, k, (((1,), (1,)), ((), ())), preferred_element_type=F32) * dmat
    inter = _dot(q, ct_ref[...].astype(BF16))
    num = w_inter * inter + _dot(s.astype(BF16), v)
    qn = jnp.sum(q.astype(F32) * n_ref[...], axis=-1, keepdims=True)
    den = w_inter * qn + jnp.sum(s, axis=-1, keepdims=True)
    h = num / jnp.maximum(jnp.abs(den), jnp.exp(-(b_col + m_col)))
    h_ref[0, 0] = h.astype(BF16)

    last = jnp.where(d == 0, cols[TM - 1:TM, :], cols[0:1, :])
    m_last = jnp.maximum(m_prev, last[:, 2:3])
    decay = jnp.exp(m_prev - m_last)
    kw = k.astype(F32) * jnp.exp(g_col - m_last)
    ct_ref[...] = decay * ct_ref[...] + lax.dot_general(
        kw.astype(BF16), v, (((0,), (0,)), ((), ())), preferred_element_type=F32)
    n_ref[...] = decay * n_ref[...] + jnp.sum(kw, axis=0, keepdims=True)
    m_ref[...] = jnp.zeros_like(m_ref) + (last[:, 0:1] + m_last)


def _mlstm_scan(q, k, v, g_rows, g_cols):
    n_batch, rows, _ = q.shape
    n_t = rows // TM

    def tile(d, c):
        return jnp.where(d == 0, c, jnp.where(c == 0, 0, n_t - c))

    qkv_spec = pl.BlockSpec((1, TM, M_HEAD_DIM), lambda b, h, d, c: (b, tile(d, c), h))
    return pl.pallas_call(
        _scan_kernel,
        out_shape=jax.ShapeDtypeStruct((2, n_batch, rows, D_INNER), BF16),
        grid=(n_batch, M_HEADS, 2, n_t),
        in_specs=[qkv_spec, qkv_spec, qkv_spec,
                  pl.BlockSpec((1, 1, 1, TM), lambda b, h, d, c: (b, d * M_HEADS + h, 0, tile(d, c))),
                  pl.BlockSpec((1, 1, TM, 3), lambda b, h, d, c: (b, d * M_HEADS + h, tile(d, c), 0))],
        out_specs=pl.BlockSpec((1, 1, TM, M_HEAD_DIM), lambda b, h, d, c: (d, b, tile(d, c), h)),
        scratch_shapes=[pltpu.VMEM((M_HEAD_DIM, M_HEAD_DIM), F32),
                        pltpu.VMEM((1, M_HEAD_DIM), F32),
                        pltpu.VMEM((8, 128), F32)],
        compiler_params=_params("parallel", "parallel", "parallel", "arbitrary"),
        name="mlstm_scan",
    )(q, k, v, g_rows, g_cols)


def _post_norm(x, y, gate, pg, pb):
    return _ln(ALPHA * x + gate * y) * pg + pb


def _readout_kernel(hf_ref, hb_ref, xc_ref, z_ref, x_ref, mod_ref, gn_ref, sk_ref, w_ref, pg_ref, pb_ref, o_ref):
    parts = []
    for h in range(M_HEADS):
        sl = slice(h * M_HEAD_DIM, (h + 1) * M_HEAD_DIM)
        hn = _ln(hf_ref[0, 0, :, sl].astype(F32) + hb_ref[0, 0, :, sl].astype(F32))
        a = hn * gn_ref[:, sl] + sk_ref[:, sl] * xc_ref[0, :, sl].astype(F32)
        parts.append((a * _silu(z_ref[0, :, sl].astype(F32))).astype(BF16))
    y = _dot(jnp.concatenate(parts, axis=1), w_ref[...])
    o_ref[0] = _post_norm(x_ref[0], y, _mod_chunk(mod_ref[0], 2), pg_ref[...], pb_ref[...])


def _mlstm_readout(h2, xc, z, xs, mod_l, gn_w, skip, w_out, pg, pb):
    n_batch, rows, _ = xs.shape
    return pl.pallas_call(
        _readout_kernel,
        out_shape=jax.ShapeDtypeStruct((n_batch, rows, D_MODEL), F32),
        grid=(n_batch, rows // TM),
        in_specs=[pl.BlockSpec((1, 1, TM, D_INNER), lambda b, t: (0, b, t, 0)),
                  pl.BlockSpec((1, 1, TM, D_INNER), lambda b, t: (1, b, t, 0)),
                  _tile_spec(D_INNER), _tile_spec(D_INNER), _tile_spec(D_MODEL), _mod_spec(n_batch, True),
                  _const_spec((1, D_INNER)), _const_spec((1, D_INNER)), _const_spec((D_INNER, D_MODEL)),
                  _const_spec((1, D_MODEL)), _const_spec((1, D_MODEL))],
        out_specs=_tile_spec(D_MODEL),
        compiler_params=_params("parallel", "parallel"),
        name="mlstm_readout",
    )(h2, h2, xc, z, xs, mod_l, gn_w, skip, w_out, pg, pb)


def _fourier_out_kernel(ctx_tile, *refs):
    if ctx_tile:
        mx_ref, mc_ref, x_ref, mod_ref, w_ref, pg_ref, pb_ref, o_ref = refs
        a = jnp.where(pl.program_id(1) == 0, mc_ref[0], mx_ref[0])
    else:
        mx_ref, x_ref, mod_ref, w_ref, pg_ref, pb_ref, o_ref = refs
        a = mx_ref[0]
    y = _dot(a.astype(BF16), w_ref[...])
    o_ref[0] = _post_norm(x_ref[0], y, _mod_chunk(mod_ref[0], 2), pg_ref[...], pb_ref[...])


def _fourier_out(mixed_x, mixed_c, xs, mod_l, w_out, pg, pb):
    n_batch = xs.shape[0]
    ctx_tile = mixed_c is not None
    n_t = mixed_x.shape[1] // TM + (1 if ctx_tile else 0)
    in_specs = [pl.BlockSpec((1, TM, D_MODEL), lambda b, t: (b, jnp.maximum(t - 1, 0) if ctx_tile else t, 0))]
    args = [mixed_x]
    if ctx_tile:
        in_specs.append(pl.BlockSpec((1, CTX_LEN, D_MODEL), lambda b, t: (b, 0, 0)))
        args.append(mixed_c)
    in_specs += [_tile_spec(D_MODEL, 0 if ctx_tile else CTX_LEN // TM), _mod_spec(n_batch, ctx_tile),
                 _const_spec((D_MODEL, D_MODEL)), _const_spec((1, D_MODEL)), _const_spec((1, D_MODEL))]
    return pl.pallas_call(
        functools.partial(_fourier_out_kernel, ctx_tile),
        out_shape=jax.ShapeDtypeStruct((n_batch, n_t * TM, D_MODEL), F32),
        grid=(n_batch, n_t),
        in_specs=in_specs,
        out_specs=_tile_spec(D_MODEL),
        compiler_params=_params("parallel", "parallel"),
        name="fourier_out",
    )(*args, xs, mod_l, w_out, pg, pb)


def _mlp_kernel(x_ref, mod_ref, w1_ref, w2_ref, pg_ref, pb_ref, o_ref):
    x = x_ref[0]
    m = mod_ref[0]
    h = (_ln(x) * (1.0 + _mod_chunk(m, 4)) + _mod_chunk(m, 3)).astype(BF16)
    acc = jnp.zeros((TM, D_MODEL), F32)
    for j in range(D_FF // D_MODEL):
        sl = slice(j * D_MODEL, (j + 1) * D_MODEL)
        u = jnp.square(jnp.maximum(_dot(h, w1_ref[:, sl]), 0.0)).astype(BF16)
        acc = acc + _dot(u, w2_ref[sl, :])
    o_ref[0] = _post_norm(x, acc, _mod_chunk(m, 5), pg_ref[...], pb_ref[...])


def _mlp(xs, mod_l, w1, w2, pg, pb, ctx_tile):
    n_batch, rows, _ = xs.shape
    return pl.pallas_call(
        _mlp_kernel,
        out_shape=jax.ShapeDtypeStruct((n_batch, rows, D_MODEL), F32),
        grid=(n_batch, rows // TM),
        in_specs=[_tile_spec(D_MODEL), _mod_spec(n_batch, ctx_tile),
                  _const_spec((D_MODEL, D_FF)), _const_spec((D_FF, D_MODEL)),
                  _const_spec((1, D_MODEL)), _const_spec((1, D_MODEL))],
        out_specs=_tile_spec(D_MODEL),
        compiler_params=_params("parallel", "parallel"),
        name="mlp",
    )(xs, mod_l, w1, w2, pg, pb)


def _dft_tables():
    two_pi = 2.0 * np.pi
    c = np.arange(F_GROUP_DIM)
    ang = two_pi * ((c[:, None] * c[None, :]) % F_GROUP_DIM) / F_GROUP_DIM
    s_ch = 1.0 / math.sqrt(F_GROUP_DIM)
    f_ch = np.concatenate([np.cos(ang) * s_ch, -np.sin(ang) * s_ch], axis=1)

    seq = FFT_T1 * FFT_T2
    t1 = np.arange(FFT_T1)
    t2 = np.arange(FFT_T2)
    num = (t1[None, :, None] * t1[None, None, :] * FFT_T2 + t2[:, None, None] * t1[None, :, None]) % seq
    ang = two_pi * num / seq
    s1 = 1.0 / math.sqrt(FFT_T1)
    gr, gi = np.cos(ang) * s1, -np.sin(ang) * s1
    g1 = np.concatenate([np.concatenate([gr, -gi], axis=2), np.concatenate([gi, gr], axis=2)], axis=1)

    ang = two_pi * ((t2[:, None] * t2[None, :]) % FFT_T2) / FFT_T2
    s2 = 1.0 / math.sqrt(FFT_T2)
    f2 = np.concatenate([np.cos(ang) * s2, np.sin(ang) * s2], axis=1)

    p = np.arange(CTX_LEN)
    ang = two_pi * ((p[:, None] * p[None, :]) % CTX_LEN) / CTX_LEN
    sc = 1.0 / math.sqrt(CTX_LEN)
    f_ctx = np.concatenate([np.cos(ang) * sc, np.sin(ang) * sc], axis=1)
    return tuple(jnp.asarray(a, dtype=F32) for a in (f_ch, g1, f2, f_ctx))


def _cdft_kernel(ctx_tile, *refs):
    if ctx_tile:
        x_ref, mod_ref, f_ref, zrx_ref, zix_ref, zrc_ref, zic_ref = refs
    else:
        x_ref, mod_ref, f_ref, zrx_ref, zix_ref = refs
    m = mod_ref[0]
    h = (_ln(x_ref[0]) * (1.0 + _mod_chunk(m, 1)) + _mod_chunk(m, 0)).astype(BF16)
    f = f_ref[...].astype(BF16)
    zr, zi = [], []
    for g in range(F_GROUPS):
        z = _dot(h[:, g * F_GROUP_DIM:(g + 1) * F_GROUP_DIM], f).astype(BF16)
        zr.append(z[:, :F_GROUP_DIM])
        zi.append(z[:, F_GROUP_DIM:])
    zr = jnp.concatenate(zr, axis=1)
    zi = jnp.concatenate(zi, axis=1)
    if ctx_tile:
        t = pl.program_id(1)

        @pl.when(t == 0)
        def _():
            zrc_ref[0] = zr
            zic_ref[0] = zi

        @pl.when(t > 0)
        def _():
            zrx_ref[0] = zr
            zix_ref[0] = zi
    else:
        zrx_ref[0] = zr
        zix_ref[0] = zi


def _channel_dft(xs, mod_l, f_ch, ctx_tile):
    n_batch, rows, _ = xs.shape
    seq = rows - CTX_LEN
    off = CTX_LEN // TM
    zx = jax.ShapeDtypeStruct((n_batch, seq, D_MODEL), BF16)
    zc = jax.ShapeDtypeStruct((n_batch, CTX_LEN, D_MODEL), BF16)
    if ctx_tile:
        n_t = rows // TM
        x_out = pl.BlockSpec((1, TM, D_MODEL), lambda b, t: (b, jnp.maximum(t - off, 0), 0))
        c_out = pl.BlockSpec((1, CTX_LEN, D_MODEL), lambda b, t: (b, 0, 0))
        out_shape, out_specs, x_in = (zx, zx, zc, zc), (x_out, x_out, c_out, c_out), _tile_spec(D_MODEL)
    else:
        n_t = seq // TM
        out_shape, out_specs, x_in = (zx, zx), (_tile_spec(D_MODEL), _tile_spec(D_MODEL)), _tile_spec(D_MODEL, off)
    return pl.pallas_call(
        functools.partial(_cdft_kernel, ctx_tile),
        out_shape=out_shape,
        grid=(n_batch, n_t),
        in_specs=[x_in, _mod_spec(n_batch, ctx_tile), _const_spec((F_GROUP_DIM, 2 * F_GROUP_DIM))],
        out_specs=out_specs,
        compiler_params=_params("parallel", "arbitrary"),
        name="fourier_channel_dft",
    )(xs, mod_l, f_ch)


def _seq_dft_kernel(zr_ref, zi_ref, g1_ref, f2_ref, o_ref, sr_ref, si_ref):
    sr_ref[...] = zr_ref[0].astype(F32)
    si_ref[...] = zi_ref[0].astype(F32)

    def stage1(t2, carry):
        rows = pl.ds(t2, FFT_T1, stride=FFT_T2)
        z = jnp.concatenate([sr_ref[rows, :], si_ref[rows, :]], axis=0).astype(BF16)
        v = _dot(g1_ref[t2].astype(BF16), z)
        sr_ref[rows, :] = v[:FFT_T1]
        si_ref[rows, :] = v[FFT_T1:]
        return carry

    lax.fori_loop(0, FFT_T2, stage1, 0, unroll=4)

    def stage2(t1, carry):
        rows = pl.ds(pl.multiple_of(t1 * FFT_T2, FFT_T2), FFT_T2)
        v = jnp.concatenate([sr_ref[rows, :], si_ref[rows, :]], axis=0).astype(BF16)
        o_ref[0, pl.ds(t1, FFT_T2, stride=FFT_T1), :] = _dot(f2_ref[...].astype(BF16), v)
        return carry

    lax.fori_loop(0, FFT_T1, stage2, 0, unroll=4)


def _seq_dft(zr, zi, g1, f2):
    n_batch, seq, _ = zr.shape
    blk = pl.BlockSpec((1, seq, FFT_CB), lambda b, j: (b, 0, j))
    return pl.pallas_call(
        _seq_dft_kernel,
        out_shape=jax.ShapeDtypeStruct((n_batch, seq, D_MODEL), F32),
        grid=(n_batch, D_MODEL // FFT_CB),
        in_specs=[blk, blk, _const_spec((FFT_T2, 2 * FFT_T1, 2 * FFT_T1)), _const_spec((FFT_T2, 2 * FFT_T2))],
        out_specs=blk,
        scratch_shapes=[pltpu.VMEM((seq, FFT_CB), F32), pltpu.VMEM((seq, FFT_CB), F32)],
        compiler_params=_params("parallel", "parallel"),
        name="fourier_seq_dft",
    )(zr, zi, g1, f2)


def _ctx_dft_kernel(zr_ref, zi_ref, f_ref, o_ref):
    o_ref[0] = _dot(f_ref[...].astype(BF16), jnp.concatenate([zr_ref[0], zi_ref[0]], axis=0))


def _ctx_dft(zr, zi, f_ctx):
    n_batch = zr.shape[0]
    blk = pl.BlockSpec((1, CTX_LEN, D_MODEL), lambda b: (b, 0, 0))
    return pl.pallas_call(
        _ctx_dft_kernel,
        out_shape=jax.ShapeDtypeStruct((n_batch, CTX_LEN, D_MODEL), F32),
        grid=(n_batch,),
        in_specs=[blk, blk, _const_spec((CTX_LEN, 2 * CTX_LEN))],
        out_specs=blk,
        compiler_params=_params("parallel"),
        name="fourier_ctx_dft",
    )(zr, zi, f_ctx)


def _mlstm_layer(xs, mod_l, w_in, w_conv, b_conv, w_q, w_k, w_v, w_gate, b_gate, gn_w, skip, w_out, pg, pb):
    n_batch, rows, _ = xs.shape
    xm, z = _in_proj(xs, mod_l, w_in.astype(BF16))
    wg = w_gate.reshape(3, M_HEADS, M_HEAD_DIM, N_GATES).transpose(1, 0, 2, 3).reshape(
        M_HEADS, 3 * M_HEAD_DIM, N_GATES)
    xc, q, k, v, pre = _conv_qkv(
        xm, w_conv.reshape(CONV_K * CONV_K, D_INNER), b_conv.reshape(1, D_INNER),
        w_q.astype(BF16), w_k.astype(BF16), w_v.astype(BF16), wg.astype(BF16), b_gate.reshape(1, N_GATES))
    scans = _gate_scans(jnp.transpose(pre, (0, 2, 1)))
    g_rows = scans[:, 1].reshape(n_batch, 2 * M_HEADS, 1, rows)
    g_cols = jnp.transpose(scans, (0, 2, 3, 1))
    h2 = _mlstm_scan(q, k, v, g_rows, g_cols)
    return _mlstm_readout(h2, xc, z, xs, mod_l, gn_w.reshape(1, D_INNER), skip.reshape(1, D_INNER),
                          w_out.astype(BF16), pg, pb)


def _fourier_layer(xs, mod_l, w_out, pg, pb, need_ctx, tables):
    f_ch, g1, f2, f_ctx = tables
    z = _channel_dft(xs, mod_l, f_ch, need_ctx)
    mixed_x = _seq_dft(z[0], z[1], g1, f2)
    mixed_c = _ctx_dft(z[2], z[3], f_ctx) if need_ctx else None
    return _fourier_out(mixed_x, mixed_c, xs, mod_l, w_out.astype(BF16), pg, pb)


def kernel(x, c, ctx, c_ctx, ada_w, ada_b, post_g, post_b, m_w_in, m_w_conv, m_b_conv, m_w_q, m_w_k, m_w_v, m_w_gate, m_b_gate, m_gn_w, m_skip, m_w_out, f_w_out, mlp_w1, mlp_w2):
    n_batch = x.shape[0]
    assert n_batch < MOD_ROWS and x.shape[1] == FFT_T1 * FFT_T2 and ctx.shape[1] == CTX_LEN == TM
    xs = jnp.concatenate([ctx, x], axis=1)
    s_in = jnp.concatenate([c, c_ctx[None, :], jnp.zeros((MOD_ROWS - n_batch - 1, D_MODEL), F32)], axis=0)
    mod = _ada_table(s_in, ada_w, ada_b)
    tables = _dft_tables()
    for i in range(DEPTH):
        is_mlstm = i % N_MIXERS == 0
        j = i // N_MIXERS
        need_ctx = i < DEPTH - 1
        mod_l = mod[i].reshape(MOD_ROWS, 1, 6 * D_MODEL)
        pg = post_g[i].reshape(2, 1, D_MODEL)
        pb = post_b[i].reshape(2, 1, D_MODEL)
        if is_mlstm:
            xs = _mlstm_layer(xs, mod_l, m_w_in[j], m_w_conv[j], m_b_conv[j], m_w_q[j], m_w_k[j], m_w_v[j],
                              m_w_gate[j], m_b_gate[j], m_gn_w[j], m_skip[j], m_w_out[j], pg[0], pb[0])
            if not need_ctx:
                xs = xs[:, CTX_LEN:]
        else:
            xs = _fourier_layer(xs, mod_l, f_w_out[j], pg[0], pb[0], need_ctx, tables)
        xs = _mlp(xs, mod_l, mlp_w1[i].astype(BF16), mlp_w2[i].astype(BF16), pg[1], pb[1], need_ctx)
    return xs
```

```python
import functools
import math

import numpy as np
import jax
import jax.numpy as jnp
from jax import lax
from jax.experimental import pallas as pl
from jax.experimental.pallas import tpu as pltpu

D_MODEL = 1024
DEPTH = 4
GRID_W = 64
CTX_LEN = 256
N_MIXERS = 2
D_INNER = 2 * D_MODEL
M_HEADS = 4
M_HEAD_DIM = D_INNER // M_HEADS
CONV_K = 3
F_GROUPS = 4
F_GROUP_DIM = D_MODEL // F_GROUPS
D_FF = 4 * D_MODEL
ALPHA = float((2 * DEPTH) ** 0.25)
LN_EPS = 1e-5

TM = 256
N_GATES = 4 * M_HEADS
GATE_ROWS = 8
MOD_ROWS = 8
FFT_T1 = 64
FFT_T2 = 128
FFT_CB = 128
DFT_UNROLL = 8
VMEM_LIMIT = 52 * 1024 * 1024

F32 = jnp.float32
BF16 = jnp.bfloat16


def _params(*sem):
    return pltpu.CompilerParams(dimension_semantics=sem, vmem_limit_bytes=VMEM_LIMIT)


def _ln(x):
    mu = jnp.mean(x, axis=-1, keepdims=True)
    xc = x - mu
    var = jnp.mean(xc * xc, axis=-1, keepdims=True)
    return xc * lax.rsqrt(var + LN_EPS)


def _silu(x):
    return x / (1.0 + jnp.exp(-x))


def _dot(a, b):
    return jnp.dot(a, b, preferred_element_type=F32)


def _mod_chunk(m, j):
    return m[:, j * D_MODEL:(j + 1) * D_MODEL]


def _const_spec(shape):
    nd = len(shape)
    return pl.BlockSpec(shape, lambda *_: (0,) * nd, pipeline_mode=pl.Buffered(1))


def _mod_spec(n_batch, ctx_tile):
    if ctx_tile:
        return pl.BlockSpec((1, 1, 6 * D_MODEL), lambda b, t: (jnp.where(t == 0, n_batch, b), 0, 0))
    return pl.BlockSpec((1, 1, 6 * D_MODEL), lambda b, t: (b, 0, 0))


def _tile_spec(width, off=0):
    return pl.BlockSpec((1, TM, width), lambda b, t: (b, t + off, 0))


def _ada_kernel(s_ref, w_ref, b_ref, o_ref):
    s = _silu(s_ref[...])
    o_ref[0] = jnp.dot(s, w_ref[0], preferred_element_type=F32, precision=lax.Precision.HIGHEST) + b_ref[0]


def _ada_table(s_in, ada_w, ada_b):
    n_chunks = 6
    return pl.pallas_call(
        _ada_kernel,
        out_shape=jax.ShapeDtypeStruct((DEPTH, MOD_ROWS, 6 * D_MODEL), F32),
        grid=(DEPTH, n_chunks),
        in_specs=[pl.BlockSpec((MOD_ROWS, D_MODEL), lambda l, j: (0, 0)),
                  pl.BlockSpec((1, D_MODEL, D_MODEL), lambda l, j: (l, 0, j)),
                  pl.BlockSpec((1, 1, D_MODEL), lambda l, j: (l, 0, j))],
        out_specs=pl.BlockSpec((1, MOD_ROWS, D_MODEL), lambda l, j: (l, 0, j)),
        compiler_params=_params("parallel", "parallel"),
        name="ada_table",
    )(s_in, ada_w, ada_b.reshape(DEPTH, 1, 6 * D_MODEL))


def _in_kernel(x_ref, mod_ref, w_ref, xm_ref, z_ref):
    m = mod_ref[0]
    h = (_ln(x_ref[0]) * (1.0 + _mod_chunk(m, 1)) + _mod_chunk(m, 0)).astype(BF16)
    for j in range(2):
        sl = slice(j * D_MODEL, (j + 1) * D_MODEL)
        xm_ref[0, :, sl] = _dot(h, w_ref[:, sl]).astype(BF16)
        z_ref[0, :, sl] = _dot(h, w_ref[:, D_INNER + j * D_MODEL:D_INNER + (j + 1) * D_MODEL]).astype(BF16)


def _in_proj(xs, mod_l, w_in):
    n_batch, rows, _ = xs.shape
    out = jax.ShapeDtypeStruct((n_batch, rows, D_INNER), BF16)
    return pl.pallas_call(
        _in_kernel,
        out_shape=(out, out),
        grid=(n_batch, rows // TM),
        in_specs=[_tile_spec(D_MODEL), _mod_spec(n_batch, True), _const_spec((D_MODEL, 2 * D_INNER))],
        out_specs=(_tile_spec(D_INNER), _tile_spec(D_INNER)),
        compiler_params=_params("parallel", "parallel"),
        name="mlstm_in_proj",
    )(xs, mod_l, w_in)


def _conv_qkv_kernel(xm_ref, prev_ref, next_ref, wc_ref, bc_ref, wq_ref, wk_ref, wv_ref, wg_ref, bg_ref,
                     xc_ref, q_ref, k_ref, v_ref, pre_ref):
    t = pl.program_id(1)
    n_t = pl.num_programs(1)
    is_ctx = t == 0
    row = lax.broadcasted_iota(jnp.int32, (TM, 1), 0)
    period = jnp.where(is_ctx, CTX_LEN - 1, GRID_W - 1)
    pos = row & period
    has_left = pos != 0
    has_right = pos != period
    prev_ok = t >= 2
    next_ok = jnp.logical_and(t >= 1, t <= n_t - 2)
    row_w = jnp.where(is_ctx, 0.0, 1.0)
    pre_t = jnp.zeros((N_GATES, TM), F32) + bg_ref[...]
    for h in range(M_HEADS):
        sl = slice(h * M_HEAD_DIM, (h + 1) * M_HEAD_DIM)
        xm = xm_ref[0, :, sl]
        e = jnp.concatenate([jnp.where(prev_ok, prev_ref[0, :, sl], 0).astype(F32),
                             xm.astype(F32),
                             jnp.where(next_ok, next_ref[0, :, sl], 0).astype(F32)], axis=0)
        p = []
        for dc in range(CONV_K):
            acc = None
            for dr in range(CONV_K):
                w = wc_ref[CONV_K * dr + dc:CONV_K * dr + dc + 1, sl]
                if dr != CONV_K // 2:
                    w = w * row_w
                term = w * e[dr * GRID_W:dr * GRID_W + TM]
                acc = term if acc is None else acc + term
            p.append(acc)
        y = (p[1] + bc_ref[:, sl]
             + jnp.where(has_left, pltpu.roll(p[0], 1, axis=0), 0.0)
             + jnp.where(has_right, pltpu.roll(p[2], TM - 1, axis=0), 0.0))
        xc = _silu(y).astype(BF16)
        xc_ref[0, :, sl] = xc
        q = _dot(xc, wq_ref[h]).astype(BF16)
        k = (_dot(xc, wk_ref[h]) * (M_HEAD_DIM ** -0.5)).astype(BF16)
        v = _dot(xm, wv_ref[h]).astype(BF16)
        q_ref[0, :, sl] = q
        k_ref[0, :, sl] = k
        v_ref[0, :, sl] = v
        pre_t = pre_t + lax.dot_general(wg_ref[h], jnp.concatenate([q, k, v], axis=1),
                                        (((1,), (1,)), ((), ())), preferred_element_type=F32)
    pre_ref[0] = pre_t


def _conv_qkv(xm, w_conv, b_conv, w_q, w_k, w_v, w_g, b_g):
    n_batch, rows, _ = xm.shape
    n_halo = rows // GRID_W
    per = TM // GRID_W
    wide = jax.ShapeDtypeStruct((n_batch, rows, D_INNER), BF16)
    return pl.pallas_call(
        _conv_qkv_kernel,
        out_shape=(wide, wide, wide, wide, jax.ShapeDtypeStruct((n_batch, N_GATES, rows), F32)),
        grid=(n_batch, rows // TM),
        in_specs=[_tile_spec(D_INNER),
                  pl.BlockSpec((1, GRID_W, D_INNER), lambda b, t: (b, jnp.maximum(per * t - 1, 0), 0)),
                  pl.BlockSpec((1, GRID_W, D_INNER), lambda b, t: (b, jnp.minimum(per * t + per, n_halo - 1), 0)),
                  _const_spec((CONV_K * CONV_K, D_INNER)),
                  _const_spec((1, D_INNER)),
                  _const_spec((M_HEADS, M_HEAD_DIM, M_HEAD_DIM)),
                  _const_spec((M_HEADS, M_HEAD_DIM, M_HEAD_DIM)),
                  _const_spec((M_HEADS, M_HEAD_DIM, M_HEAD_DIM)),
                  _const_spec((M_HEADS, N_GATES, 3 * M_HEAD_DIM)),
                  _const_spec((N_GATES, 1))],
        out_specs=(_tile_spec(D_INNER), _tile_spec(D_INNER), _tile_spec(D_INNER), _tile_spec(D_INNER),
                   pl.BlockSpec((1, N_GATES, TM), lambda b, t: (b, 0, t))),
        compiler_params=_params("parallel", "parallel"),
        name="mlstm_conv_qkv",
    )(xm, xm, xm, w_conv, b_conv, w_q, w_k, w_v, w_g, b_g)


def _scan_lanes(x, combine, fill, reverse):
    n = x.shape[-1]
    lane = lax.broadcasted_iota(jnp.int32, x.shape, 1)
    s = 1
    while s < n:
        if reverse:
            sh = jnp.where(lane < n - s, pltpu.roll(x, n - s, axis=1), fill)
        else:
            sh = jnp.where(lane >= s, pltpu.roll(x, s, axis=1), fill)
        x = combine(x, sh)
        s *= 2
    return x


def _gate_kernel(pre_ref, o_ref):
    zeros = jnp.zeros((GATE_ROWS - 3, TM), F32)
    for d in range(2):
        blk = pre_ref[0, 2 * M_HEADS * d:2 * M_HEADS * (d + 1)]
        log_f = jnp.minimum(blk, 0.0) - jnp.log1p(jnp.exp(-jnp.abs(blk)))
        b = pltpu.roll(_scan_lanes(log_f, jnp.add, 0.0, reverse=d == 1), M_HEADS, axis=0)
        g = blk - b
        big = _scan_lanes(g, jnp.maximum, -jnp.inf, reverse=d == 1)
        for h in range(M_HEADS):
            o_ref[0, d, h] = jnp.concatenate([b[h:h + 1], g[h:h + 1], big[h:h + 1], zeros], axis=0)


def _gate_scans(pre_t):
    n_batch, _, rows = pre_t.shape
    return pl.pallas_call(
        _gate_kernel,
        out_shape=jax.ShapeDtypeStruct((n_batch, 2, M_HEADS, GATE_ROWS, rows), F32),
        grid=(n_batch, rows // TM),
        in_specs=[pl.BlockSpec((1, N_GATES, TM), lambda b, t: (b, 0, t))],
        out_specs=pl.BlockSpec((1, 2, M_HEADS, GATE_ROWS, TM), lambda b, t: (b, 0, 0, 0, t)),
        compiler_params=_params("parallel", "parallel"),
        name="mlstm_gate_scans",
    )(pre_t)


def _scan_chunk(d, q_ref, k_ref, v_ref, gate_ref, h_ref, ct_ref, ctb_ref, n_ref, m_ref):
    q = q_ref[0]
    k = k_ref[0]
    v = v_ref[0]
    gates = gate_ref[0, 0, 0]
    b_row, g_row, big_row = gates[0:1], gates[1:2], gates[2:3]
    m_prev = m_ref[d, 0:1, 0:1]
    last = TM - 1 if d == 0 else 0
    m_row = jnp.maximum(m_prev, big_row)
    m_last = m_row[:, last:last + 1]
    rows = jnp.concatenate([m_row,
                            jnp.exp(m_prev - m_row),
                            jnp.exp(-(b_row + m_row)),
                            jnp.exp(g_row - m_last),
                            jnp.zeros((GATE_ROWS - 4, TM), F32)], axis=0)
    cols = jnp.transpose(rows)
    m_col, w_inter, den_floor, w_key = cols[:, 0:1], cols[:, 1:2], cols[:, 2:3], cols[:, 3:4]
    ii = lax.broadcasted_iota(jnp.int32, (TM, TM), 0)
    jj = lax.broadcasted_iota(jnp.int32, (TM, TM), 1)
    visible = ii >= jj if d == 0 else ii <= jj

    dmat = jnp.where(visible, jnp.exp(g_row - m_col), 0.0)
    s = lax.dot_general(q, k, (((1,), (1,)), ((), ())), preferred_element_type=F32) * dmat
    num = w_inter * _dot(q, ctb_ref[d]) + _dot(s.astype(BF16), v)
    qn = jnp.sum(q.astype(F32) * n_ref[d], axis=-1, keepdims=True)
    den = w_inter * qn + jnp.sum(s, axis=-1, keepdims=True)
    h_ref[0] = (num / jnp.maximum(jnp.abs(den), den_floor)).astype(BF16)

    decay = jnp.exp(m_prev - m_last)
    kw = k.astype(F32) * w_key
    ct = decay * ct_ref[d] + lax.dot_general(kw.astype(BF16), v, (((0,), (0,)), ((), ())),
                                             preferred_element_type=F32)
    ct_ref[d] = ct
    ctb_ref[d] = ct.astype(BF16)
    n_ref[d] = decay * n_ref[d] + jnp.sum(kw, axis=0, keepdims=True)
    m_ref[d] = jnp.zeros(m_ref.shape[1:], F32) + (b_row[:, last:last + 1] + m_last)


def _scan_kernel(qf_ref, kf_ref, vf_ref, gf_ref, qb_ref, kb_ref, vb_ref, gb_ref, hf_ref, hb_ref,
                 ct_ref, ctb_ref, n_ref, m_ref):
    @pl.when(pl.program_id(2) == 0)
    def _():
        ct_ref[...] = jnp.zeros_like(ct_ref)
        ctb_ref[...] = jnp.zeros_like(ctb_ref)
        n_ref[...] = jnp.zeros_like(n_ref)
        m_ref[...] = jnp.zeros_like(m_ref)

    _scan_chunk(0, qf_ref, kf_ref, vf_ref, gf_ref, hf_ref, ct_ref, ctb_ref, n_ref, m_ref)
    _scan_chunk(1, qb_ref, kb_ref, vb_ref, gb_ref, hb_ref, ct_ref, ctb_ref, n_ref, m_ref)


def _mlstm_scan(q, k, v, gates):
    n_batch, rows, _ = q.shape
    n_t = rows // TM

    def bwd_tile(c):
        return jnp.where(c == 0, 0, n_t - c)

    f_spec = pl.BlockSpec((1, TM, M_HEAD_DIM), lambda b, h, c: (b, c, h))
    b_spec = pl.BlockSpec((1, TM, M_HEAD_DIM), lambda b, h, c: (b, bwd_tile(c), h))
    gf_spec = pl.BlockSpec((1, 1, 1, GATE_ROWS, TM), lambda b, h, c: (b, 0, h, 0, c))
    gb_spec = pl.BlockSpec((1, 1, 1, GATE_ROWS, TM), lambda b, h, c: (b, 1, h, 0, bwd_tile(c)))
    out = jax.ShapeDtypeStruct((n_batch, rows, D_INNER), BF16)
    return pl.pallas_call(
        _scan_kernel,
        out_shape=(out, out),
        grid=(n_batch, M_HEADS, n_t),
        in_specs=[f_spec, f_spec, f_spec, gf_spec, b_spec, b_spec, b_spec, gb_spec],
        out_specs=(f_spec, b_spec),
        scratch_shapes=[pltpu.VMEM((2, M_HEAD_DIM, M_HEAD_DIM), F32),
                        pltpu.VMEM((2, M_HEAD_DIM, M_HEAD_DIM), BF16),
                        pltpu.VMEM((2, 1, M_HEAD_DIM), F32),
                        pltpu.VMEM((2, 8, 128), F32)],
        compiler_params=_params("parallel", "parallel", "arbitrary"),
        name="mlstm_scan",
    )(q, k, v, gates, q, k, v, gates)


def _post_norm(x, y, gate, pg, pb):
    return _ln(ALPHA * x + gate * y) * pg + pb


def _readout_kernel(hf_ref, hb_ref, xc_ref, z_ref, x_ref, mod_ref, gn_ref, sk_ref, w_ref, pg_ref, pb_ref, o_ref):
    parts = []
    for h in range(M_HEADS):
        sl = slice(h * M_HEAD_DIM, (h + 1) * M_HEAD_DIM)
        hn = _ln(hf_ref[0, :, sl].astype(F32) + hb_ref[0, :, sl].astype(F32))
        a = hn * gn_ref[:, sl] + sk_ref[:, sl] * xc_ref[0, :, sl].astype(F32)
        parts.append((a * _silu(z_ref[0, :, sl].astype(F32))).astype(BF16))
    y = _dot(jnp.concatenate(parts, axis=1), w_ref[...])
    o_ref[0] = _post_norm(x_ref[0], y, _mod_chunk(mod_ref[0], 2), pg_ref[...], pb_ref[...])


def _mlstm_readout(hf, hb, xc, z, xs, mod_l, gn_w, skip, w_out, pg, pb):
    n_batch, rows, _ = xs.shape
    return pl.pallas_call(
        _readout_kernel,
        out_shape=jax.ShapeDtypeStruct((n_batch, rows, D_MODEL), F32),
        grid=(n_batch, rows // TM),
        in_specs=[_tile_spec(D_INNER), _tile_spec(D_INNER),
                  _tile_spec(D_INNER), _tile_spec(D_INNER), _tile_spec(D_MODEL), _mod_spec(n_batch, True),
                  _const_spec((1, D_INNER)), _const_spec((1, D_INNER)), _const_spec((D_INNER, D_MODEL)),
                  _const_spec((1, D_MODEL)), _const_spec((1, D_MODEL))],
        out_specs=_tile_spec(D_MODEL),
        compiler_params=_params("parallel", "parallel"),
        name="mlstm_readout",
    )(hf, hb, xc, z, xs, mod_l, gn_w, skip, w_out, pg, pb)


def _fourier_out_kernel(ctx_tile, *refs):
    if ctx_tile:
        mx_ref, mc_ref, x_ref, mod_ref, w_ref, pg_ref, pb_ref, o_ref = refs
        a = jnp.where(pl.program_id(1) == 0, mc_ref[0], mx_ref[0])
    else:
        mx_ref, x_ref, mod_ref, w_ref, pg_ref, pb_ref, o_ref = refs
        a = mx_ref[0]
    y = _dot(a.astype(BF16), w_ref[...])
    o_ref[0] = _post_norm(x_ref[0], y, _mod_chunk(mod_ref[0], 2), pg_ref[...], pb_ref[...])


def _fourier_out(mixed_x, mixed_c, xs, mod_l, w_out, pg, pb):
    n_batch = xs.shape[0]
    ctx_tile = mixed_c is not None
    n_t = mixed_x.shape[1] // TM + (1 if ctx_tile else 0)
    in_specs = [pl.BlockSpec((1, TM, D_MODEL), lambda b, t: (b, jnp.maximum(t - 1, 0) if ctx_tile else t, 0))]
    args = [mixed_x]
    if ctx_tile:
        in_specs.append(pl.BlockSpec((1, CTX_LEN, D_MODEL), lambda b, t: (b, 0, 0)))
        args.append(mixed_c)
    in_specs += [_tile_spec(D_MODEL, 0 if ctx_tile else CTX_LEN // TM), _mod_spec(n_batch, ctx_tile),
                 _const_spec((D_MODEL, D_MODEL)), _const_spec((1, D_MODEL)), _const_spec((1, D_MODEL))]
    return pl.pallas_call(
        functools.partial(_fourier_out_kernel, ctx_tile),
        out_shape=jax.ShapeDtypeStruct((n_batch, n_t * TM, D_MODEL), F32),
        grid=(n_batch, n_t),
        in_specs=in_specs,
        out_specs=_tile_spec(D_MODEL),
        compiler_params=_params("parallel", "parallel"),
        name="fourier_out",
    )(*args, xs, mod_l, w_out, pg, pb)


def _mlp_kernel(x_ref, mod_ref, w1_ref, w2_ref, pg_ref, pb_ref, o_ref):
    x = x_ref[0]
    m = mod_ref[0]
    h = (_ln(x) * (1.0 + _mod_chunk(m, 4)) + _mod_chunk(m, 3)).astype(BF16)
    acc = jnp.zeros((TM, D_MODEL), F32)
    for j in range(D_FF // D_MODEL):
        sl = slice(j * D_MODEL, (j + 1) * D_MODEL)
        u = jnp.square(jnp.maximum(_dot(h, w1_ref[:, sl]), 0.0)).astype(BF16)
        acc = acc + _dot(u, w2_ref[sl, :])
    o_ref[0] = _post_norm(x, acc, _mod_chunk(m, 5), pg_ref[...], pb_ref[...])


def _mlp(xs, mod_l, w1, w2, pg, pb, ctx_tile):
    n_batch, rows, _ = xs.shape
    return pl.pallas_call(
        _mlp_kernel,
        out_shape=jax.ShapeDtypeStruct((n_batch, rows, D_MODEL), F32),
        grid=(n_batch, rows // TM),
        in_specs=[_tile_spec(D_MODEL), _mod_spec(n_batch, ctx_tile),
                  _const_spec((D_MODEL, D_FF)), _const_spec((D_FF, D_MODEL)),
                  _const_spec((1, D_MODEL)), _const_spec((1, D_MODEL))],
        out_specs=_tile_spec(D_MODEL),
        compiler_params=_params("parallel", "parallel"),
        name="mlp",
    )(xs, mod_l, w1, w2, pg, pb)


def _dft_tables():
    two_pi = 2.0 * np.pi
    c = np.arange(F_GROUP_DIM)
    ang = two_pi * ((c[:, None] * c[None, :]) % F_GROUP_DIM) / F_GROUP_DIM
    s_ch = 1.0 / math.sqrt(F_GROUP_DIM)
    f_ch = np.concatenate([np.cos(ang) * s_ch, -np.sin(ang) * s_ch], axis=1)

    seq = FFT_T1 * FFT_T2
    t1 = np.arange(FFT_T1)
    t2 = np.arange(FFT_T2)
    num = (t1[None, :, None] * t1[None, None, :] * FFT_T2 + t2[:, None, None] * t1[None, :, None]) % seq
    ang = two_pi * num / seq
    s1 = 1.0 / math.sqrt(FFT_T1)
    gr, gi = np.cos(ang) * s1, -np.sin(ang) * s1
    g1 = np.concatenate([np.concatenate([gr, -gi], axis=2), np.concatenate([gi, gr], axis=2)], axis=1)

    ang = two_pi * ((t2[:, None] * t2[None, :]) % FFT_T2) / FFT_T2
    s2 = 1.0 / math.sqrt(FFT_T2)
    f2 = np.concatenate([np.cos(ang) * s2, np.sin(ang) * s2], axis=1)

    p = np.arange(CTX_LEN)
    ang = two_pi * ((p[:, None] * p[None, :]) % CTX_LEN) / CTX_LEN
    sc = 1.0 / math.sqrt(CTX_LEN)
    f_ctx = np.concatenate([np.cos(ang) * sc, np.sin(ang) * sc], axis=1)
    return tuple(jnp.asarray(a, dtype=F32).astype(BF16) for a in (f_ch, g1, f2, f_ctx))


def _cdft_kernel(ctx_tile, *refs):
    if ctx_tile:
        x_ref, mod_ref, f_ref, zrx_ref, zix_ref, zrc_ref, zic_ref = refs
    else:
        x_ref, mod_ref, f_ref, zrx_ref, zix_ref = refs
    m = mod_ref[0]
    h = (_ln(x_ref[0]) * (1.0 + _mod_chunk(m, 1)) + _mod_chunk(m, 0)).astype(BF16)
    f = f_ref[...]
    zr, zi = [], []
    for g in range(F_GROUPS):
        z = _dot(h[:, g * F_GROUP_DIM:(g + 1) * F_GROUP_DIM], f).astype(BF16)
        zr.append(z[:, :F_GROUP_DIM])
        zi.append(z[:, F_GROUP_DIM:])
    zr = jnp.concatenate(zr, axis=1)
    zi = jnp.concatenate(zi, axis=1)
    if ctx_tile:
        t = pl.program_id(1)

        @pl.when(t == 0)
        def _():
            zrc_ref[0] = zr
            zic_ref[0] = zi

        @pl.when(t > 0)
        def _():
            zrx_ref[0] = zr
            zix_ref[0] = zi
    else:
        zrx_ref[0] = zr
        zix_ref[0] = zi


def _channel_dft(xs, mod_l, f_ch, ctx_tile):
    n_batch, rows, _ = xs.shape
    seq = rows - CTX_LEN
    off = CTX_LEN // TM
    zx = jax.ShapeDtypeStruct((n_batch, seq, D_MODEL), BF16)
    zc = jax.ShapeDtypeStruct((n_batch, CTX_LEN, D_MODEL), BF16)
    if ctx_tile:
        n_t = rows // TM
        x_out = pl.BlockSpec((1, TM, D_MODEL), lambda b, t: (b, jnp.maximum(t - off, 0), 0))
        c_out = pl.BlockSpec((1, CTX_LEN, D_MODEL), lambda b, t: (b, 0, 0))
        out_shape, out_specs, x_in = (zx, zx, zc, zc), (x_out, x_out, c_out, c_out), _tile_spec(D_MODEL)
    else:
        n_t = seq // TM
        out_shape, out_specs, x_in = (zx, zx), (_tile_spec(D_MODEL), _tile_spec(D_MODEL)), _tile_spec(D_MODEL, off)
    return pl.pallas_call(
        functools.partial(_cdft_kernel, ctx_tile),
        out_shape=out_shape,
        grid=(n_batch, n_t),
        in_specs=[x_in, _mod_spec(n_batch, ctx_tile), _const_spec((F_GROUP_DIM, 2 * F_GROUP_DIM))],
        out_specs=out_specs,
        compiler_params=_params("parallel", "arbitrary"),
        name="fourier_channel_dft",
    )(xs, mod_l, f_ch)


def _seq_dft_kernel(zr_ref, zi_ref, g1_ref, f2_ref, o_ref, sr_ref, si_ref):
    sr_ref[...] = zr_ref[0].astype(F32)
    si_ref[...] = zi_ref[0].astype(F32)

    def stage1(t2, carry):
        rows = pl.ds(t2, FFT_T1, stride=FFT_T2)
        z = jnp.concatenate([sr_ref[rows, :], si_ref[rows, :]], axis=0).astype(BF16)
        v = _dot(g1_ref[t2], z)
        sr_ref[rows, :] = v[:FFT_T1]
        si_ref[rows, :] = v[FFT_T1:]
        return carry

    lax.fori_loop(0, FFT_T2, stage1, 0, unroll=DFT_UNROLL)

    def stage2(t1, carry):
        rows = pl.ds(pl.multiple_of(t1 * FFT_T2, FFT_T2), FFT_T2)
        v = jnp.concatenate([sr_ref[rows, :], si_ref[rows, :]], axis=0).astype(BF16)
        o_ref[0, pl.ds(t1, FFT_T2, stride=FFT_T1), :] = _dot(f2_ref[...], v)
        return carry

    lax.fori_loop(0, FFT_T1, stage2, 0, unroll=DFT_UNROLL)


def _seq_dft(zr, zi, g1, f2):
    n_batch, seq, _ = zr.shape
    blk = pl.BlockSpec((1, seq, FFT_CB), lambda b, j: (b, 0, j))
    return pl.pallas_call(
        _seq_dft_kernel,
        out_shape=jax.ShapeDtypeStruct((n_batch, seq, D_MODEL), F32),
        grid=(n_batch, D_MODEL // FFT_CB),
        in_specs=[blk, blk, _const_spec((FFT_T2, 2 * FFT_T1, 2 * FFT_T1)), _const_spec((FFT_T2, 2 * FFT_T2))],
        out_specs=blk,
        scratch_shapes=[pltpu.VMEM((seq, FFT_CB), F32), pltpu.VMEM((seq, FFT_CB), F32)],
        compiler_params=_params("parallel", "parallel"),
        name="fourier_seq_dft",
    )(zr, zi, g1, f2)


def _ctx_dft_kernel(zr_ref, zi_ref, f_ref, o_ref):
    o_ref[0] = _dot(f_ref[...], jnp.concatenate([zr_ref[0], zi_ref[0]], axis=0))


def _ctx_dft(zr, zi, f_ctx):
    n_batch = zr.shape[0]
    blk = pl.BlockSpec((1, CTX_LEN, D_MODEL), lambda b: (b, 0, 0))
    return pl.pallas_call(
        _ctx_dft_kernel,
        out_shape=jax.ShapeDtypeStruct((n_batch, CTX_LEN, D_MODEL), F32),
        grid=(n_batch,),
        in_specs=[blk, blk, _const_spec((CTX_LEN, 2 * CTX_LEN))],
        out_specs=blk,
        compiler_params=_params("parallel"),
        name="fourier_ctx_dft",
    )(zr, zi, f_ctx)


def _mlstm_layer(xs, mod_l, w_in, w_conv, b_conv, w_q, w_k, w_v, w_gate, b_gate, gn_w, skip, w_out, pg, pb):
    xm, z = _in_proj(xs, mod_l, w_in.astype(BF16))
    wg = w_gate.reshape(3, M_HEADS, M_HEAD_DIM, N_GATES).transpose(1, 3, 0, 2).reshape(
        M_HEADS, N_GATES, 3 * M_HEAD_DIM)
    xc, q, k, v, pre_t = _conv_qkv(
        xm, w_conv.reshape(CONV_K * CONV_K, D_INNER), b_conv.reshape(1, D_INNER),
        w_q.astype(BF16), w_k.astype(BF16), w_v.astype(BF16), wg.astype(BF16), b_gate.reshape(N_GATES, 1))
    hf, hb = _mlstm_scan(q, k, v, _gate_scans(pre_t))
    return _mlstm_readout(hf, hb, xc, z, xs, mod_l, gn_w.reshape(1, D_INNER), skip.reshape(1, D_INNER),
                          w_out.astype(BF16), pg, pb)


def _fourier_layer(xs, mod_l, w_out, pg, pb, need_ctx, tables):
    f_ch, g1, f2, f_ctx = tables
    z = _channel_dft(xs, mod_l, f_ch, need_ctx)
    mixed_x = _seq_dft(z[0], z[1], g1, f2)
    mixed_c = _ctx_dft(z[2], z[3], f_ctx) if need_ctx else None
    return _fourier_out(mixed_x, mixed_c, xs, mod_l, w_out.astype(BF16), pg, pb)


def kernel(x, c, ctx, c_ctx, ada_w, ada_b, post_g, post_b, m_w_in, m_w_conv, m_b_conv, m_w_q, m_w_k, m_w_v, m_w_gate, m_b_gate, m_gn_w, m_skip, m_w_out, f_w_out, mlp_w1, mlp_w2):
    n_batch = x.shape[0]
    assert n_batch < MOD_ROWS and x.shape[1] == FFT_T1 * FFT_T2 and ctx.shape[1] == CTX_LEN == TM
    xs = jnp.concatenate([ctx, x], axis=1)
    s_in = jnp.concatenate([c, c_ctx[None, :], jnp.zeros((MOD_ROWS - n_batch - 1, D_MODEL), F32)], axis=0)
    mod = _ada_table(s_in, ada_w, ada_b)
    tables = _dft_tables()
    for i in range(DEPTH):
        is_mlstm = i % N_MIXERS == 0
        j = i // N_MIXERS
        need_ctx = i < DEPTH - 1
        mod_l = mod[i].reshape(MOD_ROWS, 1, 6 * D_MODEL)
        pg = post_g[i].reshape(2, 1, D_MODEL)
        pb = post_b[i].reshape(2, 1, D_MODEL)
        if is_mlstm:
            xs = _mlstm_layer(xs, mod_l, m_w_in[j], m_w_conv[j], m_b_conv[j], m_w_q[j], m_w_k[j], m_w_v[j],
                              m_w_gate[j], m_b_gate[j], m_gn_w[j], m_skip[j], m_w_out[j], pg[0], pb[0])
            if not need_ctx:
                xs = xs[:, CTX_LEN:]
        else:
            xs = _fourier_layer(xs, mod_l, f_w_out[j], pg[0], pb[0], need_ctx, tables)
        xs = _mlp(xs, mod_l, mlp_w1[i].astype(BF16), mlp_w2[i].astype(BF16), pg[1], pb[1], need_ctx)
    return xs
```

```python
import functools
import math

import numpy as np
import jax
import jax.numpy as jnp
from jax import lax
from jax.experimental import pallas as pl
from jax.experimental.pallas import tpu as pltpu

D_MODEL = 1024
DEPTH = 4
GRID_W = 64
CTX_LEN = 256
N_MIXERS = 2
D_INNER = 2 * D_MODEL
M_HEADS = 4
M_HEAD_DIM = D_INNER // M_HEADS
CONV_K = 3
F_GROUPS = 4
F_GROUP_DIM = D_MODEL // F_GROUPS
D_FF = 4 * D_MODEL
ALPHA = float((2 * DEPTH) ** 0.25)
LN_EPS = 1e-5

TM = 256
N_GATES = 4 * M_HEADS
GATE_ROWS = 8
MOD_ROWS = 8
FFT_T1 = 64
FFT_T2 = 128
FFT_CB = 128
GATE_CHUNKS = 3
SCAN_HEADS = 2
DFT_UNROLL = 8
VMEM_LIMIT = 52 * 1024 * 1024

F32 = jnp.float32
BF16 = jnp.bfloat16


def _params(*sem):
    return pltpu.CompilerParams(dimension_semantics=sem, vmem_limit_bytes=VMEM_LIMIT)


def _ln(x):
    mu = jnp.mean(x, axis=-1, keepdims=True)
    xc = x - mu
    var = jnp.mean(xc * xc, axis=-1, keepdims=True)
    return xc * lax.rsqrt(var + LN_EPS)


def _silu(x):
    return x / (1.0 + jnp.exp(-x))


def _dot(a, b):
    return jnp.dot(a, b, preferred_element_type=F32)


def _mod_chunk(m, j):
    return m[:, j * D_MODEL:(j + 1) * D_MODEL]


def _const_spec(shape):
    nd = len(shape)
    return pl.BlockSpec(shape, lambda *_: (0,) * nd, pipeline_mode=pl.Buffered(1))


def _mod_spec(n_batch, ctx_tile):
    if ctx_tile:
        return pl.BlockSpec((1, 1, 6 * D_MODEL), lambda b, t: (jnp.where(t == 0, n_batch, b), 0, 0))
    return pl.BlockSpec((1, 1, 6 * D_MODEL), lambda b, t: (b, 0, 0))


def _tile_spec(width, off=0):
    return pl.BlockSpec((1, TM, width), lambda b, t: (b, t + off, 0))


def _stream_specs(stream):
    if isinstance(stream, tuple):
        return [pl.BlockSpec((1, CTX_LEN, D_MODEL), lambda b, t: (b, 0, 0)),
                pl.BlockSpec((1, TM, D_MODEL), lambda b, t: (b, jnp.maximum(t - CTX_LEN // TM, 0), 0))], list(stream)
    return [_tile_spec(D_MODEL)], [stream]


def _stream_shape(stream):
    if isinstance(stream, tuple):
        return stream[1].shape[0], stream[0].shape[1] + stream[1].shape[1]
    return stream.shape[0], stream.shape[1]


def _stream_tile(refs):
    if len(refs) == 2:
        return jnp.where(pl.program_id(1) == 0, refs[0][0], refs[1][0])
    return refs[0][0]


def _ada_kernel(s_ref, w_ref, b_ref, o_ref):
    s = _silu(s_ref[...])
    o_ref[0] = jnp.dot(s, w_ref[0], preferred_element_type=F32, precision=lax.Precision.HIGHEST) + b_ref[0]


def _ada_table(s_in, ada_w, ada_b):
    n_chunks = 6
    return pl.pallas_call(
        _ada_kernel,
        out_shape=jax.ShapeDtypeStruct((DEPTH, MOD_ROWS, 6 * D_MODEL), F32),
        grid=(DEPTH, n_chunks),
        in_specs=[pl.BlockSpec((MOD_ROWS, D_MODEL), lambda l, j: (0, 0)),
                  pl.BlockSpec((1, D_MODEL, D_MODEL), lambda l, j: (l, 0, j)),
                  pl.BlockSpec((1, 1, D_MODEL), lambda l, j: (l, 0, j))],
        out_specs=pl.BlockSpec((1, MOD_ROWS, D_MODEL), lambda l, j: (l, 0, j)),
        compiler_params=_params("parallel", "parallel"),
        name="ada_table",
    )(s_in, ada_w, ada_b.reshape(DEPTH, 1, 6 * D_MODEL))


def _in_kernel(n_stream, *refs):
    mod_ref, w_ref, xm_ref, z_ref = refs[n_stream:]
    m = mod_ref[0]
    h = (_ln(_stream_tile(refs[:n_stream])) * (1.0 + _mod_chunk(m, 1)) + _mod_chunk(m, 0)).astype(BF16)
    for j in range(2):
        sl = slice(j * D_MODEL, (j + 1) * D_MODEL)
        xm_ref[0, :, sl] = _dot(h, w_ref[:, sl]).astype(BF16)
        z_ref[0, :, sl] = _dot(h, w_ref[:, D_INNER + j * D_MODEL:D_INNER + (j + 1) * D_MODEL]).astype(BF16)


def _in_proj(stream, mod_l, w_in):
    n_batch, rows = _stream_shape(stream)
    specs, args = _stream_specs(stream)
    out = jax.ShapeDtypeStruct((n_batch, rows, D_INNER), BF16)
    return pl.pallas_call(
        functools.partial(_in_kernel, len(args)),
        out_shape=(out, out),
        grid=(n_batch, rows // TM),
        in_specs=specs + [_mod_spec(n_batch, True), _const_spec((D_MODEL, 2 * D_INNER))],
        out_specs=(_tile_spec(D_INNER), _tile_spec(D_INNER)),
        compiler_params=_params("parallel", "parallel"),
        name="mlstm_in_proj",
    )(*args, mod_l, w_in)


def _conv_qkv_kernel(xm_ref, prev_ref, next_ref, wc_ref, bc_ref, wq_ref, wk_ref, wv_ref, wg_ref, bg_ref,
                     xc_ref, q_ref, k_ref, v_ref, pre_ref):
    t = pl.program_id(1)
    n_t = pl.num_programs(1)
    is_ctx = t == 0
    row = lax.broadcasted_iota(jnp.int32, (TM, 1), 0)
    period = jnp.where(is_ctx, CTX_LEN - 1, GRID_W - 1)
    pos = row & period
    has_left = pos != 0
    has_right = pos != period
    prev_ok = t >= 2
    next_ok = jnp.logical_and(t >= 1, t <= n_t - 2)
    row_w = jnp.where(is_ctx, 0.0, 1.0)
    pre_t = jnp.zeros((N_GATES, TM), F32) + bg_ref[...]
    for h in range(M_HEADS):
        sl = slice(h * M_HEAD_DIM, (h + 1) * M_HEAD_DIM)
        xm = xm_ref[0, :, sl]
        e = jnp.concatenate([jnp.where(prev_ok, prev_ref[0, :, sl], 0).astype(F32),
                             xm.astype(F32),
                             jnp.where(next_ok, next_ref[0, :, sl], 0).astype(F32)], axis=0)
        p = []
        for dc in range(CONV_K):
            acc = None
            for dr in range(CONV_K):
                w = wc_ref[CONV_K * dr + dc:CONV_K * dr + dc + 1, sl]
                if dr != CONV_K // 2:
                    w = w * row_w
                term = w * e[dr * GRID_W:dr * GRID_W + TM]
                acc = term if acc is None else acc + term
            p.append(acc)
        y = (p[1] + bc_ref[:, sl]
             + jnp.where(has_left, pltpu.roll(p[0], 1, axis=0), 0.0)
             + jnp.where(has_right, pltpu.roll(p[2], TM - 1, axis=0), 0.0))
        xc = _silu(y).astype(BF16)
        xc_ref[0, :, sl] = xc
        q = _dot(xc, wq_ref[h]).astype(BF16)
        k = (_dot(xc, wk_ref[h]) * (M_HEAD_DIM ** -0.5)).astype(BF16)
        v = _dot(xm, wv_ref[h]).astype(BF16)
        q_ref[0, :, sl] = q
        k_ref[0, :, sl] = k
        v_ref[0, :, sl] = v
        pre_t = pre_t + lax.dot_general(wg_ref[h], jnp.concatenate([q, k, v], axis=1),
                                        (((1,), (1,)), ((), ())), preferred_element_type=F32)
    pre_ref[0] = pre_t


def _conv_qkv(xm, w_conv, b_conv, w_q, w_k, w_v, w_g, b_g):
    n_batch, rows, _ = xm.shape
    n_halo = rows // GRID_W
    per = TM // GRID_W
    wide = jax.ShapeDtypeStruct((n_batch, rows, D_INNER), BF16)
    return pl.pallas_call(
        _conv_qkv_kernel,
        out_shape=(wide, wide, wide, wide, jax.ShapeDtypeStruct((n_batch, N_GATES, rows), F32)),
        grid=(n_batch, rows // TM),
        in_specs=[_tile_spec(D_INNER),
                  pl.BlockSpec((1, GRID_W, D_INNER), lambda b, t: (b, jnp.maximum(per * t - 1, 0), 0)),
                  pl.BlockSpec((1, GRID_W, D_INNER), lambda b, t: (b, jnp.minimum(per * t + per, n_halo - 1), 0)),
                  _const_spec((CONV_K * CONV_K, D_INNER)),
                  _const_spec((1, D_INNER)),
                  _const_spec((M_HEADS, M_HEAD_DIM, M_HEAD_DIM)),
                  _const_spec((M_HEADS, M_HEAD_DIM, M_HEAD_DIM)),
                  _const_spec((M_HEADS, M_HEAD_DIM, M_HEAD_DIM)),
                  _const_spec((M_HEADS, N_GATES, 3 * M_HEAD_DIM)),
                  _const_spec((N_GATES, 1))],
        out_specs=(_tile_spec(D_INNER), _tile_spec(D_INNER), _tile_spec(D_INNER), _tile_spec(D_INNER),
                   pl.BlockSpec((1, N_GATES, TM), lambda b, t: (b, 0, t))),
        compiler_params=_params("parallel", "parallel"),
        name="mlstm_conv_qkv",
    )(xm, xm, xm, w_conv, b_conv, w_q, w_k, w_v, w_g, b_g)


def _scan_lanes(x, combine, fill, reverse):
    n = x.shape[-1]
    lane = lax.broadcasted_iota(jnp.int32, x.shape, 1)
    s = 1
    while s < n:
        if reverse:
            sh = jnp.where(lane < n - s, pltpu.roll(x, n - s, axis=1), fill)
        else:
            sh = jnp.where(lane >= s, pltpu.roll(x, s, axis=1), fill)
        x = combine(x, sh)
        s *= 2
    return x


def _gate_kernel(pre_ref, o_ref):
    zeros = jnp.zeros((GATE_ROWS - 3, TM), F32)
    for c in range(GATE_CHUNKS):
        lanes = slice(c * TM, (c + 1) * TM)
        for d in range(2):
            blk = pre_ref[0, 2 * M_HEADS * d:2 * M_HEADS * (d + 1), lanes]
            log_f = jnp.minimum(blk, 0.0) - jnp.log1p(jnp.exp(-jnp.abs(blk)))
            b = pltpu.roll(_scan_lanes(log_f, jnp.add, 0.0, reverse=d == 1), M_HEADS, axis=0)
            g = blk - b
            big = _scan_lanes(g, jnp.maximum, -jnp.inf, reverse=d == 1)
            for h in range(M_HEADS):
                o_ref[0, d, h, :, lanes] = jnp.concatenate([b[h:h + 1], g[h:h + 1], big[h:h + 1], zeros], axis=0)


def _gate_scans(pre_t):
    n_batch, _, rows = pre_t.shape
    width = GATE_CHUNKS * TM
    return pl.pallas_call(
        _gate_kernel,
        out_shape=jax.ShapeDtypeStruct((n_batch, 2, M_HEADS, GATE_ROWS, rows), F32),
        grid=(n_batch, rows // width),
        in_specs=[pl.BlockSpec((1, N_GATES, width), lambda b, t: (b, 0, t))],
        out_specs=pl.BlockSpec((1, 2, M_HEADS, GATE_ROWS, width), lambda b, t: (b, 0, 0, 0, t)),
        compiler_params=_params("parallel", "parallel"),
        name="mlstm_gate_scans",
    )(pre_t)


def _scan_chunk(d, hh, q_ref, k_ref, v_ref, gate_ref, h_ref, ct_ref, ctb_ref, n_ref, m_ref):
    sl = slice(hh * M_HEAD_DIM, (hh + 1) * M_HEAD_DIM)
    q = q_ref[0, :, sl]
    k = k_ref[0, :, sl]
    v = v_ref[0, :, sl]
    gates = gate_ref[0, 0, hh]
    b_row, g_row, big_row = gates[0:1], gates[1:2], gates[2:3]
    m_prev = m_ref[d, hh, 0:1, 0:1]
    last = TM - 1 if d == 0 else 0
    m_row = jnp.maximum(m_prev, big_row)
    m_last = m_row[:, last:last + 1]
    rows = jnp.concatenate([m_row,
                            jnp.exp(m_prev - m_row),
                            jnp.exp(-(b_row + m_row)),
                            jnp.exp(g_row - m_last),
                            jnp.zeros((GATE_ROWS - 4, TM), F32)], axis=0)
    cols = jnp.transpose(rows)
    m_col, w_inter, den_floor, w_key = cols[:, 0:1], cols[:, 1:2], cols[:, 2:3], cols[:, 3:4]
    ii = lax.broadcasted_iota(jnp.int32, (TM, TM), 0)
    jj = lax.broadcasted_iota(jnp.int32, (TM, TM), 1)
    visible = ii >= jj if d == 0 else ii <= jj

    dmat = jnp.where(visible, jnp.exp(g_row - m_col), 0.0)
    s = lax.dot_general(q, k, (((1,), (1,)), ((), ())), preferred_element_type=F32) * dmat
    num = w_inter * _dot(q, ctb_ref[d, hh]) + _dot(s.astype(BF16), v)
    qn = jnp.sum(q.astype(F32) * n_ref[d, hh], axis=-1, keepdims=True)
    den = w_inter * qn + jnp.sum(s, axis=-1, keepdims=True)
    h_ref[0, :, sl] = (num / jnp.maximum(jnp.abs(den), den_floor)).astype(BF16)

    decay = jnp.exp(m_prev - m_last)
    kw = k.astype(F32) * w_key
    ct = decay * ct_ref[d, hh] + lax.dot_general(kw.astype(BF16), v, (((0,), (0,)), ((), ())),
                                                 preferred_element_type=F32)
    ct_ref[d, hh] = ct
    ctb_ref[d, hh] = ct.astype(BF16)
    n_ref[d, hh] = decay * n_ref[d, hh] + jnp.sum(kw, axis=0, keepdims=True)
    m_ref[d, hh] = jnp.zeros(m_ref.shape[2:], F32) + (b_row[:, last:last + 1] + m_last)


def _scan_kernel(qf_ref, kf_ref, vf_ref, gf_ref, qb_ref, kb_ref, vb_ref, gb_ref, hf_ref, hb_ref,
                 ct_ref, ctb_ref, n_ref, m_ref):
    @pl.when(pl.program_id(2) == 0)
    def _():
        ct_ref[...] = jnp.zeros_like(ct_ref)
        ctb_ref[...] = jnp.zeros_like(ctb_ref)
        n_ref[...] = jnp.zeros_like(n_ref)
        m_ref[...] = jnp.zeros_like(m_ref)

    for hh in range(SCAN_HEADS):
        _scan_chunk(0, hh, qf_ref, kf_ref, vf_ref, gf_ref, hf_ref, ct_ref, ctb_ref, n_ref, m_ref)
        _scan_chunk(1, hh, qb_ref, kb_ref, vb_ref, gb_ref, hb_ref, ct_ref, ctb_ref, n_ref, m_ref)


def _mlstm_scan(q, k, v, gates):
    n_batch, rows, _ = q.shape
    n_t = rows // TM
    width = SCAN_HEADS * M_HEAD_DIM

    def bwd_tile(c):
        return jnp.where(c == 0, 0, n_t - c)

    f_spec = pl.BlockSpec((1, TM, width), lambda b, h, c: (b, c, h))
    b_spec = pl.BlockSpec((1, TM, width), lambda b, h, c: (b, bwd_tile(c), h))
    gf_spec = pl.BlockSpec((1, 1, SCAN_HEADS, GATE_ROWS, TM), lambda b, h, c: (b, 0, h, 0, c))
    gb_spec = pl.BlockSpec((1, 1, SCAN_HEADS, GATE_ROWS, TM), lambda b, h, c: (b, 1, h, 0, bwd_tile(c)))
    out = jax.ShapeDtypeStruct((n_batch, rows, D_INNER), BF16)
    return pl.pallas_call(
        _scan_kernel,
        out_shape=(out, out),
        grid=(n_batch, M_HEADS // SCAN_HEADS, n_t),
        in_specs=[f_spec, f_spec, f_spec, gf_spec, b_spec, b_spec, b_spec, gb_spec],
        out_specs=(f_spec, b_spec),
        scratch_shapes=[pltpu.VMEM((2, SCAN_HEADS, M_HEAD_DIM, M_HEAD_DIM), F32),
                        pltpu.VMEM((2, SCAN_HEADS, M_HEAD_DIM, M_HEAD_DIM), BF16),
                        pltpu.VMEM((2, SCAN_HEADS, 1, M_HEAD_DIM), F32),
                        pltpu.VMEM((2, SCAN_HEADS, 8, 128), F32)],
        compiler_params=_params("parallel", "parallel", "arbitrary"),
        name="mlstm_scan",
    )(q, k, v, gates, q, k, v, gates)


def _post_norm(x, y, gate, pg, pb):
    return _ln(ALPHA * x + gate * y) * pg + pb


def _readout_kernel(n_stream, hf_ref, hb_ref, xc_ref, z_ref, *refs):
    mod_ref, gn_ref, sk_ref, w_ref, pg_ref, pb_ref, o_ref = refs[n_stream:]
    parts = []
    for h in range(M_HEADS):
        sl = slice(h * M_HEAD_DIM, (h + 1) * M_HEAD_DIM)
        hn = _ln(hf_ref[0, :, sl].astype(F32) + hb_ref[0, :, sl].astype(F32))
        a = hn * gn_ref[:, sl] + sk_ref[:, sl] * xc_ref[0, :, sl].astype(F32)
        parts.append((a * _silu(z_ref[0, :, sl].astype(F32))).astype(BF16))
    y = _dot(jnp.concatenate(parts, axis=1), w_ref[...])
    o_ref[0] = _post_norm(_stream_tile(refs[:n_stream]), y, _mod_chunk(mod_ref[0], 2), pg_ref[...], pb_ref[...])


def _mlstm_readout(hf, hb, xc, z, stream, mod_l, gn_w, skip, w_out, pg, pb):
    n_batch, rows = _stream_shape(stream)
    specs, args = _stream_specs(stream)
    return pl.pallas_call(
        functools.partial(_readout_kernel, len(args)),
        out_shape=jax.ShapeDtypeStruct((n_batch, rows, D_MODEL), F32),
        grid=(n_batch, rows // TM),
        in_specs=[_tile_spec(D_INNER), _tile_spec(D_INNER), _tile_spec(D_INNER), _tile_spec(D_INNER)] + specs
        + [_mod_spec(n_batch, True),
           _const_spec((1, D_INNER)), _const_spec((1, D_INNER)), _const_spec((D_INNER, D_MODEL)),
           _const_spec((1, D_MODEL)), _const_spec((1, D_MODEL))],
        out_specs=_tile_spec(D_MODEL),
        compiler_params=_params("parallel", "parallel"),
        name="mlstm_readout",
    )(hf, hb, xc, z, *args, mod_l, gn_w, skip, w_out, pg, pb)


def _fourier_out_kernel(ctx_tile, *refs):
    if ctx_tile:
        mx_ref, mc_ref, x_ref, mod_ref, w_ref, pg_ref, pb_ref, o_ref = refs
        a = jnp.where(pl.program_id(1) == 0, mc_ref[0], mx_ref[0])
    else:
        mx_ref, x_ref, mod_ref, w_ref, pg_ref, pb_ref, o_ref = refs
        a = mx_ref[0]
    y = _dot(a.astype(BF16), w_ref[...])
    o_ref[0] = _post_norm(x_ref[0], y, _mod_chunk(mod_ref[0], 2), pg_ref[...], pb_ref[...])


def _fourier_out(mixed_x, mixed_c, xs, mod_l, w_out, pg, pb):
    n_batch = xs.shape[0]
    ctx_tile = mixed_c is not None
    n_t = mixed_x.shape[1] // TM + (1 if ctx_tile else 0)
    in_specs = [pl.BlockSpec((1, TM, D_MODEL), lambda b, t: (b, jnp.maximum(t - 1, 0) if ctx_tile else t, 0))]
    args = [mixed_x]
    if ctx_tile:
        in_specs.append(pl.BlockSpec((1, CTX_LEN, D_MODEL), lambda b, t: (b, 0, 0)))
        args.append(mixed_c)
    in_specs += [_tile_spec(D_MODEL, 0 if ctx_tile else CTX_LEN // TM), _mod_spec(n_batch, ctx_tile),
                 _const_spec((D_MODEL, D_MODEL)), _const_spec((1, D_MODEL)), _const_spec((1, D_MODEL))]
    return pl.pallas_call(
        functools.partial(_fourier_out_kernel, ctx_tile),
        out_shape=jax.ShapeDtypeStruct((n_batch, n_t * TM, D_MODEL), F32),
        grid=(n_batch, n_t),
        in_specs=in_specs,
        out_specs=_tile_spec(D_MODEL),
        compiler_params=_params("parallel", "parallel"),
        name="fourier_out",
    )(*args, xs, mod_l, w_out, pg, pb)


def _mlp_kernel(x_ref, mod_ref, w1_ref, w2_ref, pg_ref, pb_ref, o_ref):
    x = x_ref[0]
    m = mod_ref[0]
    h = (_ln(x) * (1.0 + _mod_chunk(m, 4)) + _mod_chunk(m, 3)).astype(BF16)
    acc = jnp.zeros((TM, D_MODEL), F32)
    for j in range(D_FF // D_MODEL):
        sl = slice(j * D_MODEL, (j + 1) * D_MODEL)
        u = jnp.square(jnp.maximum(_dot(h, w1_ref[:, sl]), 0.0)).astype(BF16)
        acc = acc + _dot(u, w2_ref[sl, :])
    o_ref[0] = _post_norm(x, acc, _mod_chunk(m, 5), pg_ref[...], pb_ref[...])


def _mlp(xs, mod_l, w1, w2, pg, pb, ctx_tile):
    n_batch, rows, _ = xs.shape
    return pl.pallas_call(
        _mlp_kernel,
        out_shape=jax.ShapeDtypeStruct((n_batch, rows, D_MODEL), F32),
        grid=(n_batch, rows // TM),
        in_specs=[_tile_spec(D_MODEL), _mod_spec(n_batch, ctx_tile),
                  _const_spec((D_MODEL, D_FF)), _const_spec((D_FF, D_MODEL)),
                  _const_spec((1, D_MODEL)), _const_spec((1, D_MODEL))],
        out_specs=_tile_spec(D_MODEL),
        compiler_params=_params("parallel", "parallel"),
        name="mlp",
    )(xs, mod_l, w1, w2, pg, pb)


def _dft_tables():
    two_pi = 2.0 * np.pi
    c = np.arange(F_GROUP_DIM)
    ang = two_pi * ((c[:, None] * c[None, :]) % F_GROUP_DIM) / F_GROUP_DIM
    s_ch = 1.0 / math.sqrt(F_GROUP_DIM)
    f_ch = np.concatenate([np.cos(ang) * s_ch, -np.sin(ang) * s_ch], axis=1)

    seq = FFT_T1 * FFT_T2
    t1 = np.arange(FFT_T1)
    t2 = np.arange(FFT_T2)
    num = (t1[None, :, None] * t1[None, None, :] * FFT_T2 + t2[:, None, None] * t1[None, :, None]) % seq
    ang = two_pi * num / seq
    s1 = 1.0 / math.sqrt(FFT_T1)
    gr, gi = np.cos(ang) * s1, -np.sin(ang) * s1
    g1 = np.concatenate([np.concatenate([gr, -gi], axis=2), np.concatenate([gi, gr], axis=2)], axis=1)

    ang = two_pi * ((t2[:, None] * t2[None, :]) % FFT_T2) / FFT_T2
    s2 = 1.0 / math.sqrt(FFT_T2)
    f2 = np.concatenate([np.cos(ang) * s2, np.sin(ang) * s2], axis=1)

    p = np.arange(CTX_LEN)
    ang = two_pi * ((p[:, None] * p[None, :]) % CTX_LEN) / CTX_LEN
    sc = 1.0 / math.sqrt(CTX_LEN)
    f_ctx = np.concatenate([np.cos(ang) * sc, np.sin(ang) * sc], axis=1)
    return tuple(jnp.asarray(a, dtype=F32).astype(BF16) for a in (f_ch, g1, f2, f_ctx))


def _cdft_kernel(ctx_tile, *refs):
    if ctx_tile:
        x_ref, mod_ref, f_ref, zrx_ref, zix_ref, zrc_ref, zic_ref = refs
    else:
        x_ref, mod_ref, f_ref, zrx_ref, zix_ref = refs
    m = mod_ref[0]
    h = (_ln(x_ref[0]) * (1.0 + _mod_chunk(m, 1)) + _mod_chunk(m, 0)).astype(BF16)
    f = f_ref[...]
    zr, zi = [], []
    for g in range(F_GROUPS):
        z = _dot(h[:, g * F_GROUP_DIM:(g + 1) * F_GROUP_DIM], f).astype(BF16)
        zr.append(z[:, :F_GROUP_DIM])
        zi.append(z[:, F_GROUP_DIM:])
    zr = jnp.concatenate(zr, axis=1)
    zi = jnp.concatenate(zi, axis=1)
    if ctx_tile:
        t = pl.program_id(1)

        @pl.when(t == 0)
        def _():
            zrc_ref[0] = zr
            zic_ref[0] = zi

        @pl.when(t > 0)
        def _():
            zrx_ref[0] = zr
            zix_ref[0] = zi
    else:
        zrx_ref[0] = zr
        zix_ref[0] = zi


def _channel_dft(xs, mod_l, f_ch, ctx_tile):
    n_batch, rows, _ = xs.shape
    seq = rows - CTX_LEN
    off = CTX_LEN // TM
    zx = jax.ShapeDtypeStruct((n_batch, seq, D_MODEL), BF16)
    zc = jax.ShapeDtypeStruct((n_batch, CTX_LEN, D_MODEL), BF16)
    if ctx_tile:
        n_t = rows // TM
        x_out = pl.BlockSpec((1, TM, D_MODEL), lambda b, t: (b, jnp.maximum(t - off, 0), 0))
        c_out = pl.BlockSpec((1, CTX_LEN, D_MODEL), lambda b, t: (b, 0, 0))
        out_shape, out_specs, x_in = (zx, zx, zc, zc), (x_out, x_out, c_out, c_out), _tile_spec(D_MODEL)
    else:
        n_t = seq // TM
        out_shape, out_specs, x_in = (zx, zx), (_tile_spec(D_MODEL), _tile_spec(D_MODEL)), _tile_spec(D_MODEL, off)
    return pl.pallas_call(
        functools.partial(_cdft_kernel, ctx_tile),
        out_shape=out_shape,
        grid=(n_batch, n_t),
        in_specs=[x_in, _mod_spec(n_batch, ctx_tile), _const_spec((F_GROUP_DIM, 2 * F_GROUP_DIM))],
        out_specs=out_specs,
        compiler_params=_params("parallel", "arbitrary"),
        name="fourier_channel_dft",
    )(xs, mod_l, f_ch)


def _seq_dft_kernel(zr_ref, zi_ref, g1_ref, f2_ref, o_ref, sr_ref, si_ref):
    sr_ref[...] = zr_ref[0].astype(F32)
    si_ref[...] = zi_ref[0].astype(F32)

    def stage1(t2, carry):
        rows = pl.ds(t2, FFT_T1, stride=FFT_T2)
        z = jnp.concatenate([sr_ref[rows, :], si_ref[rows, :]], axis=0).astype(BF16)
        v = _dot(g1_ref[t2], z)
        sr_ref[rows, :] = v[:FFT_T1]
        si_ref[rows, :] = v[FFT_T1:]
        return carry

    lax.fori_loop(0, FFT_T2, stage1, 0, unroll=DFT_UNROLL)

    def stage2(t1, carry):
        rows = pl.ds(pl.multiple_of(t1 * FFT_T2, FFT_T2), FFT_T2)
        v = jnp.concatenate([sr_ref[rows, :], si_ref[rows, :]], axis=0).astype(BF16)
        o_ref[0, pl.ds(t1, FFT_T2, stride=FFT_T1), :] = _dot(f2_ref[...], v)
        return carry

    lax.fori_loop(0, FFT_T1, stage2, 0, unroll=DFT_UNROLL)


def _seq_dft(zr, zi, g1, f2):
    n_batch, seq, _ = zr.shape
    blk = pl.BlockSpec((1, seq, FFT_CB), lambda b, j: (b, 0, j))
    return pl.pallas_call(
        _seq_dft_kernel,
        out_shape=jax.ShapeDtypeStruct((n_batch, seq, D_MODEL), F32),
        grid=(n_batch, D_MODEL // FFT_CB),
        in_specs=[blk, blk, _const_spec((FFT_T2, 2 * FFT_T1, 2 * FFT_T1)), _const_spec((FFT_T2, 2 * FFT_T2))],
        out_specs=blk,
        scratch_shapes=[pltpu.VMEM((seq, FFT_CB), F32), pltpu.VMEM((seq, FFT_CB), F32)],
        compiler_params=_params("parallel", "parallel"),
        name="fourier_seq_dft",
    )(zr, zi, g1, f2)


def _ctx_dft_kernel(zr_ref, zi_ref, f_ref, o_ref):
    o_ref[0] = _dot(f_ref[...], jnp.concatenate([zr_ref[0], zi_ref[0]], axis=0))


def _ctx_dft(zr, zi, f_ctx):
    n_batch = zr.shape[0]
    blk = pl.BlockSpec((1, CTX_LEN, D_MODEL), lambda b: (b, 0, 0))
    return pl.pallas_call(
        _ctx_dft_kernel,
        out_shape=jax.ShapeDtypeStruct((n_batch, CTX_LEN, D_MODEL), F32),
        grid=(n_batch,),
        in_specs=[blk, blk, _const_spec((CTX_LEN, 2 * CTX_LEN))],
        out_specs=blk,
        compiler_params=_params("parallel"),
        name="fourier_ctx_dft",
    )(zr, zi, f_ctx)


def _mlstm_layer(xs, mod_l, w_in, w_conv, b_conv, w_q, w_k, w_v, w_gate, b_gate, gn_w, skip, w_out, pg, pb):
    xm, z = _in_proj(xs, mod_l, w_in.astype(BF16))
    wg = w_gate.reshape(3, M_HEADS, M_HEAD_DIM, N_GATES).transpose(1, 3, 0, 2).reshape(
        M_HEADS, N_GATES, 3 * M_HEAD_DIM)
    xc, q, k, v, pre_t = _conv_qkv(
        xm, w_conv.reshape(CONV_K * CONV_K, D_INNER), b_conv.reshape(1, D_INNER),
        w_q.astype(BF16), w_k.astype(BF16), w_v.astype(BF16), wg.astype(BF16), b_gate.reshape(N_GATES, 1))
    hf, hb = _mlstm_scan(q, k, v, _gate_scans(pre_t))
    return _mlstm_readout(hf, hb, xc, z, xs, mod_l, gn_w.reshape(1, D_INNER), skip.reshape(1, D_INNER),
                          w_out.astype(BF16), pg, pb)


def _fourier_layer(xs, mod_l, w_out, pg, pb, need_ctx, tables):
    f_ch, g1, f2, f_ctx = tables
    z = _channel_dft(xs, mod_l, f_ch, need_ctx)
    mixed_x = _seq_dft(z[0], z[1], g1, f2)
    mixed_c = _ctx_dft(z[2], z[3], f_ctx) if need_ctx else None
    return _fourier_out(mixed_x, mixed_c, xs, mod_l, w_out.astype(BF16), pg, pb)


def kernel(x, c, ctx, c_ctx, ada_w, ada_b, post_g, post_b, m_w_in, m_w_conv, m_b_conv, m_w_q, m_w_k, m_w_v, m_w_gate, m_b_gate, m_gn_w, m_skip, m_w_out, f_w_out, mlp_w1, mlp_w2):
    n_batch = x.shape[0]
    assert n_batch < MOD_ROWS and x.shape[1] == FFT_T1 * FFT_T2 and ctx.shape[1] == CTX_LEN == TM
    xs = (ctx, x)
    s_in = jnp.concatenate([c, c_ctx[None, :], jnp.zeros((MOD_ROWS - n_batch - 1, D_MODEL), F32)], axis=0)
    mod = _ada_table(s_in, ada_w, ada_b)
    tables = _dft_tables()
    for i in range(DEPTH):
        is_mlstm = i % N_MIXERS == 0
        j = i // N_MIXERS
        need_ctx = i < DEPTH - 1
        mod_l = mod[i].reshape(MOD_ROWS, 1, 6 * D_MODEL)
        pg = post_g[i].reshape(2, 1, D_MODEL)
        pb = post_b[i].reshape(2, 1, D_MODEL)
        if is_mlstm:
            xs = _mlstm_layer(xs, mod_l, m_w_in[j], m_w_conv[j], m_b_conv[j], m_w_q[j], m_w_k[j], m_w_v[j],
                              m_w_gate[j], m_b_gate[j], m_gn_w[j], m_skip[j], m_w_out[j], pg[0], pb[0])
            if not need_ctx:
                xs = xs[:, CTX_LEN:]
        else:
            xs = _fourier_layer(xs, mod_l, f_w_out[j], pg[0], pb[0], need_ctx, tables)
        xs = _mlp(xs, mod_l, mlp_w1[i].astype(BF16), mlp_w2[i].astype(BF16), pg[1], pb[1], need_ctx)
    return xs
```

```python
import functools
import math

import numpy as np
import jax
import jax.numpy as jnp
from jax import lax
from jax.experimental import pallas as pl
from jax.experimental.pallas import tpu as pltpu

D_MODEL = 1024
DEPTH = 4
GRID_W = 64
CTX_LEN = 256
N_MIXERS = 2
D_INNER = 2 * D_MODEL
M_HEADS = 4
M_HEAD_DIM = D_INNER // M_HEADS
CONV_K = 3
F_GROUPS = 4
F_GROUP_DIM = D_MODEL // F_GROUPS
D_FF = 4 * D_MODEL
ALPHA = float((2 * DEPTH) ** 0.25)
LN_EPS = 1e-5

TM = 256
N_GATES = 4 * M_HEADS
GATE_ROWS = 8
MOD_ROWS = 8
FFT_T1 = 64
FFT_T2 = 128
FFT_CB = 128
GATE_CHUNKS = 3
N_ROWS = 16
STATE_ROWS = N_ROWS + M_HEAD_DIM
MXU_WIDTH = 256
SCAN_HEADS = 2
DFT_UNROLL = 8
VMEM_LIMIT = 52 * 1024 * 1024

F32 = jnp.float32
BF16 = jnp.bfloat16


def _params(*sem):
    return pltpu.CompilerParams(dimension_semantics=sem, vmem_limit_bytes=VMEM_LIMIT)


def _ln(x):
    mu = jnp.mean(x, axis=-1, keepdims=True)
    xc = x - mu
    var = jnp.mean(xc * xc, axis=-1, keepdims=True)
    return xc * lax.rsqrt(var + LN_EPS)


def _silu(x):
    return x / (1.0 + jnp.exp(-x))


def _dot(a, b):
    return jnp.dot(a, b, preferred_element_type=F32)


def _dot_nt(a, b):
    return lax.dot_general(a, b, (((1,), (1,)), ((), ())), preferred_element_type=F32)


def _mod_chunk(m, j):
    return m[:, j * D_MODEL:(j + 1) * D_MODEL]


def _const_spec(shape):
    nd = len(shape)
    return pl.BlockSpec(shape, lambda *_: (0,) * nd, pipeline_mode=pl.Buffered(1))


def _mod_spec(n_batch, ctx_tile):
    if ctx_tile:
        return pl.BlockSpec((1, 1, 6 * D_MODEL), lambda b, t: (jnp.where(t == 0, n_batch, b), 0, 0))
    return pl.BlockSpec((1, 1, 6 * D_MODEL), lambda b, t: (b, 0, 0))


def _tile_spec(width, off=0):
    return pl.BlockSpec((1, TM, width), lambda b, t: (b, t + off, 0))


def _stream_specs(stream):
    if isinstance(stream, tuple):
        return [pl.BlockSpec((1, CTX_LEN, D_MODEL), lambda b, t: (b, 0, 0)),
                pl.BlockSpec((1, TM, D_MODEL), lambda b, t: (b, jnp.maximum(t - CTX_LEN // TM, 0), 0))], list(stream)
    return [_tile_spec(D_MODEL)], [stream]


def _stream_shape(stream):
    if isinstance(stream, tuple):
        return stream[1].shape[0], stream[0].shape[1] + stream[1].shape[1]
    return stream.shape[0], stream.shape[1]


def _stream_tile(refs):
    if len(refs) == 2:
        return jnp.where(pl.program_id(1) == 0, refs[0][0], refs[1][0])
    return refs[0][0]


def _ada_kernel(s_ref, w_ref, b_ref, o_ref):
    s = _silu(s_ref[...])
    o_ref[0] = jnp.dot(s, w_ref[0], preferred_element_type=F32, precision=lax.Precision.HIGHEST) + b_ref[0]


def _ada_table(s_in, ada_w, ada_b):
    n_chunks = 6
    return pl.pallas_call(
        _ada_kernel,
        out_shape=jax.ShapeDtypeStruct((DEPTH, MOD_ROWS, 6 * D_MODEL), F32),
        grid=(DEPTH, n_chunks),
        in_specs=[pl.BlockSpec((MOD_ROWS, D_MODEL), lambda l, j: (0, 0)),
                  pl.BlockSpec((1, D_MODEL, D_MODEL), lambda l, j: (l, 0, j)),
                  pl.BlockSpec((1, 1, D_MODEL), lambda l, j: (l, 0, j))],
        out_specs=pl.BlockSpec((1, MOD_ROWS, D_MODEL), lambda l, j: (l, 0, j)),
        compiler_params=_params("parallel", "parallel"),
        name="ada_table",
    )(s_in, ada_w, ada_b.reshape(DEPTH, 1, 6 * D_MODEL))


def _in_kernel(n_stream, *refs):
    mod_ref, w_ref, xm_ref, z_ref = refs[n_stream:]
    m = mod_ref[0]
    h = (_ln(_stream_tile(refs[:n_stream])) * (1.0 + _mod_chunk(m, 1)) + _mod_chunk(m, 0)).astype(BF16)
    for j in range(2):
        sl = slice(j * D_MODEL, (j + 1) * D_MODEL)
        xm_ref[0, :, sl] = _dot(h, w_ref[:, sl]).astype(BF16)
        z_ref[0, :, sl] = _dot(h, w_ref[:, D_INNER + j * D_MODEL:D_INNER + (j + 1) * D_MODEL]).astype(BF16)


def _in_proj(stream, mod_l, w_in):
    n_batch, rows = _stream_shape(stream)
    specs, args = _stream_specs(stream)
    out = jax.ShapeDtypeStruct((n_batch, rows, D_INNER), BF16)
    return pl.pallas_call(
        functools.partial(_in_kernel, len(args)),
        out_shape=(out, out),
        grid=(n_batch, rows // TM),
        in_specs=specs + [_mod_spec(n_batch, True), _const_spec((D_MODEL, 2 * D_INNER))],
        out_specs=(_tile_spec(D_INNER), _tile_spec(D_INNER)),
        compiler_params=_params("parallel", "parallel"),
        name="mlstm_in_proj",
    )(*args, mod_l, w_in)


def _conv_qkv_kernel(xm_ref, prev_ref, next_ref, wc_ref, bc_ref, wq_ref, wk_ref, wv_ref, wg_ref, bg_ref,
                     xc_ref, q_ref, k_ref, v_ref, pre_ref):
    t = pl.program_id(1)
    n_t = pl.num_programs(1)
    is_ctx = t == 0
    row = lax.broadcasted_iota(jnp.int32, (TM, 1), 0)
    period = jnp.where(is_ctx, CTX_LEN - 1, GRID_W - 1)
    pos = row & period
    has_left = pos != 0
    has_right = pos != period
    prev_ok = t >= 2
    next_ok = jnp.logical_and(t >= 1, t <= n_t - 2)
    row_w = jnp.where(is_ctx, 0.0, 1.0)
    pre_t = jnp.zeros((N_GATES, TM), F32) + bg_ref[...]
    for h in range(M_HEADS):
        sl = slice(h * M_HEAD_DIM, (h + 1) * M_HEAD_DIM)
        xm = xm_ref[0, :, sl]
        e = jnp.concatenate([jnp.where(prev_ok, prev_ref[0, :, sl], 0).astype(F32),
                             xm.astype(F32),
                             jnp.where(next_ok, next_ref[0, :, sl], 0).astype(F32)], axis=0)
        p = []
        for dc in range(CONV_K):
            acc = None
            for dr in range(CONV_K):
                w = wc_ref[CONV_K * dr + dc:CONV_K * dr + dc + 1, sl]
                if dr != CONV_K // 2:
                    w = w * row_w
                term = w * e[dr * GRID_W:dr * GRID_W + TM]
                acc = term if acc is None else acc + term
            p.append(acc)
        y = (p[1] + bc_ref[:, sl]
             + jnp.where(has_left, pltpu.roll(p[0], 1, axis=0), 0.0)
             + jnp.where(has_right, pltpu.roll(p[2], TM - 1, axis=0), 0.0))
        xc = _silu(y).astype(BF16)
        xc_ref[0, :, sl] = xc
        q_t = _dot_nt(wq_ref[h], xc).astype(BF16)
        k = (_dot(xc, wk_ref[h]) * (M_HEAD_DIM ** -0.5)).astype(BF16)
        v_t = _dot_nt(wv_ref[h], xm).astype(BF16)
        q_ref[0, sl, :] = q_t
        k_ref[0, :, sl] = k
        v_ref[0, sl, :] = v_t
        pre_t = pre_t + _dot(wg_ref[h, 0], q_t) + _dot_nt(wg_ref[h, 1], k) + _dot(wg_ref[h, 2], v_t)
    pre_ref[0] = pre_t


def _conv_qkv(xm, w_conv, b_conv, w_q, w_k, w_v, w_g, b_g):
    n_batch, rows, _ = xm.shape
    n_halo = rows // GRID_W
    per = TM // GRID_W
    wide = jax.ShapeDtypeStruct((n_batch, rows, D_INNER), BF16)
    wide_t = jax.ShapeDtypeStruct((n_batch, D_INNER, rows), BF16)
    col_spec = pl.BlockSpec((1, D_INNER, TM), lambda b, t: (b, 0, t))
    return pl.pallas_call(
        _conv_qkv_kernel,
        out_shape=(wide, wide_t, wide, wide_t, jax.ShapeDtypeStruct((n_batch, N_GATES, rows), F32)),
        grid=(n_batch, rows // TM),
        in_specs=[_tile_spec(D_INNER),
                  pl.BlockSpec((1, GRID_W, D_INNER), lambda b, t: (b, jnp.maximum(per * t - 1, 0), 0)),
                  pl.BlockSpec((1, GRID_W, D_INNER), lambda b, t: (b, jnp.minimum(per * t + per, n_halo - 1), 0)),
                  _const_spec((CONV_K * CONV_K, D_INNER)),
                  _const_spec((1, D_INNER)),
                  _const_spec((M_HEADS, M_HEAD_DIM, M_HEAD_DIM)),
                  _const_spec((M_HEADS, M_HEAD_DIM, M_HEAD_DIM)),
                  _const_spec((M_HEADS, M_HEAD_DIM, M_HEAD_DIM)),
                  _const_spec((M_HEADS, 3, N_GATES, M_HEAD_DIM)),
                  _const_spec((N_GATES, 1))],
        out_specs=(_tile_spec(D_INNER), col_spec, _tile_spec(D_INNER), col_spec,
                   pl.BlockSpec((1, N_GATES, TM), lambda b, t: (b, 0, t))),
        compiler_params=_params("parallel", "parallel"),
        name="mlstm_conv_qkv",
    )(xm, xm, xm, w_conv, b_conv, w_q, w_k, w_v, w_g, b_g)


def _scan_lanes(x, combine, fill, reverse):
    n = x.shape[-1]
    lane = lax.broadcasted_iota(jnp.int32, x.shape, 1)
    s = 1
    while s < n:
        if reverse:
            sh = jnp.where(lane < n - s, pltpu.roll(x, n - s, axis=1), fill)
        else:
            sh = jnp.where(lane >= s, pltpu.roll(x, s, axis=1), fill)
        x = combine(x, sh)
        s *= 2
    return x


def _gate_kernel(pre_ref, o_ref):
    zeros = jnp.zeros((GATE_ROWS - 3, TM), F32)
    for c in range(GATE_CHUNKS):
        lanes = slice(c * TM, (c + 1) * TM)
        for d in range(2):
            blk = pre_ref[0, 2 * M_HEADS * d:2 * M_HEADS * (d + 1), lanes]
            log_f = jnp.minimum(blk, 0.0) - jnp.log1p(jnp.exp(-jnp.abs(blk)))
            b = pltpu.roll(_scan_lanes(log_f, jnp.add, 0.0, reverse=d == 1), M_HEADS, axis=0)
            g = blk - b
            big = _scan_lanes(g, jnp.maximum, -jnp.inf, reverse=d == 1)
            for h in range(M_HEADS):
                o_ref[0, d, h, :, lanes] = jnp.concatenate([b[h:h + 1], g[h:h + 1], big[h:h + 1], zeros], axis=0)


def _gate_scans(pre_t):
    n_batch, _, rows = pre_t.shape
    width = GATE_CHUNKS * TM
    return pl.pallas_call(
        _gate_kernel,
        out_shape=jax.ShapeDtypeStruct((n_batch, 2, M_HEADS, GATE_ROWS, rows), F32),
        grid=(n_batch, rows // width),
        in_specs=[pl.BlockSpec((1, N_GATES, width), lambda b, t: (b, 0, t))],
        out_specs=pl.BlockSpec((1, 2, M_HEADS, GATE_ROWS, width), lambda b, t: (b, 0, 0, 0, t)),
        compiler_params=_params("parallel", "parallel"),
        name="mlstm_gate_scans",
    )(pre_t)


def _scan_chunk(d, hh, q_ref, k_ref, v_ref, gate_ref, h_ref, cn_ref, cnb_ref, m_ref):
    sl = slice(hh * M_HEAD_DIM, (hh + 1) * M_HEAD_DIM)
    q_t = q_ref[0, sl, :]
    k = k_ref[0, :, sl]
    v_t = v_ref[0, sl, :]
    gates = gate_ref[0, 0, hh]
    b_row, g_row, big_row = gates[0:1], gates[1:2], gates[2:3]
    m_prev = m_ref[d, hh, 0:1, 0:1]
    last = TM - 1 if d == 0 else 0
    m_row = jnp.maximum(m_prev, big_row)
    m_last = m_row[:, last:last + 1]
    w_inter = jnp.exp(m_prev - m_row)
    den_floor = jnp.exp(-(b_row + m_row))
    w_key = jnp.exp(g_row - m_last).astype(BF16)
    g_col = jnp.transpose(gates)[:, 1:2]
    jj = lax.broadcasted_iota(jnp.int32, (TM, TM), 0)
    ii = lax.broadcasted_iota(jnp.int32, (TM, TM), 1)
    visible = jj <= ii if d == 0 else jj >= ii
    pad = jnp.zeros((N_ROWS - 1, TM), BF16)

    s_t = _dot(k, q_t) * jnp.where(visible, jnp.exp(g_col - m_row), 0.0)
    inter = _dot(cnb_ref[d, hh], q_t)
    v_one = jnp.concatenate([jnp.ones((1, TM), BF16), pad, v_t], axis=0)
    intra = _dot(v_one, s_t.astype(BF16))
    den = w_inter * inter[0:1] + intra[0:1]
    scale = 1.0 / jnp.maximum(jnp.abs(den), den_floor)
    h_t = (w_inter * scale) * inter[N_ROWS:] + scale * intra[N_ROWS:]
    h_ref[0, :, sl] = jnp.transpose(h_t).astype(BF16)

    decay = jnp.exp(m_prev - m_last)
    vw = jnp.concatenate([w_key, pad, v_t * w_key], axis=0)
    for c0 in range(0, M_HEAD_DIM, MXU_WIDTH):
        cn = decay * cn_ref[d, hh, :, c0:c0 + MXU_WIDTH] + _dot(vw, k[:, c0:c0 + MXU_WIDTH])
        cn_ref[d, hh, :, c0:c0 + MXU_WIDTH] = cn
        cnb_ref[d, hh, :, c0:c0 + MXU_WIDTH] = cn.astype(BF16)
    m_ref[d, hh] = jnp.zeros(m_ref.shape[2:], F32) + (b_row[:, last:last + 1] + m_last)


def _scan_kernel(qf_ref, kf_ref, vf_ref, gf_ref, qb_ref, kb_ref, vb_ref, gb_ref, hf_ref, hb_ref,
                 cn_ref, cnb_ref, m_ref):
    @pl.when(pl.program_id(2) == 0)
    def _():
        cn_ref[...] = jnp.zeros_like(cn_ref)
        cnb_ref[...] = jnp.zeros_like(cnb_ref)
        m_ref[...] = jnp.zeros_like(m_ref)

    for hh in range(SCAN_HEADS):
        _scan_chunk(0, hh, qf_ref, kf_ref, vf_ref, gf_ref, hf_ref, cn_ref, cnb_ref, m_ref)
        _scan_chunk(1, hh, qb_ref, kb_ref, vb_ref, gb_ref, hb_ref, cn_ref, cnb_ref, m_ref)


def _mlstm_scan(q_t, k, v_t, gates):
    n_batch, rows, _ = k.shape
    n_t = rows // TM
    width = SCAN_HEADS * M_HEAD_DIM

    def bwd_tile(c):
        return jnp.where(c == 0, 0, n_t - c)

    kf_spec = pl.BlockSpec((1, TM, width), lambda b, h, c: (b, c, h))
    kb_spec = pl.BlockSpec((1, TM, width), lambda b, h, c: (b, bwd_tile(c), h))
    tf_spec = pl.BlockSpec((1, width, TM), lambda b, h, c: (b, h, c))
    tb_spec = pl.BlockSpec((1, width, TM), lambda b, h, c: (b, h, bwd_tile(c)))
    gf_spec = pl.BlockSpec((1, 1, SCAN_HEADS, GATE_ROWS, TM), lambda b, h, c: (b, 0, h, 0, c))
    gb_spec = pl.BlockSpec((1, 1, SCAN_HEADS, GATE_ROWS, TM), lambda b, h, c: (b, 1, h, 0, bwd_tile(c)))
    out = jax.ShapeDtypeStruct((n_batch, rows, D_INNER), BF16)
    return pl.pallas_call(
        _scan_kernel,
        out_shape=(out, out),
        grid=(n_batch, M_HEADS // SCAN_HEADS, n_t),
        in_specs=[tf_spec, kf_spec, tf_spec, gf_spec, tb_spec, kb_spec, tb_spec, gb_spec],
        out_specs=(kf_spec, kb_spec),
        scratch_shapes=[pltpu.VMEM((2, SCAN_HEADS, STATE_ROWS, M_HEAD_DIM), F32),
                        pltpu.VMEM((2, SCAN_HEADS, STATE_ROWS, M_HEAD_DIM), BF16),
                        pltpu.VMEM((2, SCAN_HEADS, 8, 128), F32)],
        compiler_params=_params("parallel", "parallel", "arbitrary"),
        name="mlstm_scan",
    )(q_t, k, v_t, gates, q_t, k, v_t, gates)


def _post_norm(x, y, gate, pg, pb):
    return _ln(ALPHA * x + gate * y) * pg + pb


def _readout_kernel(n_stream, hf_ref, hb_ref, xc_ref, z_ref, *refs):
    mod_ref, gn_ref, sk_ref, w_ref, pg_ref, pb_ref, o_ref = refs[n_stream:]
    parts = []
    for h in range(M_HEADS):
        sl = slice(h * M_HEAD_DIM, (h + 1) * M_HEAD_DIM)
        hn = _ln(hf_ref[0, :, sl].astype(F32) + hb_ref[0, :, sl].astype(F32))
        a = hn * gn_ref[:, sl] + sk_ref[:, sl] * xc_ref[0, :, sl].astype(F32)
        parts.append((a * _silu(z_ref[0, :, sl].astype(F32))).astype(BF16))
    y = _dot(jnp.concatenate(parts, axis=1), w_ref[...])
    o_ref[0] = _post_norm(_stream_tile(refs[:n_stream]), y, _mod_chunk(mod_ref[0], 2), pg_ref[...], pb_ref[...])


def _mlstm_readout(hf, hb, xc, z, stream, mod_l, gn_w, skip, w_out, pg, pb):
    n_batch, rows = _stream_shape(stream)
    specs, args = _stream_specs(stream)
    return pl.pallas_call(
        functools.partial(_readout_kernel, len(args)),
        out_shape=jax.ShapeDtypeStruct((n_batch, rows, D_MODEL), F32),
        grid=(n_batch, rows // TM),
        in_specs=[_tile_spec(D_INNER), _tile_spec(D_INNER), _tile_spec(D_INNER), _tile_spec(D_INNER)] + specs
        + [_mod_spec(n_batch, True),
           _const_spec((1, D_INNER)), _const_spec((1, D_INNER)), _const_spec((D_INNER, D_MODEL)),
           _const_spec((1, D_MODEL)), _const_spec((1, D_MODEL))],
        out_specs=_tile_spec(D_MODEL),
        compiler_params=_params("parallel", "parallel"),
        name="mlstm_readout",
    )(hf, hb, xc, z, *args, mod_l, gn_w, skip, w_out, pg, pb)


def _fourier_out_kernel(ctx_tile, *refs):
    if ctx_tile:
        mx_ref, mc_ref, x_ref, mod_ref, w_ref, pg_ref, pb_ref, o_ref = refs
        a = jnp.where(pl.program_id(1) == 0, mc_ref[0], mx_ref[0])
    else:
        mx_ref, x_ref, mod_ref, w_ref, pg_ref, pb_ref, o_ref = refs
        a = mx_ref[0]
    y = _dot(a.astype(BF16), w_ref[...])
    o_ref[0] = _post_norm(x_ref[0], y, _mod_chunk(mod_ref[0], 2), pg_ref[...], pb_ref[...])


def _fourier_out(mixed_x, mixed_c, xs, mod_l, w_out, pg, pb):
    n_batch = xs.shape[0]
    ctx_tile = mixed_c is not None
    n_t = mixed_x.shape[1] // TM + (1 if ctx_tile else 0)
    in_specs = [pl.BlockSpec((1, TM, D_MODEL), lambda b, t: (b, jnp.maximum(t - 1, 0) if ctx_tile else t, 0))]
    args = [mixed_x]
    if ctx_tile:
        in_specs.append(pl.BlockSpec((1, CTX_LEN, D_MODEL), lambda b, t: (b, 0, 0)))
        args.append(mixed_c)
    in_specs += [_tile_spec(D_MODEL, 0 if ctx_tile else CTX_LEN // TM), _mod_spec(n_batch, ctx_tile),
                 _const_spec((D_MODEL, D_MODEL)), _const_spec((1, D_MODEL)), _const_spec((1, D_MODEL))]
    return pl.pallas_call(
        functools.partial(_fourier_out_kernel, ctx_tile),
        out_shape=jax.ShapeDtypeStruct((n_batch, n_t * TM, D_MODEL), F32),
        grid=(n_batch, n_t),
        in_specs=in_specs,
        out_specs=_tile_spec(D_MODEL),
        compiler_params=_params("parallel", "parallel"),
        name="fourier_out",
    )(*args, xs, mod_l, w_out, pg, pb)


def _mlp_kernel(x_ref, mod_ref, w1_ref, w2_ref, pg_ref, pb_ref, o_ref):
    x = x_ref[0]
    m = mod_ref[0]
    h = (_ln(x) * (1.0 + _mod_chunk(m, 4)) + _mod_chunk(m, 3)).astype(BF16)
    acc = jnp.zeros((TM, D_MODEL), F32)
    for j in range(D_FF // D_MODEL):
        sl = slice(j * D_MODEL, (j + 1) * D_MODEL)
        u = jnp.square(jnp.maximum(_dot(h, w1_ref[:, sl]), 0.0)).astype(BF16)
        acc = acc + _dot(u, w2_ref[sl, :])
    o_ref[0] = _post_norm(x, acc, _mod_chunk(m, 5), pg_ref[...], pb_ref[...])


def _mlp(xs, mod_l, w1, w2, pg, pb, ctx_tile):
    n_batch, rows, _ = xs.shape
    return pl.pallas_call(
        _mlp_kernel,
        out_shape=jax.ShapeDtypeStruct((n_batch, rows, D_MODEL), F32),
        grid=(n_batch, rows // TM),
        in_specs=[_tile_spec(D_MODEL), _mod_spec(n_batch, ctx_tile),
                  _const_spec((D_MODEL, D_FF)), _const_spec((D_FF, D_MODEL)),
                  _const_spec((1, D_MODEL)), _const_spec((1, D_MODEL))],
        out_specs=_tile_spec(D_MODEL),
        compiler_params=_params("parallel", "parallel"),
        name="mlp",
    )(xs, mod_l, w1, w2, pg, pb)


def _dft_tables():
    two_pi = 2.0 * np.pi
    c = np.arange(F_GROUP_DIM)
    ang = two_pi * ((c[:, None] * c[None, :]) % F_GROUP_DIM) / F_GROUP_DIM
    s_ch = 1.0 / math.sqrt(F_GROUP_DIM)
    f_ch = np.concatenate([np.cos(ang) * s_ch, -np.sin(ang) * s_ch], axis=1)

    seq = FFT_T1 * FFT_T2
    t1 = np.arange(FFT_T1)
    t2 = np.arange(FFT_T2)
    num = (t1[None, :, None] * t1[None, None, :] * FFT_T2 + t2[:, None, None] * t1[None, :, None]) % seq
    ang = two_pi * num / seq
    s1 = 1.0 / math.sqrt(FFT_T1)
    gr, gi = np.cos(ang) * s1, -np.sin(ang) * s1
    g1 = np.concatenate([np.concatenate([gr, -gi], axis=2), np.concatenate([gi, gr], axis=2)], axis=1)

    ang = two_pi * ((t2[:, None] * t2[None, :]) % FFT_T2) / FFT_T2
    s2 = 1.0 / math.sqrt(FFT_T2)
    f2 = np.concatenate([np.cos(ang) * s2, np.sin(ang) * s2], axis=1)

    p = np.arange(CTX_LEN)
    ang = two_pi * ((p[:, None] * p[None, :]) % CTX_LEN) / CTX_LEN
    sc = 1.0 / math.sqrt(CTX_LEN)
    f_ctx = np.concatenate([np.cos(ang) * sc, np.sin(ang) * sc], axis=1)
    return tuple(jnp.asarray(a, dtype=F32).astype(BF16) for a in (f_ch, g1, f2, f_ctx))


def _cdft_kernel(ctx_tile, *refs):
    if ctx_tile:
        x_ref, mod_ref, f_ref, zrx_ref, zix_ref, zrc_ref, zic_ref = refs
    else:
        x_ref, mod_ref, f_ref, zrx_ref, zix_ref = refs
    m = mod_ref[0]
    h = (_ln(x_ref[0]) * (1.0 + _mod_chunk(m, 1)) + _mod_chunk(m, 0)).astype(BF16)
    f = f_ref[...]
    zr, zi = [], []
    for g in range(F_GROUPS):
        z = _dot(h[:, g * F_GROUP_DIM:(g + 1) * F_GROUP_DIM], f).astype(BF16)
        zr.append(z[:, :F_GROUP_DIM])
        zi.append(z[:, F_GROUP_DIM:])
    zr = jnp.concatenate(zr, axis=1)
    zi = jnp.concatenate(zi, axis=1)
    if ctx_tile:
        t = pl.program_id(1)

        @pl.when(t == 0)
        def _():
            zrc_ref[0] = zr
            zic_ref[0] = zi

        @pl.when(t > 0)
        def _():
            zrx_ref[0] = zr
            zix_ref[0] = zi
    else:
        zrx_ref[0] = zr
        zix_ref[0] = zi


def _channel_dft(xs, mod_l, f_ch, ctx_tile):
    n_batch, rows, _ = xs.shape
    seq = rows - CTX_LEN
    off = CTX_LEN // TM
    zx = jax.ShapeDtypeStruct((n_batch, seq, D_MODEL), BF16)
    zc = jax.ShapeDtypeStruct((n_batch, CTX_LEN, D_MODEL), BF16)
    if ctx_tile:
        n_t = rows // TM
        x_out = pl.BlockSpec((1, TM, D_MODEL), lambda b, t: (b, jnp.maximum(t - off, 0), 0))
        c_out = pl.BlockSpec((1, CTX_LEN, D_MODEL), lambda b, t: (b, 0, 0))
        out_shape, out_specs, x_in = (zx, zx, zc, zc), (x_out, x_out, c_out, c_out), _tile_spec(D_MODEL)
    else:
        n_t = seq // TM
        out_shape, out_specs, x_in = (zx, zx), (_tile_spec(D_MODEL), _tile_spec(D_MODEL)), _tile_spec(D_MODEL, off)
    return pl.pallas_call(
        functools.partial(_cdft_kernel, ctx_tile),
        out_shape=out_shape,
        grid=(n_batch, n_t),
        in_specs=[x_in, _mod_spec(n_batch, ctx_tile), _const_spec((F_GROUP_DIM, 2 * F_GROUP_DIM))],
        out_specs=out_specs,
        compiler_params=_params("parallel", "arbitrary"),
        name="fourier_channel_dft",
    )(xs, mod_l, f_ch)


def _seq_dft_kernel(zr_ref, zi_ref, g1_ref, f2_ref, o_ref, sr_ref, si_ref):
    sr_ref[...] = zr_ref[0].astype(F32)
    si_ref[...] = zi_ref[0].astype(F32)

    def stage1(t2, carry):
        rows = pl.ds(t2, FFT_T1, stride=FFT_T2)
        z = jnp.concatenate([sr_ref[rows, :], si_ref[rows, :]], axis=0).astype(BF16)
        v = _dot(g1_ref[t2], z)
        sr_ref[rows, :] = v[:FFT_T1]
        si_ref[rows, :] = v[FFT_T1:]
        return carry

    lax.fori_loop(0, FFT_T2, stage1, 0, unroll=DFT_UNROLL)

    def stage2(t1, carry):
        rows = pl.ds(pl.multiple_of(t1 * FFT_T2, FFT_T2), FFT_T2)
        v = jnp.concatenate([sr_ref[rows, :], si_ref[rows, :]], axis=0).astype(BF16)
        o_ref[0, pl.ds(t1, FFT_T2, stride=FFT_T1), :] = _dot(f2_ref[...], v)
        return carry

    lax.fori_loop(0, FFT_T1, stage2, 0, unroll=DFT_UNROLL)


def _seq_dft(zr, zi, g1, f2):
    n_batch, seq, _ = zr.shape
    blk = pl.BlockSpec((1, seq, FFT_CB), lambda b, j: (b, 0, j))
    return pl.pallas_call(
        _seq_dft_kernel,
        out_shape=jax.ShapeDtypeStruct((n_batch, seq, D_MODEL), F32),
        grid=(n_batch, D_MODEL // FFT_CB),
        in_specs=[blk, blk, _const_spec((FFT_T2, 2 * FFT_T1, 2 * FFT_T1)), _const_spec((FFT_T2, 2 * FFT_T2))],
        out_specs=blk,
        scratch_shapes=[pltpu.VMEM((seq, FFT_CB), F32), pltpu.VMEM((seq, FFT_CB), F32)],
        compiler_params=_params("parallel", "parallel"),
        name="fourier_seq_dft",
    )(zr, zi, g1, f2)


def _ctx_dft_kernel(zr_ref, zi_ref, f_ref, o_ref):
    o_ref[0] = _dot(f_ref[...], jnp.concatenate([zr_ref[0], zi_ref[0]], axis=0))


def _ctx_dft(zr, zi, f_ctx):
    n_batch = zr.shape[0]
    blk = pl.BlockSpec((1, CTX_LEN, D_MODEL), lambda b: (b, 0, 0))
    return pl.pallas_call(
        _ctx_dft_kernel,
        out_shape=jax.ShapeDtypeStruct((n_batch, CTX_LEN, D_MODEL), F32),
        grid=(n_batch,),
        in_specs=[blk, blk, _const_spec((CTX_LEN, 2 * CTX_LEN))],
        out_specs=blk,
        compiler_params=_params("parallel"),
        name="fourier_ctx_dft",
    )(zr, zi, f_ctx)


def _mlstm_layer(xs, mod_l, w_in, w_conv, b_conv, w_q, w_k, w_v, w_gate, b_gate, gn_w, skip, w_out, pg, pb):
    xm, z = _in_proj(xs, mod_l, w_in.astype(BF16))
    wg = w_gate.reshape(3, M_HEADS, M_HEAD_DIM, N_GATES).transpose(1, 0, 3, 2)
    w_qt = jnp.swapaxes(w_q, 1, 2).astype(BF16)
    w_vt = jnp.swapaxes(w_v, 1, 2).astype(BF16)
    xc, q_t, k, v_t, pre_t = _conv_qkv(
        xm, w_conv.reshape(CONV_K * CONV_K, D_INNER), b_conv.reshape(1, D_INNER),
        w_qt, w_k.astype(BF16), w_vt, wg.astype(BF16), b_gate.reshape(N_GATES, 1))
    hf, hb = _mlstm_scan(q_t, k, v_t, _gate_scans(pre_t))
    return _mlstm_readout(hf, hb, xc, z, xs, mod_l, gn_w.reshape(1, D_INNER), skip.reshape(1, D_INNER),
                          w_out.astype(BF16), pg, pb)


def _fourier_layer(xs, mod_l, w_out, pg, pb, need_ctx, tables):
    f_ch, g1, f2, f_ctx = tables
    z = _channel_dft(xs, mod_l, f_ch, need_ctx)
    mixed_x = _seq_dft(z[0], z[1], g1, f2)
    mixed_c = _ctx_dft(z[2], z[3], f_ctx) if need_ctx else None
    return _fourier_out(mixed_x, mixed_c, xs, mod_l, w_out.astype(BF16), pg, pb)


def kernel(x, c, ctx, c_ctx, ada_w, ada_b, post_g, post_b, m_w_in, m_w_conv, m_b_conv, m_w_q, m_w_k, m_w_v, m_w_gate, m_b_gate, m_gn_w, m_skip, m_w_out, f_w_out, mlp_w1, mlp_w2):
    n_batch = x.shape[0]
    assert n_batch < MOD_ROWS and x.shape[1] == FFT_T1 * FFT_T2 and ctx.shape[1] == CTX_LEN == TM
    xs = (ctx, x)
    s_in = jnp.concatenate([c, c_ctx[None, :], jnp.zeros((MOD_ROWS - n_batch - 1, D_MODEL), F32)], axis=0)
    mod = _ada_table(s_in, ada_w, ada_b)
    tables = _dft_tables()
    for i in range(DEPTH):
        is_mlstm = i % N_MIXERS == 0
        j = i // N_MIXERS
        need_ctx = i < DEPTH - 1
        mod_l = mod[i].reshape(MOD_ROWS, 1, 6 * D_MODEL)
        pg = post_g[i].reshape(2, 1, D_MODEL)
        pb = post_b[i].reshape(2, 1, D_MODEL)
        if is_mlstm:
            xs = _mlstm_layer(xs, mod_l, m_w_in[j], m_w_conv[j], m_b_conv[j], m_w_q[j], m_w_k[j], m_w_v[j],
                              m_w_gate[j], m_b_gate[j], m_gn_w[j], m_skip[j], m_w_out[j], pg[0], pb[0])
            if not need_ctx:
                xs = xs[:, CTX_LEN:]
        else:
            xs = _fourier_layer(xs, mod_l, f_w_out[j], pg[0], pb[0], need_ctx, tables)
        xs = _mlp(xs, mod_l, mlp_w1[i].astype(BF16), mlp_w2[i].astype(BF16), pg[1], pb[1], need_ctx)
    return xs
```

```python
import functools
import math

import numpy as np
import jax
import jax.numpy as jnp
from jax import lax
from jax.experimental import pallas as pl
from jax.experimental.pallas import tpu as pltpu

D_MODEL = 1024
DEPTH = 4
GRID_W = 64
CTX_LEN = 256
N_MIXERS = 2
D_INNER = 2 * D_MODEL
M_HEADS = 4
M_HEAD_DIM = D_INNER // M_HEADS
CONV_K = 3
F_GROUPS = 4
F_GROUP_DIM = D_MODEL // F_GROUPS
D_FF = 4 * D_MODEL
ALPHA = float((2 * DEPTH) ** 0.25)
LN_EPS = 1e-5

TM = 256
N_GATES = 4 * M_HEADS
GATE_ROWS = 8
MOD_ROWS = 8
FFT_T1 = 64
FFT_T2 = 128
FFT_CB = 128
GATE_CHUNKS = 3
N_ROWS = 16
STATE_ROWS = N_ROWS + M_HEAD_DIM
MXU_WIDTH = 256
SCAN_HEADS = 2
DFT_UNROLL = 8
VMEM_LIMIT = 52 * 1024 * 1024

F32 = jnp.float32
BF16 = jnp.bfloat16


def _params(*sem):
    return pltpu.CompilerParams(dimension_semantics=sem, vmem_limit_bytes=VMEM_LIMIT)


def _ln(x):
    mu = jnp.mean(x, axis=-1, keepdims=True)
    xc = x - mu
    var = jnp.mean(xc * xc, axis=-1, keepdims=True)
    return xc * lax.rsqrt(var + LN_EPS)


def _silu(x):
    return x / (1.0 + jnp.exp(-x))


def _dot(a, b):
    return jnp.dot(a, b, preferred_element_type=F32)


def _dot_nt(a, b):
    return lax.dot_general(a, b, (((1,), (1,)), ((), ())), preferred_element_type=F32)


def _mod_chunk(m, j):
    return m[:, j * D_MODEL:(j + 1) * D_MODEL]


def _const_spec(shape):
    nd = len(shape)
    return pl.BlockSpec(shape, lambda *_: (0,) * nd, pipeline_mode=pl.Buffered(1))


def _mod_spec(n_batch, ctx_tile):
    if ctx_tile:
        return pl.BlockSpec((1, 1, 6 * D_MODEL), lambda b, t: (jnp.where(t == 0, n_batch, b), 0, 0))
    return pl.BlockSpec((1, 1, 6 * D_MODEL), lambda b, t: (b, 0, 0))


def _tile_spec(width, off=0):
    return pl.BlockSpec((1, TM, width), lambda b, t: (b, t + off, 0))


def _stream_specs(stream):
    if isinstance(stream, tuple):
        return [pl.BlockSpec((1, CTX_LEN, D_MODEL), lambda b, t: (b, 0, 0)),
                pl.BlockSpec((1, TM, D_MODEL), lambda b, t: (b, jnp.maximum(t - CTX_LEN // TM, 0), 0))], list(stream)
    return [_tile_spec(D_MODEL)], [stream]


def _stream_shape(stream):
    if isinstance(stream, tuple):
        return stream[1].shape[0], stream[0].shape[1] + stream[1].shape[1]
    return stream.shape[0], stream.shape[1]


def _stream_tile(refs):
    if len(refs) == 2:
        return jnp.where(pl.program_id(1) == 0, refs[0][0], refs[1][0])
    return refs[0][0]


def _ada_kernel(s_ref, w_ref, b_ref, o_ref):
    s = _silu(s_ref[...])
    o_ref[0] = jnp.dot(s, w_ref[0], preferred_element_type=F32, precision=lax.Precision.HIGHEST) + b_ref[0]


def _ada_table(s_in, ada_w, ada_b):
    n_chunks = 6
    return pl.pallas_call(
        _ada_kernel,
        out_shape=jax.ShapeDtypeStruct((DEPTH, MOD_ROWS, 6 * D_MODEL), F32),
        grid=(DEPTH, n_chunks),
        in_specs=[pl.BlockSpec((MOD_ROWS, D_MODEL), lambda l, j: (0, 0)),
                  pl.BlockSpec((1, D_MODEL, D_MODEL), lambda l, j: (l, 0, j)),
                  pl.BlockSpec((1, 1, D_MODEL), lambda l, j: (l, 0, j))],
        out_specs=pl.BlockSpec((1, MOD_ROWS, D_MODEL), lambda l, j: (l, 0, j)),
        compiler_params=_params("parallel", "parallel"),
        name="ada_table",
    )(s_in, ada_w, ada_b.reshape(DEPTH, 1, 6 * D_MODEL))


def _in_kernel(n_stream, n_sub, ctx_row, *refs):
    mod_ref, w_ref, xm_ref, z_ref = refs[n_stream:]
    for r in range(n_sub):
        rs = slice(r * TM, (r + 1) * TM)
        x = _stream_tile(refs[:n_stream]) if n_stream == 2 else refs[0][0, rs, :]
        m = _tile_mod(mod_ref, ctx_row, r)
        h = (_ln(x) * (1.0 + _mod_chunk(m, 1)) + _mod_chunk(m, 0)).astype(BF16)
        for j in range(2):
            sl = slice(j * D_MODEL, (j + 1) * D_MODEL)
            xm_ref[0, rs, sl] = _dot(h, w_ref[:, sl]).astype(BF16)
            z_ref[0, rs, sl] = _dot(h, w_ref[:, D_INNER + j * D_MODEL:D_INNER + (j + 1) * D_MODEL]).astype(BF16)


def _in_proj(stream, mod_l, w_in):
    n_batch, rows = _stream_shape(stream)
    if isinstance(stream, tuple):
        n_sub = 1
        specs, args = _stream_specs(stream)
    else:
        n_sub = _tiles_per_step(rows)
        specs, args = [pl.BlockSpec((1, n_sub * TM, D_MODEL), lambda b, t: (b, t, 0))], [stream]
    out = jax.ShapeDtypeStruct((n_batch, rows, D_INNER), BF16)
    out_spec = pl.BlockSpec((1, n_sub * TM, D_INNER), lambda b, t: (b, t, 0))
    return pl.pallas_call(
        functools.partial(_in_kernel, len(args), n_sub, n_batch),
        out_shape=(out, out),
        grid=(n_batch, rows // (n_sub * TM)),
        in_specs=specs + [_const_spec((MOD_ROWS, 1, 6 * D_MODEL)), _const_spec((D_MODEL, 2 * D_INNER))],
        out_specs=(out_spec, out_spec),
        compiler_params=_params("parallel", "parallel"),
        name="mlstm_in_proj",
    )(*args, mod_l, w_in)


def _conv_qkv_kernel(xm_ref, prev_ref, next_ref, wc_ref, bc_ref, wq_ref, wk_ref, wv_ref, wg_ref, bg_ref,
                     xc_ref, q_ref, k_ref, v_ref, pre_ref):
    t = pl.program_id(1)
    n_t = pl.num_programs(1)
    is_ctx = t == 0
    row = lax.broadcasted_iota(jnp.int32, (TM, 1), 0)
    period = jnp.where(is_ctx, CTX_LEN - 1, GRID_W - 1)
    pos = row & period
    has_left = pos != 0
    has_right = pos != period
    prev_ok = t >= 2
    next_ok = jnp.logical_and(t >= 1, t <= n_t - 2)
    row_w = jnp.where(is_ctx, 0.0, 1.0)
    pre_t = jnp.zeros((N_GATES, TM), F32) + bg_ref[...]
    for h in range(M_HEADS):
        sl = slice(h * M_HEAD_DIM, (h + 1) * M_HEAD_DIM)
        xm = xm_ref[0, :, sl]
        e = jnp.concatenate([jnp.where(prev_ok, prev_ref[0, :, sl], 0).astype(F32),
                             xm.astype(F32),
                             jnp.where(next_ok, next_ref[0, :, sl], 0).astype(F32)], axis=0)
        p = []
        for dc in range(CONV_K):
            acc = None
            for dr in range(CONV_K):
                w = wc_ref[CONV_K * dr + dc:CONV_K * dr + dc + 1, sl]
                if dr != CONV_K // 2:
                    w = w * row_w
                term = w * e[dr * GRID_W:dr * GRID_W + TM]
                acc = term if acc is None else acc + term
            p.append(acc)
        y = (p[1] + bc_ref[:, sl]
             + jnp.where(has_left, pltpu.roll(p[0], 1, axis=0), 0.0)
             + jnp.where(has_right, pltpu.roll(p[2], TM - 1, axis=0), 0.0))
        xc = _silu(y).astype(BF16)
        xc_ref[0, :, sl] = xc
        q_t = _dot_nt(wq_ref[h], xc).astype(BF16)
        k = (_dot(xc, wk_ref[h]) * (M_HEAD_DIM ** -0.5)).astype(BF16)
        v_t = _dot_nt(wv_ref[h], xm).astype(BF16)
        q_ref[0, 0, sl, :] = q_t
        k_ref[0, :, sl] = k
        v_ref[0, 0, sl, :] = v_t
        pre_t = pre_t + _dot(wg_ref[h, 0], q_t) + _dot_nt(wg_ref[h, 1], k) + _dot(wg_ref[h, 2], v_t)
    pre_ref[0] = pre_t


def _conv_qkv(xm, w_conv, b_conv, w_q, w_k, w_v, w_g, b_g):
    n_batch, rows, _ = xm.shape
    n_halo = rows // GRID_W
    per = TM // GRID_W
    wide = jax.ShapeDtypeStruct((n_batch, rows, D_INNER), BF16)
    wide_t = jax.ShapeDtypeStruct((n_batch, rows // TM, D_INNER, TM), BF16)
    col_spec = pl.BlockSpec((1, 1, D_INNER, TM), lambda b, t: (b, t, 0, 0))
    return pl.pallas_call(
        _conv_qkv_kernel,
        out_shape=(wide, wide_t, wide, wide_t, jax.ShapeDtypeStruct((n_batch, N_GATES, rows), F32)),
        grid=(n_batch, rows // TM),
        in_specs=[_tile_spec(D_INNER),
                  pl.BlockSpec((1, GRID_W, D_INNER), lambda b, t: (b, jnp.maximum(per * t - 1, 0), 0)),
                  pl.BlockSpec((1, GRID_W, D_INNER), lambda b, t: (b, jnp.minimum(per * t + per, n_halo - 1), 0)),
                  _const_spec((CONV_K * CONV_K, D_INNER)),
                  _const_spec((1, D_INNER)),
                  _const_spec((M_HEADS, M_HEAD_DIM, M_HEAD_DIM)),
                  _const_spec((M_HEADS, M_HEAD_DIM, M_HEAD_DIM)),
                  _const_spec((M_HEADS, M_HEAD_DIM, M_HEAD_DIM)),
                  _const_spec((M_HEADS, 3, N_GATES, M_HEAD_DIM)),
                  _const_spec((N_GATES, 1))],
        out_specs=(_tile_spec(D_INNER), col_spec, _tile_spec(D_INNER), col_spec,
                   pl.BlockSpec((1, N_GATES, TM), lambda b, t: (b, 0, t))),
        compiler_params=_params("parallel", "parallel"),
        name="mlstm_conv_qkv",
    )(xm, xm, xm, w_conv, b_conv, w_q, w_k, w_v, w_g, b_g)


def _scan_lanes(x, combine, fill, reverse):
    n = x.shape[-1]
    lane = lax.broadcasted_iota(jnp.int32, x.shape, 1)
    s = 1
    while s < n:
        if reverse:
            sh = jnp.where(lane < n - s, pltpu.roll(x, n - s, axis=1), fill)
        else:
            sh = jnp.where(lane >= s, pltpu.roll(x, s, axis=1), fill)
        x = combine(x, sh)
        s *= 2
    return x


def _gate_kernel(pre_ref, o_ref):
    zeros = jnp.zeros((GATE_ROWS - 3, TM), F32)
    for c in range(GATE_CHUNKS):
        lanes = slice(c * TM, (c + 1) * TM)
        for d in range(2):
            blk = pre_ref[0, 2 * M_HEADS * d:2 * M_HEADS * (d + 1), lanes]
            log_f = jnp.minimum(blk, 0.0) - jnp.log1p(jnp.exp(-jnp.abs(blk)))
            b = pltpu.roll(_scan_lanes(log_f, jnp.add, 0.0, reverse=d == 1), M_HEADS, axis=0)
            g = blk - b
            big = _scan_lanes(g, jnp.maximum, -jnp.inf, reverse=d == 1)
            for h in range(M_HEADS):
                o_ref[0, d, h, :, lanes] = jnp.concatenate([b[h:h + 1], g[h:h + 1], big[h:h + 1], zeros], axis=0)


def _gate_scans(pre_t):
    n_batch, _, rows = pre_t.shape
    width = GATE_CHUNKS * TM
    return pl.pallas_call(
        _gate_kernel,
        out_shape=jax.ShapeDtypeStruct((n_batch, 2, M_HEADS, GATE_ROWS, rows), F32),
        grid=(n_batch, rows // width),
        in_specs=[pl.BlockSpec((1, N_GATES, width), lambda b, t: (b, 0, t))],
        out_specs=pl.BlockSpec((1, 2, M_HEADS, GATE_ROWS, width), lambda b, t: (b, 0, 0, 0, t)),
        compiler_params=_params("parallel", "parallel"),
        name="mlstm_gate_scans",
    )(pre_t)


def _scan_chunk(d, hh, q_ref, k_ref, v_ref, gate_ref, h_ref, cn_ref, cnb_ref, m_ref):
    sl = slice(hh * M_HEAD_DIM, (hh + 1) * M_HEAD_DIM)
    q_t = q_ref[0, 0, sl, :]
    k = k_ref[0, :, sl]
    v_t = v_ref[0, 0, sl, :]
    gates = gate_ref[0, 0, hh]
    b_row, g_row, big_row = gates[0:1], gates[1:2], gates[2:3]
    m_prev = m_ref[d, hh, 0:1, 0:1]
    last = TM - 1 if d == 0 else 0
    m_row = jnp.maximum(m_prev, big_row)
    m_last = m_row[:, last:last + 1]
    w_inter = jnp.exp(m_prev - m_row)
    den_floor = jnp.exp(-(b_row + m_row))
    w_key = jnp.exp(g_row - m_last).astype(BF16)
    g_col = jnp.transpose(gates)[:, 1:2]
    jj = lax.broadcasted_iota(jnp.int32, (TM, TM), 0)
    ii = lax.broadcasted_iota(jnp.int32, (TM, TM), 1)
    visible = jj <= ii if d == 0 else jj >= ii
    pad = jnp.zeros((N_ROWS - 1, TM), BF16)

    s_t = _dot(k, q_t) * jnp.where(visible, jnp.exp(g_col - m_row), 0.0)
    inter = _dot(cnb_ref[d, hh], q_t)
    v_one = jnp.concatenate([jnp.ones((1, TM), BF16), pad, v_t], axis=0)
    intra = _dot(v_one, s_t.astype(BF16))
    den = w_inter * inter[0:1] + intra[0:1]
    scale = 1.0 / jnp.maximum(jnp.abs(den), den_floor)
    h_t = (w_inter * scale) * inter[N_ROWS:] + scale * intra[N_ROWS:]
    h_ref[0, :, sl] = jnp.transpose(h_t).astype(BF16)

    decay = jnp.exp(m_prev - m_last)
    vw = jnp.concatenate([w_key, pad, v_t * w_key], axis=0)
    for c0 in range(0, M_HEAD_DIM, MXU_WIDTH):
        cn = decay * cn_ref[d, hh, :, c0:c0 + MXU_WIDTH] + _dot(vw, k[:, c0:c0 + MXU_WIDTH])
        cn_ref[d, hh, :, c0:c0 + MXU_WIDTH] = cn
        cnb_ref[d, hh, :, c0:c0 + MXU_WIDTH] = cn.astype(BF16)
    m_ref[d, hh] = jnp.zeros(m_ref.shape[2:], F32) + (b_row[:, last:last + 1] + m_last)


def _scan_kernel(qf_ref, kf_ref, vf_ref, gf_ref, qb_ref, kb_ref, vb_ref, gb_ref, hf_ref, hb_ref,
                 cn_ref, cnb_ref, m_ref):
    @pl.when(pl.program_id(2) == 0)
    def _():
        cn_ref[...] = jnp.zeros_like(cn_ref)
        cnb_ref[...] = jnp.zeros_like(cnb_ref)
        m_ref[...] = jnp.zeros_like(m_ref)

    for hh in range(SCAN_HEADS):
        _scan_chunk(0, hh, qf_ref, kf_ref, vf_ref, gf_ref, hf_ref, cn_ref, cnb_ref, m_ref)
        _scan_chunk(1, hh, qb_ref, kb_ref, vb_ref, gb_ref, hb_ref, cn_ref, cnb_ref, m_ref)


def _mlstm_scan(q_t, k, v_t, gates):
    n_batch, rows, _ = k.shape
    n_t = rows // TM
    width = SCAN_HEADS * M_HEAD_DIM

    def bwd_tile(c):
        return jnp.where(c == 0, 0, n_t - c)

    kf_spec = pl.BlockSpec((1, TM, width), lambda b, h, c: (b, c, h))
    kb_spec = pl.BlockSpec((1, TM, width), lambda b, h, c: (b, bwd_tile(c), h))
    tf_spec = pl.BlockSpec((1, 1, width, TM), lambda b, h, c: (b, c, h, 0))
    tb_spec = pl.BlockSpec((1, 1, width, TM), lambda b, h, c: (b, bwd_tile(c), h, 0))
    gf_spec = pl.BlockSpec((1, 1, SCAN_HEADS, GATE_ROWS, TM), lambda b, h, c: (b, 0, h, 0, c))
    gb_spec = pl.BlockSpec((1, 1, SCAN_HEADS, GATE_ROWS, TM), lambda b, h, c: (b, 1, h, 0, bwd_tile(c)))
    out = jax.ShapeDtypeStruct((n_batch, rows, D_INNER), BF16)
    return pl.pallas_call(
        _scan_kernel,
        out_shape=(out, out),
        grid=(n_batch, M_HEADS // SCAN_HEADS, n_t),
        in_specs=[tf_spec, kf_spec, tf_spec, gf_spec, tb_spec, kb_spec, tb_spec, gb_spec],
        out_specs=(kf_spec, kb_spec),
        scratch_shapes=[pltpu.VMEM((2, SCAN_HEADS, STATE_ROWS, M_HEAD_DIM), F32),
                        pltpu.VMEM((2, SCAN_HEADS, STATE_ROWS, M_HEAD_DIM), BF16),
                        pltpu.VMEM((2, SCAN_HEADS, 8, 128), F32)],
        compiler_params=_params("parallel", "parallel", "arbitrary"),
        name="mlstm_scan",
    )(q_t, k, v_t, gates, q_t, k, v_t, gates)


def _post_norm(x, y, gate, pg, pb):
    return _ln(ALPHA * x + gate * y) * pg + pb


def _readout_kernel(n_stream, hf_ref, hb_ref, xc_ref, z_ref, *refs):
    mod_ref, gn_ref, sk_ref, w_ref, pg_ref, pb_ref, o_ref = refs[n_stream:]
    parts = []
    for h in range(M_HEADS):
        sl = slice(h * M_HEAD_DIM, (h + 1) * M_HEAD_DIM)
        hn = _ln(hf_ref[0, :, sl].astype(F32) + hb_ref[0, :, sl].astype(F32))
        a = hn * gn_ref[:, sl] + sk_ref[:, sl] * xc_ref[0, :, sl].astype(F32)
        parts.append((a * _silu(z_ref[0, :, sl].astype(F32))).astype(BF16))
    y = _dot(jnp.concatenate(parts, axis=1), w_ref[...])
    o_ref[0] = _post_norm(_stream_tile(refs[:n_stream]), y, _mod_chunk(mod_ref[0], 2), pg_ref[...], pb_ref[...])


def _mlstm_readout(hf, hb, xc, z, stream, mod_l, gn_w, skip, w_out, pg, pb):
    n_batch, rows = _stream_shape(stream)
    specs, args = _stream_specs(stream)
    return pl.pallas_call(
        functools.partial(_readout_kernel, len(args)),
        out_shape=jax.ShapeDtypeStruct((n_batch, rows, D_MODEL), F32),
        grid=(n_batch, rows // TM),
        in_specs=[_tile_spec(D_INNER), _tile_spec(D_INNER), _tile_spec(D_INNER), _tile_spec(D_INNER)] + specs
        + [_mod_spec(n_batch, True),
           _const_spec((1, D_INNER)), _const_spec((1, D_INNER)), _const_spec((D_INNER, D_MODEL)),
           _const_spec((1, D_MODEL)), _const_spec((1, D_MODEL))],
        out_specs=_tile_spec(D_MODEL),
        compiler_params=_params("parallel", "parallel"),
        name="mlstm_readout",
    )(hf, hb, xc, z, *args, mod_l, gn_w, skip, w_out, pg, pb)


def _fourier_out_kernel(ctx_tile, *refs):
    if ctx_tile:
        mx_ref, mc_ref, x_ref, mod_ref, w_ref, pg_ref, pb_ref, o_ref = refs
        a = jnp.where(pl.program_id(1) == 0, mc_ref[0], mx_ref[0])
    else:
        mx_ref, x_ref, mod_ref, w_ref, pg_ref, pb_ref, o_ref = refs
        a = mx_ref[0]
    y = _dot(a.astype(BF16), w_ref[...])
    o_ref[0] = _post_norm(x_ref[0], y, _mod_chunk(mod_ref[0], 2), pg_ref[...], pb_ref[...])


def _fourier_out(mixed_x, mixed_c, xs, mod_l, w_out, pg, pb):
    n_batch = xs.shape[0]
    ctx_tile = mixed_c is not None
    n_t = mixed_x.shape[1] // TM + (1 if ctx_tile else 0)
    in_specs = [pl.BlockSpec((1, TM, D_MODEL), lambda b, t: (b, jnp.maximum(t - 1, 0) if ctx_tile else t, 0))]
    args = [mixed_x]
    if ctx_tile:
        in_specs.append(pl.BlockSpec((1, CTX_LEN, D_MODEL), lambda b, t: (b, 0, 0)))
        args.append(mixed_c)
    in_specs += [_tile_spec(D_MODEL, 0 if ctx_tile else CTX_LEN // TM), _mod_spec(n_batch, ctx_tile),
                 _const_spec((D_MODEL, D_MODEL)), _const_spec((1, D_MODEL)), _const_spec((1, D_MODEL))]
    return pl.pallas_call(
        functools.partial(_fourier_out_kernel, ctx_tile),
        out_shape=jax.ShapeDtypeStruct((n_batch, n_t * TM, D_MODEL), F32),
        grid=(n_batch, n_t),
        in_specs=in_specs,
        out_specs=_tile_spec(D_MODEL),
        compiler_params=_params("parallel", "parallel"),
        name="fourier_out",
    )(*args, xs, mod_l, w_out, pg, pb)


def _tile_mod(mod_ref, ctx_row, r):
    m = mod_ref[pl.program_id(0)]
    if ctx_row is not None and r == 0:
        m = jnp.where(pl.program_id(1) == 0, mod_ref[ctx_row], m)
    return m


def _tiles_per_step(rows):
    n_t = rows // TM
    return 3 if n_t % 3 == 0 else 2


def _mlp_kernel(n_sub, ctx_row, x_ref, mod_ref, w1_ref, w2_ref, pg_ref, pb_ref, o_ref):
    for r in range(n_sub):
        rs = slice(r * TM, (r + 1) * TM)
        x = x_ref[0, rs, :]
        m = _tile_mod(mod_ref, ctx_row, r)
        h = (_ln(x) * (1.0 + _mod_chunk(m, 4)) + _mod_chunk(m, 3)).astype(BF16)
        acc = jnp.zeros((TM, D_MODEL), F32)
        for j in range(D_FF // D_MODEL):
            sl = slice(j * D_MODEL, (j + 1) * D_MODEL)
            u = jnp.square(jnp.maximum(_dot(h, w1_ref[:, sl]), 0.0)).astype(BF16)
            acc = acc + _dot(u, w2_ref[sl, :])
        o_ref[0, rs, :] = _post_norm(x, acc, _mod_chunk(m, 5), pg_ref[...], pb_ref[...])


def _mlp(xs, mod_l, w1, w2, pg, pb, ctx_tile):
    n_batch, rows, _ = xs.shape
    n_sub = _tiles_per_step(rows)
    blk = pl.BlockSpec((1, n_sub * TM, D_MODEL), lambda b, t: (b, t, 0))
    return pl.pallas_call(
        functools.partial(_mlp_kernel, n_sub, n_batch if ctx_tile else None),
        out_shape=jax.ShapeDtypeStruct((n_batch, rows, D_MODEL), F32),
        grid=(n_batch, rows // (n_sub * TM)),
        in_specs=[blk, _const_spec((MOD_ROWS, 1, 6 * D_MODEL)),
                  _const_spec((D_MODEL, D_FF)), _const_spec((D_FF, D_MODEL)),
                  _const_spec((1, D_MODEL)), _const_spec((1, D_MODEL))],
        out_specs=blk,
        compiler_params=_params("parallel", "parallel"),
        name="mlp",
    )(xs, mod_l, w1, w2, pg, pb)


def _dft_tables():
    two_pi = 2.0 * np.pi
    c = np.arange(F_GROUP_DIM)
    ang = two_pi * ((c[:, None] * c[None, :]) % F_GROUP_DIM) / F_GROUP_DIM
    s_ch = 1.0 / math.sqrt(F_GROUP_DIM)
    f_ch = np.concatenate([np.cos(ang) * s_ch, -np.sin(ang) * s_ch], axis=1)

    seq = FFT_T1 * FFT_T2
    t1 = np.arange(FFT_T1)
    t2 = np.arange(FFT_T2)
    num = (t1[None, :, None] * t1[None, None, :] * FFT_T2 + t2[:, None, None] * t1[None, :, None]) % seq
    ang = two_pi * num / seq
    s1 = 1.0 / math.sqrt(FFT_T1)
    gr, gi = np.cos(ang) * s1, -np.sin(ang) * s1
    g1 = np.concatenate([np.concatenate([gr, -gi], axis=2), np.concatenate([gi, gr], axis=2)], axis=1)

    ang = two_pi * ((t2[:, None] * t2[None, :]) % FFT_T2) / FFT_T2
    s2 = 1.0 / math.sqrt(FFT_T2)
    f2 = np.concatenate([np.cos(ang) * s2, np.sin(ang) * s2], axis=1)

    p = np.arange(CTX_LEN)
    ang = two_pi * ((p[:, None] * p[None, :]) % CTX_LEN) / CTX_LEN
    sc = 1.0 / math.sqrt(CTX_LEN)
    f_ctx = np.concatenate([np.cos(ang) * sc, np.sin(ang) * sc], axis=1)
    return tuple(jnp.asarray(a, dtype=F32).astype(BF16) for a in (f_ch, g1, f2, f_ctx))


def _cdft_kernel(ctx_tile, *refs):
    if ctx_tile:
        x_ref, mod_ref, f_ref, zrx_ref, zix_ref, zrc_ref, zic_ref = refs
    else:
        x_ref, mod_ref, f_ref, zrx_ref, zix_ref = refs
    m = mod_ref[0]
    h = (_ln(x_ref[0]) * (1.0 + _mod_chunk(m, 1)) + _mod_chunk(m, 0)).astype(BF16)
    f = f_ref[...]
    zr, zi = [], []
    for g in range(F_GROUPS):
        z = _dot(h[:, g * F_GROUP_DIM:(g + 1) * F_GROUP_DIM], f).astype(BF16)
        zr.append(z[:, :F_GROUP_DIM])
        zi.append(z[:, F_GROUP_DIM:])
    zr = jnp.concatenate(zr, axis=1)
    zi = jnp.concatenate(zi, axis=1)
    if ctx_tile:
        t = pl.program_id(1)

        @pl.when(t == 0)
        def _():
            zrc_ref[0] = zr
            zic_ref[0] = zi

        @pl.when(t > 0)
        def _():
            zrx_ref[0] = zr
            zix_ref[0] = zi
    else:
        zrx_ref[0] = zr
        zix_ref[0] = zi


def _channel_dft(xs, mod_l, f_ch, ctx_tile):
    n_batch, rows, _ = xs.shape
    seq = rows - CTX_LEN
    off = CTX_LEN // TM
    zx = jax.ShapeDtypeStruct((n_batch, seq, D_MODEL), BF16)
    zc = jax.ShapeDtypeStruct((n_batch, CTX_LEN, D_MODEL), BF16)
    if ctx_tile:
        n_t = rows // TM
        x_out = pl.BlockSpec((1, TM, D_MODEL), lambda b, t: (b, jnp.maximum(t - off, 0), 0))
        c_out = pl.BlockSpec((1, CTX_LEN, D_MODEL), lambda b, t: (b, 0, 0))
        out_shape, out_specs, x_in = (zx, zx, zc, zc), (x_out, x_out, c_out, c_out), _tile_spec(D_MODEL)
    else:
        n_t = seq // TM
        out_shape, out_specs, x_in = (zx, zx), (_tile_spec(D_MODEL), _tile_spec(D_MODEL)), _tile_spec(D_MODEL, off)
    return pl.pallas_call(
        functools.partial(_cdft_kernel, ctx_tile),
        out_shape=out_shape,
        grid=(n_batch, n_t),
        in_specs=[x_in, _mod_spec(n_batch, ctx_tile), _const_spec((F_GROUP_DIM, 2 * F_GROUP_DIM))],
        out_specs=out_specs,
        compiler_params=_params("parallel", "arbitrary"),
        name="fourier_channel_dft",
    )(xs, mod_l, f_ch)


def _seq_dft_kernel(zr_ref, zi_ref, g1_ref, f2_ref, o_ref, sr_ref, si_ref):
    sr_ref[...] = zr_ref[0].astype(F32)
    si_ref[...] = zi_ref[0].astype(F32)

    def stage1(t2, carry):
        rows = pl.ds(t2, FFT_T1, stride=FFT_T2)
        z = jnp.concatenate([sr_ref[rows, :], si_ref[rows, :]], axis=0).astype(BF16)
        v = _dot(g1_ref[t2], z)
        sr_ref[rows, :] = v[:FFT_T1]
        si_ref[rows, :] = v[FFT_T1:]
        return carry

    lax.fori_loop(0, FFT_T2, stage1, 0, unroll=DFT_UNROLL)

    def stage2(t1, carry):
        rows = pl.ds(pl.multiple_of(t1 * FFT_T2, FFT_T2), FFT_T2)
        v = jnp.concatenate([sr_ref[rows, :], si_ref[rows, :]], axis=0).astype(BF16)
        o_ref[0, pl.ds(t1, FFT_T2, stride=FFT_T1), :] = _dot(f2_ref[...], v)
        return carry

    lax.fori_loop(0, FFT_T1, stage2, 0, unroll=DFT_UNROLL)


def _seq_dft(zr, zi, g1, f2):
    n_batch, seq, _ = zr.shape
    blk = pl.BlockSpec((1, seq, FFT_CB), lambda b, j: (b, 0, j))
    return pl.pallas_call(
        _seq_dft_kernel,
        out_shape=jax.ShapeDtypeStruct((n_batch, seq, D_MODEL), F32),
        grid=(n_batch, D_MODEL // FFT_CB),
        in_specs=[blk, blk, _const_spec((FFT_T2, 2 * FFT_T1, 2 * FFT_T1)), _const_spec((FFT_T2, 2 * FFT_T2))],
        out_specs=blk,
        scratch_shapes=[pltpu.VMEM((seq, FFT_CB), F32), pltpu.VMEM((seq, FFT_CB), F32)],
        compiler_params=_params("parallel", "parallel"),
        name="fourier_seq_dft",
    )(zr, zi, g1, f2)


def _ctx_dft_kernel(zr_ref, zi_ref, f_ref, o_ref):
    o_ref[0] = _dot(f_ref[...], jnp.concatenate([zr_ref[0], zi_ref[0]], axis=0))


def _ctx_dft(zr, zi, f_ctx):
    n_batch = zr.shape[0]
    blk = pl.BlockSpec((1, CTX_LEN, D_MODEL), lambda b: (b, 0, 0))
    return pl.pallas_call(
        _ctx_dft_kernel,
        out_shape=jax.ShapeDtypeStruct((n_batch, CTX_LEN, D_MODEL), F32),
        grid=(n_batch,),
        in_specs=[blk, blk, _const_spec((CTX_LEN, 2 * CTX_LEN))],
        out_specs=blk,
        compiler_params=_params("parallel"),
        name="fourier_ctx_dft",
    )(zr, zi, f_ctx)


def _mlstm_layer(xs, mod_l, w_in, w_conv, b_conv, w_q, w_k, w_v, w_gate, b_gate, gn_w, skip, w_out, pg, pb):
    xm, z = _in_proj(xs, mod_l, w_in.astype(BF16))
    wg = w_gate.reshape(3, M_HEADS, M_HEAD_DIM, N_GATES).transpose(1, 0, 3, 2)
    w_qt = jnp.swapaxes(w_q, 1, 2).astype(BF16)
    w_vt = jnp.swapaxes(w_v, 1, 2).astype(BF16)
    xc, q_t, k, v_t, pre_t = _conv_qkv(
        xm, w_conv.reshape(CONV_K * CONV_K, D_INNER), b_conv.reshape(1, D_INNER),
        w_qt, w_k.astype(BF16), w_vt, wg.astype(BF16), b_gate.reshape(N_GATES, 1))
    hf, hb = _mlstm_scan(q_t, k, v_t, _gate_scans(pre_t))
    return _mlstm_readout(hf, hb, xc, z, xs, mod_l, gn_w.reshape(1, D_INNER), skip.reshape(1, D_INNER),
                          w_out.astype(BF16), pg, pb)


def _fourier_layer(xs, mod_l, w_out, pg, pb, need_ctx, tables):
    f_ch, g1, f2, f_ctx = tables
    z = _channel_dft(xs, mod_l, f_ch, need_ctx)
    mixed_x = _seq_dft(z[0], z[1], g1, f2)
    mixed_c = _ctx_dft(z[2], z[3], f_ctx) if need_ctx else None
    return _fourier_out(mixed_x, mixed_c, xs, mod_l, w_out.astype(BF16), pg, pb)


def kernel(x, c, ctx, c_ctx, ada_w, ada_b, post_g, post_b, m_w_in, m_w_conv, m_b_conv, m_w_q, m_w_k, m_w_v, m_w_gate, m_b_gate, m_gn_w, m_skip, m_w_out, f_w_out, mlp_w1, mlp_w2):
    n_batch = x.shape[0]
    assert n_batch < MOD_ROWS and x.shape[1] == FFT_T1 * FFT_T2 and ctx.shape[1] == CTX_LEN == TM
    xs = (ctx, x)
    s_in = jnp.concatenate([c, c_ctx[None, :], jnp.zeros((MOD_ROWS - n_batch - 1, D_MODEL), F32)], axis=0)
    mod = _ada_table(s_in, ada_w, ada_b)
    tables = _dft_tables()
    for i in range(DEPTH):
        is_mlstm = i % N_MIXERS == 0
        j = i // N_MIXERS
        need_ctx = i < DEPTH - 1
        mod_l = mod[i].reshape(MOD_ROWS, 1, 6 * D_MODEL)
        pg = post_g[i].reshape(2, 1, D_MODEL)
        pb = post_b[i].reshape(2, 1, D_MODEL)
        if is_mlstm:
            xs = _mlstm_layer(xs, mod_l, m_w_in[j], m_w_conv[j], m_b_conv[j], m_w_q[j], m_w_k[j], m_w_v[j],
                              m_w_gate[j], m_b_gate[j], m_gn_w[j], m_skip[j], m_w_out[j], pg[0], pb[0])
            if not need_ctx:
                xs = xs[:, CTX_LEN:]
        else:
            xs = _fourier_layer(xs, mod_l, f_w_out[j], pg[0], pb[0], need_ctx, tables)
        xs = _mlp(xs, mod_l, mlp_w1[i].astype(BF16), mlp_w2[i].astype(BF16), pg[1], pb[1], need_ctx)
    return xs
```

```python
import functools
import math

import numpy as np
import jax
import jax.numpy as jnp
from jax import lax
from jax.experimental import pallas as pl
from jax.experimental.pallas import tpu as pltpu

D_MODEL = 1024
DEPTH = 4
GRID_W = 64
CTX_LEN = 256
N_MIXERS = 2
D_INNER = 2 * D_MODEL
M_HEADS = 4
M_HEAD_DIM = D_INNER // M_HEADS
CONV_K = 3
F_GROUPS = 4
F_GROUP_DIM = D_MODEL // F_GROUPS
D_FF = 4 * D_MODEL
ALPHA = float((2 * DEPTH) ** 0.25)
LN_EPS = 1e-5

TM = 256
N_GATES = 4 * M_HEADS
GATE_ROWS = 8
MOD_ROWS = 8
FFT_R = 8
FFT_C = 128
FFT_J = 16
FFT_CB = 256
LANES = 128
GATE_CHUNKS = 3
N_ROWS = 16
STATE_ROWS = N_ROWS + M_HEAD_DIM
MXU_WIDTH = 256
SCAN_HEADS = 2
DFT_UNROLL = 8
VMEM_LIMIT = 52 * 1024 * 1024

F32 = jnp.float32
BF16 = jnp.bfloat16


def _params(*sem):
    return pltpu.CompilerParams(dimension_semantics=sem, vmem_limit_bytes=VMEM_LIMIT)


def _ln(x):
    mu = jnp.mean(x, axis=-1, keepdims=True)
    xc = x - mu
    var = jnp.mean(xc * xc, axis=-1, keepdims=True)
    return xc * lax.rsqrt(var + LN_EPS)


def _silu(x):
    return x / (1.0 + jnp.exp(-x))


def _dot(a, b):
    return jnp.dot(a, b, preferred_element_type=F32)


def _dot_nt(a, b):
    return lax.dot_general(a, b, (((1,), (1,)), ((), ())), preferred_element_type=F32)


def _mod_chunk(m, j):
    return m[:, j * D_MODEL:(j + 1) * D_MODEL]


def _const_spec(shape):
    nd = len(shape)
    return pl.BlockSpec(shape, lambda *_: (0,) * nd, pipeline_mode=pl.Buffered(1))


def _mod_spec(n_batch, ctx_tile):
    if ctx_tile:
        return pl.BlockSpec((1, 1, 6 * D_MODEL), lambda b, t: (jnp.where(t == 0, n_batch, b), 0, 0))
    return pl.BlockSpec((1, 1, 6 * D_MODEL), lambda b, t: (b, 0, 0))


def _tile_spec(width, off=0):
    return pl.BlockSpec((1, TM, width), lambda b, t: (b, t + off, 0))


def _stream_specs(stream):
    if isinstance(stream, tuple):
        return [pl.BlockSpec((1, CTX_LEN, D_MODEL), lambda b, t: (b, 0, 0)),
                pl.BlockSpec((1, TM, D_MODEL), lambda b, t: (b, jnp.maximum(t - CTX_LEN // TM, 0), 0))], list(stream)
    return [_tile_spec(D_MODEL)], [stream]


def _stream_shape(stream):
    if isinstance(stream, tuple):
        return stream[1].shape[0], stream[0].shape[1] + stream[1].shape[1]
    return stream.shape[0], stream.shape[1]


def _stream_tile(refs):
    if len(refs) == 2:
        return jnp.where(pl.program_id(1) == 0, refs[0][0], refs[1][0])
    return refs[0][0]


def _ada_kernel(s_ref, w_ref, b_ref, o_ref):
    s = _silu(s_ref[...])
    o_ref[0] = jnp.dot(s, w_ref[0], preferred_element_type=F32, precision=lax.Precision.HIGHEST) + b_ref[0]


def _ada_table(s_in, ada_w, ada_b):
    n_chunks = 6
    return pl.pallas_call(
        _ada_kernel,
        out_shape=jax.ShapeDtypeStruct((DEPTH, MOD_ROWS, 6 * D_MODEL), F32),
        grid=(DEPTH, n_chunks),
        in_specs=[pl.BlockSpec((MOD_ROWS, D_MODEL), lambda l, j: (0, 0)),
                  pl.BlockSpec((1, D_MODEL, D_MODEL), lambda l, j: (l, 0, j)),
                  pl.BlockSpec((1, 1, D_MODEL), lambda l, j: (l, 0, j))],
        out_specs=pl.BlockSpec((1, MOD_ROWS, D_MODEL), lambda l, j: (l, 0, j)),
        compiler_params=_params("parallel", "parallel"),
        name="ada_table",
    )(s_in, ada_w, ada_b.reshape(DEPTH, 1, 6 * D_MODEL))


def _in_kernel(n_stream, n_sub, ctx_row, *refs):
    mod_ref, w_ref, xm_ref, z_ref = refs[n_stream:]
    for r in range(n_sub):
        rs = slice(r * TM, (r + 1) * TM)
        x = _stream_tile(refs[:n_stream]) if n_stream == 2 else refs[0][0, rs, :]
        m = _tile_mod(mod_ref, ctx_row, r)
        h = (_ln(x) * (1.0 + _mod_chunk(m, 1)) + _mod_chunk(m, 0)).astype(BF16)
        for j in range(2):
            sl = slice(j * D_MODEL, (j + 1) * D_MODEL)
            xm_ref[0, rs, sl] = _dot(h, w_ref[:, sl]).astype(BF16)
            z_ref[0, rs, sl] = _dot(h, w_ref[:, D_INNER + j * D_MODEL:D_INNER + (j + 1) * D_MODEL]).astype(BF16)


def _in_proj(stream, mod_l, w_in):
    n_batch, rows = _stream_shape(stream)
    if isinstance(stream, tuple):
        n_sub = 1
        specs, args = _stream_specs(stream)
    else:
        n_sub = _tiles_per_step(rows)
        specs, args = [pl.BlockSpec((1, n_sub * TM, D_MODEL), lambda b, t: (b, t, 0))], [stream]
    out = jax.ShapeDtypeStruct((n_batch, rows, D_INNER), BF16)
    out_spec = pl.BlockSpec((1, n_sub * TM, D_INNER), lambda b, t: (b, t, 0))
    return pl.pallas_call(
        functools.partial(_in_kernel, len(args), n_sub, n_batch),
        out_shape=(out, out),
        grid=(n_batch, rows // (n_sub * TM)),
        in_specs=specs + [_const_spec((MOD_ROWS, 1, 6 * D_MODEL)), _const_spec((D_MODEL, 2 * D_INNER))],
        out_specs=(out_spec, out_spec),
        compiler_params=_params("parallel", "parallel"),
        name="mlstm_in_proj",
    )(*args, mod_l, w_in)


def _conv_qkv_kernel(xm_ref, prev_ref, next_ref, wc_ref, bc_ref, wq_ref, wk_ref, wv_ref, wg_ref, bg_ref,
                     xc_ref, q_ref, k_ref, v_ref, pre_ref):
    t = pl.program_id(1)
    n_t = pl.num_programs(1)
    is_ctx = t == 0
    row = lax.broadcasted_iota(jnp.int32, (TM, 1), 0)
    period = jnp.where(is_ctx, CTX_LEN - 1, GRID_W - 1)
    pos = row & period
    has_left = pos != 0
    has_right = pos != period
    prev_ok = t >= 2
    next_ok = jnp.logical_and(t >= 1, t <= n_t - 2)
    row_w = jnp.where(is_ctx, 0.0, 1.0)
    pre_t = jnp.zeros((N_GATES, TM), F32) + bg_ref[...]
    for h in range(M_HEADS):
        sl = slice(h * M_HEAD_DIM, (h + 1) * M_HEAD_DIM)
        xm = xm_ref[0, :, sl]
        e = jnp.concatenate([jnp.where(prev_ok, prev_ref[0, :, sl], 0).astype(F32),
                             xm.astype(F32),
                             jnp.where(next_ok, next_ref[0, :, sl], 0).astype(F32)], axis=0)
        p = []
        for dc in range(CONV_K):
            acc = None
            for dr in range(CONV_K):
                w = wc_ref[CONV_K * dr + dc:CONV_K * dr + dc + 1, sl]
                if dr != CONV_K // 2:
                    w = w * row_w
                term = w * e[dr * GRID_W:dr * GRID_W + TM]
                acc = term if acc is None else acc + term
            p.append(acc)
        y = (p[1] + bc_ref[:, sl]
             + jnp.where(has_left, pltpu.roll(p[0], 1, axis=0), 0.0)
             + jnp.where(has_right, pltpu.roll(p[2], TM - 1, axis=0), 0.0))
        xc = _silu(y).astype(BF16)
        xc_ref[0, :, sl] = xc
        q_t = _dot_nt(wq_ref[h], xc).astype(BF16)
        k = (_dot(xc, wk_ref[h]) * (M_HEAD_DIM ** -0.5)).astype(BF16)
        v_t = _dot_nt(wv_ref[h], xm).astype(BF16)
        q_ref[0, 0, sl, :] = q_t
        k_ref[0, :, sl] = k
        v_ref[0, 0, sl, :] = v_t
        pre_t = pre_t + _dot(wg_ref[h, 0], q_t) + _dot_nt(wg_ref[h, 1], k) + _dot(wg_ref[h, 2], v_t)
    pre_ref[0] = pre_t


def _conv_qkv(xm, w_conv, b_conv, w_q, w_k, w_v, w_g, b_g):
    n_batch, rows, _ = xm.shape
    n_halo = rows // GRID_W
    per = TM // GRID_W
    wide = jax.ShapeDtypeStruct((n_batch, rows, D_INNER), BF16)
    wide_t = jax.ShapeDtypeStruct((n_batch, rows // TM, D_INNER, TM), BF16)
    col_spec = pl.BlockSpec((1, 1, D_INNER, TM), lambda b, t: (b, t, 0, 0))
    return pl.pallas_call(
        _conv_qkv_kernel,
        out_shape=(wide, wide_t, wide, wide_t, jax.ShapeDtypeStruct((n_batch, N_GATES, rows), F32)),
        grid=(n_batch, rows // TM),
        in_specs=[_tile_spec(D_INNER),
                  pl.BlockSpec((1, GRID_W, D_INNER), lambda b, t: (b, jnp.maximum(per * t - 1, 0), 0)),
                  pl.BlockSpec((1, GRID_W, D_INNER), lambda b, t: (b, jnp.minimum(per * t + per, n_halo - 1), 0)),
                  _const_spec((CONV_K * CONV_K, D_INNER)),
                  _const_spec((1, D_INNER)),
                  _const_spec((M_HEADS, M_HEAD_DIM, M_HEAD_DIM)),
                  _const_spec((M_HEADS, M_HEAD_DIM, M_HEAD_DIM)),
                  _const_spec((M_HEADS, M_HEAD_DIM, M_HEAD_DIM)),
                  _const_spec((M_HEADS, 3, N_GATES, M_HEAD_DIM)),
                  _const_spec((N_GATES, 1))],
        out_specs=(_tile_spec(D_INNER), col_spec, _tile_spec(D_INNER), col_spec,
                   pl.BlockSpec((1, N_GATES, TM), lambda b, t: (b, 0, t))),
        compiler_params=_params("parallel", "parallel"),
        name="mlstm_conv_qkv",
    )(xm, xm, xm, w_conv, b_conv, w_q, w_k, w_v, w_g, b_g)


def _scan_lanes(x, combine, fill, reverse):
    n = x.shape[-1]
    lane = lax.broadcasted_iota(jnp.int32, x.shape, 1)
    s = 1
    while s < n:
        if reverse:
            sh = jnp.where(lane < n - s, pltpu.roll(x, n - s, axis=1), fill)
        else:
            sh = jnp.where(lane >= s, pltpu.roll(x, s, axis=1), fill)
        x = combine(x, sh)
        s *= 2
    return x


def _gate_kernel(pre_ref, o_ref):
    zeros = jnp.zeros((GATE_ROWS - 3, TM), F32)
    for c in range(GATE_CHUNKS):
        lanes = slice(c * TM, (c + 1) * TM)
        for d in range(2):
            blk = pre_ref[0, 2 * M_HEADS * d:2 * M_HEADS * (d + 1), lanes]
            log_f = jnp.minimum(blk, 0.0) - jnp.log1p(jnp.exp(-jnp.abs(blk)))
            b = pltpu.roll(_scan_lanes(log_f, jnp.add, 0.0, reverse=d == 1), M_HEADS, axis=0)
            g = blk - b
            big = _scan_lanes(g, jnp.maximum, -jnp.inf, reverse=d == 1)
            for h in range(M_HEADS):
                o_ref[0, d, h, :, lanes] = jnp.concatenate([b[h:h + 1], g[h:h + 1], big[h:h + 1], zeros], axis=0)


def _gate_scans(pre_t):
    n_batch, _, rows = pre_t.shape
    width = GATE_CHUNKS * TM
    return pl.pallas_call(
        _gate_kernel,
        out_shape=jax.ShapeDtypeStruct((n_batch, 2, M_HEADS, GATE_ROWS, rows), F32),
        grid=(n_batch, rows // width),
        in_specs=[pl.BlockSpec((1, N_GATES, width), lambda b, t: (b, 0, t))],
        out_specs=pl.BlockSpec((1, 2, M_HEADS, GATE_ROWS, width), lambda b, t: (b, 0, 0, 0, t)),
        compiler_params=_params("parallel", "parallel"),
        name="mlstm_gate_scans",
    )(pre_t)


def _scan_chunk(d, hh, q_ref, k_ref, v_ref, gate_ref, h_ref, cn_ref, cnb_ref, m_ref):
    sl = slice(hh * M_HEAD_DIM, (hh + 1) * M_HEAD_DIM)
    q_t = q_ref[0, 0, sl, :]
    k = k_ref[0, :, sl]
    v_t = v_ref[0, 0, sl, :]
    gates = gate_ref[0, 0, hh]
    b_row, g_row, big_row = gates[0:1], gates[1:2], gates[2:3]
    m_prev = m_ref[d, hh, 0:1, 0:1]
    last = TM - 1 if d == 0 else 0
    m_row = jnp.maximum(m_prev, big_row)
    m_last = m_row[:, last:last + 1]
    w_inter = jnp.exp(m_prev - m_row)
    den_floor = jnp.exp(-(b_row + m_row))
    w_key = jnp.exp(g_row - m_last).astype(BF16)
    g_col = jnp.transpose(gates)[:, 1:2]
    jj = lax.broadcasted_iota(jnp.int32, (TM, TM), 0)
    ii = lax.broadcasted_iota(jnp.int32, (TM, TM), 1)
    visible = jj <= ii if d == 0 else jj >= ii
    pad = jnp.zeros((N_ROWS - 1, TM), BF16)

    s_t = _dot(k, q_t) * jnp.where(visible, jnp.exp(g_col - m_row), 0.0)
    inter = _dot(cnb_ref[d, hh], q_t)
    v_one = jnp.concatenate([jnp.ones((1, TM), BF16), pad, v_t], axis=0)
    intra = _dot(v_one, s_t.astype(BF16))
    den = w_inter * inter[0:1] + intra[0:1]
    scale = 1.0 / jnp.maximum(jnp.abs(den), den_floor)
    h_t = (w_inter * scale) * inter[N_ROWS:] + scale * intra[N_ROWS:]
    h_ref[0, :, sl] = jnp.transpose(h_t).astype(BF16)

    decay = jnp.exp(m_prev - m_last)
    vw = jnp.concatenate([w_key, pad, v_t * w_key], axis=0)
    for c0 in range(0, M_HEAD_DIM, MXU_WIDTH):
        cn = decay * cn_ref[d, hh, :, c0:c0 + MXU_WIDTH] + _dot(vw, k[:, c0:c0 + MXU_WIDTH])
        cn_ref[d, hh, :, c0:c0 + MXU_WIDTH] = cn
        cnb_ref[d, hh, :, c0:c0 + MXU_WIDTH] = cn.astype(BF16)
    m_ref[d, hh] = jnp.zeros(m_ref.shape[2:], F32) + (b_row[:, last:last + 1] + m_last)


def _scan_kernel(qf_ref, kf_ref, vf_ref, gf_ref, qb_ref, kb_ref, vb_ref, gb_ref, hf_ref, hb_ref,
                 cn_ref, cnb_ref, m_ref):
    @pl.when(pl.program_id(2) == 0)
    def _():
        cn_ref[...] = jnp.zeros_like(cn_ref)
        cnb_ref[...] = jnp.zeros_like(cnb_ref)
        m_ref[...] = jnp.zeros_like(m_ref)

    for hh in range(SCAN_HEADS):
        _scan_chunk(0, hh, qf_ref, kf_ref, vf_ref, gf_ref, hf_ref, cn_ref, cnb_ref, m_ref)
        _scan_chunk(1, hh, qb_ref, kb_ref, vb_ref, gb_ref, hb_ref, cn_ref, cnb_ref, m_ref)


def _mlstm_scan(q_t, k, v_t, gates):
    n_batch, rows, _ = k.shape
    n_t = rows // TM
    width = SCAN_HEADS * M_HEAD_DIM

    def bwd_tile(c):
        return jnp.where(c == 0, 0, n_t - c)

    kf_spec = pl.BlockSpec((1, TM, width), lambda b, h, c: (b, c, h))
    kb_spec = pl.BlockSpec((1, TM, width), lambda b, h, c: (b, bwd_tile(c), h))
    tf_spec = pl.BlockSpec((1, 1, width, TM), lambda b, h, c: (b, c, h, 0))
    tb_spec = pl.BlockSpec((1, 1, width, TM), lambda b, h, c: (b, bwd_tile(c), h, 0))
    gf_spec = pl.BlockSpec((1, 1, SCAN_HEADS, GATE_ROWS, TM), lambda b, h, c: (b, 0, h, 0, c))
    gb_spec = pl.BlockSpec((1, 1, SCAN_HEADS, GATE_ROWS, TM), lambda b, h, c: (b, 1, h, 0, bwd_tile(c)))
    out = jax.ShapeDtypeStruct((n_batch, rows, D_INNER), BF16)
    return pl.pallas_call(
        _scan_kernel,
        out_shape=(out, out),
        grid=(n_batch, M_HEADS // SCAN_HEADS, n_t),
        in_specs=[tf_spec, kf_spec, tf_spec, gf_spec, tb_spec, kb_spec, tb_spec, gb_spec],
        out_specs=(kf_spec, kb_spec),
        scratch_shapes=[pltpu.VMEM((2, SCAN_HEADS, STATE_ROWS, M_HEAD_DIM), F32),
                        pltpu.VMEM((2, SCAN_HEADS, STATE_ROWS, M_HEAD_DIM), BF16),
                        pltpu.VMEM((2, SCAN_HEADS, 8, 128), F32)],
        compiler_params=_params("parallel", "parallel", "arbitrary"),
        name="mlstm_scan",
    )(q_t, k, v_t, gates, q_t, k, v_t, gates)


def _post_norm(x, y, gate, pg, pb):
    return _ln(ALPHA * x + gate * y) * pg + pb


def _readout_kernel(n_stream, hf_ref, hb_ref, xc_ref, z_ref, *refs):
    mod_ref, gn_ref, sk_ref, w_ref, pg_ref, pb_ref, o_ref = refs[n_stream:]
    parts = []
    for h in range(M_HEADS):
        sl = slice(h * M_HEAD_DIM, (h + 1) * M_HEAD_DIM)
        hn = _ln(hf_ref[0, :, sl].astype(F32) + hb_ref[0, :, sl].astype(F32))
        a = hn * gn_ref[:, sl] + sk_ref[:, sl] * xc_ref[0, :, sl].astype(F32)
        parts.append((a * _silu(z_ref[0, :, sl].astype(F32))).astype(BF16))
    y = _dot(jnp.concatenate(parts, axis=1), w_ref[...])
    o_ref[0] = _post_norm(_stream_tile(refs[:n_stream]), y, _mod_chunk(mod_ref[0], 2), pg_ref[...], pb_ref[...])


def _mlstm_readout(hf, hb, xc, z, stream, mod_l, gn_w, skip, w_out, pg, pb):
    n_batch, rows = _stream_shape(stream)
    specs, args = _stream_specs(stream)
    return pl.pallas_call(
        functools.partial(_readout_kernel, len(args)),
        out_shape=jax.ShapeDtypeStruct((n_batch, rows, D_MODEL), F32),
        grid=(n_batch, rows // TM),
        in_specs=[_tile_spec(D_INNER), _tile_spec(D_INNER), _tile_spec(D_INNER), _tile_spec(D_INNER)] + specs
        + [_mod_spec(n_batch, True),
           _const_spec((1, D_INNER)), _const_spec((1, D_INNER)), _const_spec((D_INNER, D_MODEL)),
           _const_spec((1, D_MODEL)), _const_spec((1, D_MODEL))],
        out_specs=_tile_spec(D_MODEL),
        compiler_params=_params("parallel", "parallel"),
        name="mlstm_readout",
    )(hf, hb, xc, z, *args, mod_l, gn_w, skip, w_out, pg, pb)


def _fourier_out_kernel(ctx_tile, *refs):
    if ctx_tile:
        mx_ref, mc_ref, x_ref, mod_ref, w_ref, pg_ref, pb_ref, o_ref = refs
    else:
        mx_ref, x_ref, mod_ref, w_ref, pg_ref, pb_ref, o_ref = refs
    a = jnp.concatenate([mx_ref[0, j] for j in range(D_MODEL // LANES)], axis=1)
    if ctx_tile:
        a = jnp.where(pl.program_id(1) == 0, mc_ref[0], a)
    y = _dot(a.astype(BF16), w_ref[...])
    o_ref[0] = _post_norm(x_ref[0], y, _mod_chunk(mod_ref[0], 2), pg_ref[...], pb_ref[...])


def _fourier_out(mixed_x, mixed_c, xs, mod_l, w_out, pg, pb):
    n_batch = xs.shape[0]
    ctx_tile = mixed_c is not None
    n_t = mixed_x.shape[2] // TM + (1 if ctx_tile else 0)
    in_specs = [pl.BlockSpec((1, D_MODEL // LANES, TM, LANES),
                             lambda b, t: (b, 0, jnp.maximum(t - 1, 0) if ctx_tile else t, 0))]
    args = [mixed_x]
    if ctx_tile:
        in_specs.append(pl.BlockSpec((1, CTX_LEN, D_MODEL), lambda b, t: (b, 0, 0)))
        args.append(mixed_c)
    in_specs += [_tile_spec(D_MODEL, 0 if ctx_tile else CTX_LEN // TM), _mod_spec(n_batch, ctx_tile),
                 _const_spec((D_MODEL, D_MODEL)), _const_spec((1, D_MODEL)), _const_spec((1, D_MODEL))]
    return pl.pallas_call(
        functools.partial(_fourier_out_kernel, ctx_tile),
        out_shape=jax.ShapeDtypeStruct((n_batch, n_t * TM, D_MODEL), F32),
        grid=(n_batch, n_t),
        in_specs=in_specs,
        out_specs=_tile_spec(D_MODEL),
        compiler_params=_params("parallel", "parallel"),
        name="fourier_out",
    )(*args, xs, mod_l, w_out, pg, pb)


def _tile_mod(mod_ref, ctx_row, r):
    m = mod_ref[pl.program_id(0)]
    if ctx_row is not None and r == 0:
        m = jnp.where(pl.program_id(1) == 0, mod_ref[ctx_row], m)
    return m


def _tiles_per_step(rows):
    n_t = rows // TM
    return 3 if n_t % 3 == 0 else 2


def _mlp_kernel(n_sub, ctx_row, x_ref, mod_ref, w1_ref, w2_ref, pg_ref, pb_ref, o_ref):
    for r in range(n_sub):
        rs = slice(r * TM, (r + 1) * TM)
        x = x_ref[0, rs, :]
        m = _tile_mod(mod_ref, ctx_row, r)
        h = (_ln(x) * (1.0 + _mod_chunk(m, 4)) + _mod_chunk(m, 3)).astype(BF16)
        acc = jnp.zeros((TM, D_MODEL), F32)
        for j in range(D_FF // D_MODEL):
            sl = slice(j * D_MODEL, (j + 1) * D_MODEL)
            u = jnp.square(jnp.maximum(_dot(h, w1_ref[:, sl]), 0.0)).astype(BF16)
            acc = acc + _dot(u, w2_ref[sl, :])
        o_ref[0, rs, :] = _post_norm(x, acc, _mod_chunk(m, 5), pg_ref[...], pb_ref[...])


def _mlp(xs, mod_l, w1, w2, pg, pb, ctx_tile):
    n_batch, rows, _ = xs.shape
    n_sub = _tiles_per_step(rows)
    blk = pl.BlockSpec((1, n_sub * TM, D_MODEL), lambda b, t: (b, t, 0))
    return pl.pallas_call(
        functools.partial(_mlp_kernel, n_sub, n_batch if ctx_tile else None),
        out_shape=jax.ShapeDtypeStruct((n_batch, rows, D_MODEL), F32),
        grid=(n_batch, rows // (n_sub * TM)),
        in_specs=[blk, _const_spec((MOD_ROWS, 1, 6 * D_MODEL)),
                  _const_spec((D_MODEL, D_FF)), _const_spec((D_FF, D_MODEL)),
                  _const_spec((1, D_MODEL)), _const_spec((1, D_MODEL))],
        out_specs=blk,
        compiler_params=_params("parallel", "parallel"),
        name="mlp",
    )(xs, mod_l, w1, w2, pg, pb)


def _dft_tables():
    two_pi = 2.0 * np.pi
    c = np.arange(F_GROUP_DIM)
    ang = two_pi * ((c[:, None] * c[None, :]) % F_GROUP_DIM) / F_GROUP_DIM
    s_ch = 1.0 / math.sqrt(F_GROUP_DIM)
    f_ch = np.concatenate([np.cos(ang) * s_ch, -np.sin(ang) * s_ch], axis=1)

    seq = FFT_R * FFT_R * FFT_C
    r = np.arange(FFT_R)
    eye = np.eye(FFT_J)

    def tile_mix(w):
        wr, wi = np.kron(w.real, eye), np.kron(w.imag, eye)
        return np.block([[wr, -wi], [wi, wr]])

    w8 = np.exp(-2j * np.pi * ((r[:, None] * r[None, :]) % FFT_R) / FFT_R) / math.sqrt(FFT_R)
    k_a = tile_mix(w8)
    k_b = np.stack([tile_mix(w8 * np.exp(-2j * np.pi * (r[None, :] * a1) / (FFT_R * FFT_R)))
                    for a1 in range(FFT_R)])
    c = np.arange(FFT_C)
    t1 = np.arange(FFT_R * FFT_R)
    num = (c[None, :, None] * c[None, None, :] * (FFT_R * FFT_R) + t1[:, None, None] * c[None, None, :]) % seq
    ang = two_pi * num / seq
    f_c = np.concatenate([np.cos(ang), np.sin(ang)], axis=2) / math.sqrt(FFT_C)

    p = np.arange(CTX_LEN)
    ang = two_pi * ((p[:, None] * p[None, :]) % CTX_LEN) / CTX_LEN
    sc = 1.0 / math.sqrt(CTX_LEN)
    f_ctx = np.concatenate([np.cos(ang) * sc, np.sin(ang) * sc], axis=1)
    return tuple(jnp.asarray(a, dtype=F32).astype(BF16) for a in (f_ch, k_a, k_b, f_c, f_ctx))


def _cdft_kernel(ctx_tile, *refs):
    if ctx_tile:
        x_ref, mod_ref, f_ref, zrx_ref, zix_ref, zrc_ref, zic_ref = refs
    else:
        x_ref, mod_ref, f_ref, zrx_ref, zix_ref = refs
    m = mod_ref[0]
    h = (_ln(x_ref[0]) * (1.0 + _mod_chunk(m, 1)) + _mod_chunk(m, 0)).astype(BF16)
    f = f_ref[...]
    zr, zi = [], []
    for g in range(F_GROUPS):
        z = _dot(h[:, g * F_GROUP_DIM:(g + 1) * F_GROUP_DIM], f).astype(BF16)
        zr.append(z[:, :F_GROUP_DIM])
        zi.append(z[:, F_GROUP_DIM:])
    zr = jnp.concatenate(zr, axis=1)
    zi = jnp.concatenate(zi, axis=1)
    if ctx_tile:
        t = pl.program_id(1)

        @pl.when(t == 0)
        def _():
            zrc_ref[0] = zr
            zic_ref[0] = zi

        @pl.when(t > 0)
        def _():
            zrx_ref[0] = zr
            zix_ref[0] = zi
    else:
        zrx_ref[0] = zr
        zix_ref[0] = zi


def _channel_dft(xs, mod_l, f_ch, ctx_tile):
    n_batch, rows, _ = xs.shape
    seq = rows - CTX_LEN
    off = CTX_LEN // TM
    zx = jax.ShapeDtypeStruct((n_batch, seq, D_MODEL), BF16)
    zc = jax.ShapeDtypeStruct((n_batch, CTX_LEN, D_MODEL), BF16)
    if ctx_tile:
        n_t = rows // TM
        x_out = pl.BlockSpec((1, TM, D_MODEL), lambda b, t: (b, jnp.maximum(t - off, 0), 0))
        c_out = pl.BlockSpec((1, CTX_LEN, D_MODEL), lambda b, t: (b, 0, 0))
        out_shape, out_specs, x_in = (zx, zx, zc, zc), (x_out, x_out, c_out, c_out), _tile_spec(D_MODEL)
    else:
        n_t = seq // TM
        out_shape, out_specs, x_in = (zx, zx), (_tile_spec(D_MODEL), _tile_spec(D_MODEL)), _tile_spec(D_MODEL, off)
    return pl.pallas_call(
        functools.partial(_cdft_kernel, ctx_tile),
        out_shape=out_shape,
        grid=(n_batch, n_t),
        in_specs=[x_in, _mod_spec(n_batch, ctx_tile), _const_spec((F_GROUP_DIM, 2 * F_GROUP_DIM))],
        out_specs=out_specs,
        compiler_params=_params("parallel", "arbitrary"),
        name="fourier_channel_dft",
    )(xs, mod_l, f_ch)


def _seq_dft_kernel(zr_ref, zi_ref, ka_ref, kb_ref, fc_ref, o_ref, ar_ref, ai_ref, br_ref, bi_ref):
    a_rows = FFT_R * FFT_C
    n_tiles = FFT_C // FFT_J

    def mix(load, store, base, step, k):
        offs = [base + i * step for i in range(FFT_R)]
        z = jnp.concatenate([load(0, o) for o in offs] + [load(1, o) for o in offs], axis=0)
        u = _dot(k, z).astype(BF16)
        for i, o in enumerate(offs):
            store(0, o, u[i * FFT_J:(i + 1) * FFT_J])
            store(1, o, u[(FFT_R + i) * FFT_J:(FFT_R + i + 1) * FFT_J])

    def load_in(p, o):
        return (zr_ref, zi_ref)[p][0, pl.ds(o, FFT_J), :]

    def load_a(p, o):
        return (ar_ref, ai_ref)[p][pl.ds(o, FFT_J), :]

    def store_a(p, o, val):
        (ar_ref, ai_ref)[p][pl.ds(o, FFT_J), :] = val

    def store_b(p, o, val):
        (br_ref, bi_ref)[p][pl.ds(o, FFT_J), :] = val

    def stage_a(b, carry):
        for ct in range(n_tiles):
            mix(load_in, store_a, pl.multiple_of(b * FFT_C, FFT_C) + ct * FFT_J, a_rows, ka_ref[...])
        return carry

    lax.fori_loop(0, FFT_R, stage_a, 0)

    def stage_b(a1, carry):
        k = kb_ref[a1]
        for ct in range(n_tiles):
            mix(load_a, store_b, pl.multiple_of(a1 * a_rows, a_rows) + ct * FFT_J, FFT_C, k)
        return carry

    lax.fori_loop(0, FFT_R, stage_b, 0)

    def stage_c(t1, carry):
        rows = pl.ds(pl.multiple_of((t1 & (FFT_R - 1)) * a_rows + lax.shift_right_logical(t1, 3) * FFT_C, FFT_C),
                     FFT_C)
        z = jnp.concatenate([br_ref[rows, :], bi_ref[rows, :]], axis=0)
        y = _dot(fc_ref[t1], z)
        for j in range(FFT_CB // LANES):
            o_ref[0, j, pl.ds(t1, FFT_C, stride=FFT_R * FFT_R), :] = y[:, j * LANES:(j + 1) * LANES]
        return carry

    lax.fori_loop(0, FFT_R * FFT_R, stage_c, 0, unroll=DFT_UNROLL)


def _seq_dft(zr, zi, k_a, k_b, f_c):
    n_batch, seq, _ = zr.shape
    blk = pl.BlockSpec((1, seq, FFT_CB), lambda b, j: (b, 0, j), pipeline_mode=pl.Buffered(1))
    return pl.pallas_call(
        _seq_dft_kernel,
        out_shape=jax.ShapeDtypeStruct((n_batch, D_MODEL // LANES, seq, LANES), F32),
        grid=(n_batch, D_MODEL // FFT_CB),
        in_specs=[blk, blk, _const_spec(k_a.shape), _const_spec(k_b.shape), _const_spec(f_c.shape)],
        out_specs=pl.BlockSpec((1, FFT_CB // LANES, seq, LANES), lambda b, j: (b, j, 0, 0)),
        scratch_shapes=[pltpu.VMEM((seq, FFT_CB), BF16)] * 4,
        compiler_params=_params("parallel", "parallel"),
        name="fourier_seq_dft",
    )(zr, zi, k_a, k_b, f_c)


def _ctx_dft_kernel(zr_ref, zi_ref, f_ref, o_ref):
    o_ref[0] = _dot(f_ref[...], jnp.concatenate([zr_ref[0], zi_ref[0]], axis=0))


def _ctx_dft(zr, zi, f_ctx):
    n_batch = zr.shape[0]
    blk = pl.BlockSpec((1, CTX_LEN, D_MODEL), lambda b: (b, 0, 0))
    return pl.pallas_call(
        _ctx_dft_kernel,
        out_shape=jax.ShapeDtypeStruct((n_batch, CTX_LEN, D_MODEL), F32),
        grid=(n_batch,),
        in_specs=[blk, blk, _const_spec((CTX_LEN, 2 * CTX_LEN))],
        out_specs=blk,
        compiler_params=_params("parallel"),
        name="fourier_ctx_dft",
    )(zr, zi, f_ctx)


def _mlstm_layer(xs, mod_l, w_in, w_conv, b_conv, w_q, w_k, w_v, w_gate, b_gate, gn_w, skip, w_out, pg, pb):
    xm, z = _in_proj(xs, mod_l, w_in.astype(BF16))
    wg = w_gate.reshape(3, M_HEADS, M_HEAD_DIM, N_GATES).transpose(1, 0, 3, 2)
    w_qt = jnp.swapaxes(w_q, 1, 2).astype(BF16)
    w_vt = jnp.swapaxes(w_v, 1, 2).astype(BF16)
    xc, q_t, k, v_t, pre_t = _conv_qkv(
        xm, w_conv.reshape(CONV_K * CONV_K, D_INNER), b_conv.reshape(1, D_INNER),
        w_qt, w_k.astype(BF16), w_vt, wg.astype(BF16), b_gate.reshape(N_GATES, 1))
    hf, hb = _mlstm_scan(q_t, k, v_t, _gate_scans(pre_t))
    return _mlstm_readout(hf, hb, xc, z, xs, mod_l, gn_w.reshape(1, D_INNER), skip.reshape(1, D_INNER),
                          w_out.astype(BF16), pg, pb)


def _fourier_layer(xs, mod_l, w_out, pg, pb, need_ctx, tables):
    f_ch, k_a, k_b, f_c, f_ctx = tables
    z = _channel_dft(xs, mod_l, f_ch, need_ctx)
    mixed_x = _seq_dft(z[0], z[1], k_a, k_b, f_c)
    mixed_c = _ctx_dft(z[2], z[3], f_ctx) if need_ctx else None
    return _fourier_out(mixed_x, mixed_c, xs, mod_l, w_out.astype(BF16), pg, pb)


def kernel(x, c, ctx, c_ctx, ada_w, ada_b, post_g, post_b, m_w_in, m_w_conv, m_b_conv, m_w_q, m_w_k, m_w_v, m_w_gate, m_b_gate, m_gn_w, m_skip, m_w_out, f_w_out, mlp_w1, mlp_w2):
    n_batch = x.shape[0]
    assert n_batch < MOD_ROWS and x.shape[1] == FFT_R * FFT_R * FFT_C and ctx.shape[1] == CTX_LEN == TM
    xs = (ctx, x)
    s_in = jnp.concatenate([c, c_ctx[None, :], jnp.zeros((MOD_ROWS - n_batch - 1, D_MODEL), F32)], axis=0)
    mod = _ada_table(s_in, ada_w, ada_b)
    tables = _dft_tables()
    for i in range(DEPTH):
        is_mlstm = i % N_MIXERS == 0
        j = i // N_MIXERS
        need_ctx = i < DEPTH - 1
        mod_l = mod[i].reshape(MOD_ROWS, 1, 6 * D_MODEL)
        pg = post_g[i].reshape(2, 1, D_MODEL)
        pb = post_b[i].reshape(2, 1, D_MODEL)
        if is_mlstm:
            xs = _mlstm_layer(xs, mod_l, m_w_in[j], m_w_conv[j], m_b_conv[j], m_w_q[j], m_w_k[j], m_w_v[j],
                              m_w_gate[j], m_b_gate[j], m_gn_w[j], m_skip[j], m_w_out[j], pg[0], pb[0])
            if not need_ctx:
                xs = xs[:, CTX_LEN:]
        else:
            xs = _fourier_layer(xs, mod_l, f_w_out[j], pg[0], pb[0], need_ctx, tables)
        xs = _mlp(xs, mod_l, mlp_w1[i].astype(BF16), mlp_w2[i].astype(BF16), pg[1], pb[1], need_ctx)
    return xs
```

```python
import functools
import math

import numpy as np
import jax
import jax.numpy as jnp
from jax import lax
from jax.experimental import pallas as pl
from jax.experimental.pallas import tpu as pltpu

D_MODEL = 1024
DEPTH = 4
GRID_W = 64
CTX_LEN = 256
N_MIXERS = 2
D_INNER = 2 * D_MODEL
M_HEADS = 4
M_HEAD_DIM = D_INNER // M_HEADS
CONV_K = 3
F_GROUPS = 4
F_GROUP_DIM = D_MODEL // F_GROUPS
D_FF = 4 * D_MODEL
ALPHA = float((2 * DEPTH) ** 0.25)
LN_EPS = 1e-5

TM = 256
N_GATES = 4 * M_HEADS
GATE_ROWS = 8
MOD_ROWS = 8
FFT_R = 8
FFT_C = 128
FFT_J = 16
FFT_CB = F_GROUP_DIM
LANES = 128
GATE_CHUNKS = 3
N_ROWS = 16
STATE_ROWS = N_ROWS + M_HEAD_DIM
MXU_WIDTH = 256
SCAN_HEADS = 2
DFT_UNROLL = 8
VMEM_LIMIT = 52 * 1024 * 1024

F32 = jnp.float32
BF16 = jnp.bfloat16


def _params(*sem):
    return pltpu.CompilerParams(dimension_semantics=sem, vmem_limit_bytes=VMEM_LIMIT)


def _ln(x):
    mu = jnp.mean(x, axis=-1, keepdims=True)
    xc = x - mu
    var = jnp.mean(xc * xc, axis=-1, keepdims=True)
    return xc * lax.rsqrt(var + LN_EPS)


def _silu(x):
    return x / (1.0 + jnp.exp(-x))


def _dot(a, b):
    return jnp.dot(a, b, preferred_element_type=F32)


def _dot_nt(a, b):
    return lax.dot_general(a, b, (((1,), (1,)), ((), ())), preferred_element_type=F32)


def _mod_chunk(m, j):
    return m[:, j * D_MODEL:(j + 1) * D_MODEL]


def _const_spec(shape):
    nd = len(shape)
    return pl.BlockSpec(shape, lambda *_: (0,) * nd, pipeline_mode=pl.Buffered(1))


def _mod_spec(n_batch, ctx_tile):
    if ctx_tile:
        return pl.BlockSpec((1, 1, 6 * D_MODEL), lambda b, t: (jnp.where(t == 0, n_batch, b), 0, 0))
    return pl.BlockSpec((1, 1, 6 * D_MODEL), lambda b, t: (b, 0, 0))


def _tile_spec(width, off=0):
    return pl.BlockSpec((1, TM, width), lambda b, t: (b, t + off, 0))


def _stream_specs(stream):
    if isinstance(stream, tuple):
        return [pl.BlockSpec((1, CTX_LEN, D_MODEL), lambda b, t: (b, 0, 0)),
                pl.BlockSpec((1, TM, D_MODEL), lambda b, t: (b, jnp.maximum(t - CTX_LEN // TM, 0), 0))], list(stream)
    return [_tile_spec(D_MODEL)], [stream]


def _stream_shape(stream):
    if isinstance(stream, tuple):
        return stream[1].shape[0], stream[0].shape[1] + stream[1].shape[1]
    return stream.shape[0], stream.shape[1]


def _stream_tile(refs):
    if len(refs) == 2:
        return jnp.where(pl.program_id(1) == 0, refs[0][0], refs[1][0])
    return refs[0][0]


def _ada_kernel(s_ref, w_ref, b_ref, o_ref):
    s = _silu(s_ref[...])
    o_ref[0] = jnp.dot(s, w_ref[0], preferred_element_type=F32, precision=lax.Precision.HIGHEST) + b_ref[0]


def _ada_table(s_in, ada_w, ada_b):
    n_chunks = 6
    return pl.pallas_call(
        _ada_kernel,
        out_shape=jax.ShapeDtypeStruct((DEPTH, MOD_ROWS, 6 * D_MODEL), F32),
        grid=(DEPTH, n_chunks),
        in_specs=[pl.BlockSpec((MOD_ROWS, D_MODEL), lambda l, j: (0, 0)),
                  pl.BlockSpec((1, D_MODEL, D_MODEL), lambda l, j: (l, 0, j)),
                  pl.BlockSpec((1, 1, D_MODEL), lambda l, j: (l, 0, j))],
        out_specs=pl.BlockSpec((1, MOD_ROWS, D_MODEL), lambda l, j: (l, 0, j)),
        compiler_params=_params("parallel", "parallel"),
        name="ada_table",
    )(s_in, ada_w, ada_b.reshape(DEPTH, 1, 6 * D_MODEL))


def _in_kernel(n_stream, n_sub, ctx_row, *refs):
    mod_ref, w_ref, xm_ref, z_ref = refs[n_stream:]
    for r in range(n_sub):
        rs = slice(r * TM, (r + 1) * TM)
        x = _stream_tile(refs[:n_stream]) if n_stream == 2 else refs[0][0, rs, :]
        m = _tile_mod(mod_ref, ctx_row, r)
        h = (_ln(x) * (1.0 + _mod_chunk(m, 1)) + _mod_chunk(m, 0)).astype(BF16)
        for j in range(2):
            sl = slice(j * D_MODEL, (j + 1) * D_MODEL)
            xm_ref[0, rs, sl] = _dot(h, w_ref[:, sl]).astype(BF16)
            z_ref[0, rs, sl] = _dot(h, w_ref[:, D_INNER + j * D_MODEL:D_INNER + (j + 1) * D_MODEL]).astype(BF16)


def _in_proj(stream, mod_l, w_in):
    n_batch, rows = _stream_shape(stream)
    if isinstance(stream, tuple):
        n_sub = 1
        specs, args = _stream_specs(stream)
    else:
        n_sub = _tiles_per_step(rows)
        specs, args = [pl.BlockSpec((1, n_sub * TM, D_MODEL), lambda b, t: (b, t, 0))], [stream]
    out = jax.ShapeDtypeStruct((n_batch, rows, D_INNER), BF16)
    out_spec = pl.BlockSpec((1, n_sub * TM, D_INNER), lambda b, t: (b, t, 0))
    return pl.pallas_call(
        functools.partial(_in_kernel, len(args), n_sub, n_batch),
        out_shape=(out, out),
        grid=(n_batch, rows // (n_sub * TM)),
        in_specs=specs + [_const_spec((MOD_ROWS, 1, 6 * D_MODEL)), _const_spec((D_MODEL, 2 * D_INNER))],
        out_specs=(out_spec, out_spec),
        compiler_params=_params("parallel", "parallel"),
        name="mlstm_in_proj",
    )(*args, mod_l, w_in)


def _conv_qkv_kernel(xm_ref, prev_ref, next_ref, wc_ref, bc_ref, wq_ref, wk_ref, wv_ref, wg_ref, bg_ref,
                     xc_ref, q_ref, k_ref, v_ref, pre_ref):
    t = pl.program_id(1)
    n_t = pl.num_programs(1)
    is_ctx = t == 0
    row = lax.broadcasted_iota(jnp.int32, (TM, 1), 0)
    period = jnp.where(is_ctx, CTX_LEN - 1, GRID_W - 1)
    pos = row & period
    has_left = pos != 0
    has_right = pos != period
    prev_ok = t >= 2
    next_ok = jnp.logical_and(t >= 1, t <= n_t - 2)
    row_w = jnp.where(is_ctx, 0.0, 1.0)
    pre_t = jnp.zeros((N_GATES, TM), F32) + bg_ref[...]
    for h in range(M_HEADS):
        sl = slice(h * M_HEAD_DIM, (h + 1) * M_HEAD_DIM)
        xm = xm_ref[0, :, sl]
        e = jnp.concatenate([jnp.where(prev_ok, prev_ref[0, :, sl], 0).astype(F32),
                             xm.astype(F32),
                             jnp.where(next_ok, next_ref[0, :, sl], 0).astype(F32)], axis=0)
        p = []
        for dc in range(CONV_K):
            acc = None
            for dr in range(CONV_K):
                w = wc_ref[CONV_K * dr + dc:CONV_K * dr + dc + 1, sl]
                if dr != CONV_K // 2:
                    w = w * row_w
                term = w * e[dr * GRID_W:dr * GRID_W + TM]
                acc = term if acc is None else acc + term
            p.append(acc)
        y = (p[1] + bc_ref[:, sl]
             + jnp.where(has_left, pltpu.roll(p[0], 1, axis=0), 0.0)
             + jnp.where(has_right, pltpu.roll(p[2], TM - 1, axis=0), 0.0))
        xc = _silu(y).astype(BF16)
        xc_ref[0, :, sl] = xc
        q_t = _dot_nt(wq_ref[h], xc).astype(BF16)
        k = (_dot(xc, wk_ref[h]) * (M_HEAD_DIM ** -0.5)).astype(BF16)
        v_t = _dot_nt(wv_ref[h], xm).astype(BF16)
        q_ref[0, 0, sl, :] = q_t
        k_ref[0, :, sl] = k
        v_ref[0, 0, sl, :] = v_t
        pre_t = pre_t + _dot(wg_ref[h, 0], q_t) + _dot_nt(wg_ref[h, 1], k) + _dot(wg_ref[h, 2], v_t)
    pre_ref[0] = pre_t


def _conv_qkv(xm, w_conv, b_conv, w_q, w_k, w_v, w_g, b_g):
    n_batch, rows, _ = xm.shape
    n_halo = rows // GRID_W
    per = TM // GRID_W
    wide = jax.ShapeDtypeStruct((n_batch, rows, D_INNER), BF16)
    wide_t = jax.ShapeDtypeStruct((n_batch, rows // TM, D_INNER, TM), BF16)
    col_spec = pl.BlockSpec((1, 1, D_INNER, TM), lambda b, t: (b, t, 0, 0))
    return pl.pallas_call(
        _conv_qkv_kernel,
        out_shape=(wide, wide_t, wide, wide_t, jax.ShapeDtypeStruct((n_batch, N_GATES, rows), F32)),
        grid=(n_batch, rows // TM),
        in_specs=[_tile_spec(D_INNER),
                  pl.BlockSpec((1, GRID_W, D_INNER), lambda b, t: (b, jnp.maximum(per * t - 1, 0), 0)),
                  pl.BlockSpec((1, GRID_W, D_INNER), lambda b, t: (b, jnp.minimum(per * t + per, n_halo - 1), 0)),
                  _const_spec((CONV_K * CONV_K, D_INNER)),
                  _const_spec((1, D_INNER)),
                  _const_spec((M_HEADS, M_HEAD_DIM, M_HEAD_DIM)),
                  _const_spec((M_HEADS, M_HEAD_DIM, M_HEAD_DIM)),
                  _const_spec((M_HEADS, M_HEAD_DIM, M_HEAD_DIM)),
                  _const_spec((M_HEADS, 3, N_GATES, M_HEAD_DIM)),
                  _const_spec((N_GATES, 1))],
        out_specs=(_tile_spec(D_INNER), col_spec, _tile_spec(D_INNER), col_spec,
                   pl.BlockSpec((1, N_GATES, TM), lambda b, t: (b, 0, t))),
        compiler_params=_params("parallel", "parallel"),
        name="mlstm_conv_qkv",
    )(xm, xm, xm, w_conv, b_conv, w_q, w_k, w_v, w_g, b_g)


def _scan_lanes(x, combine, fill, reverse):
    n = x.shape[-1]
    lane = lax.broadcasted_iota(jnp.int32, x.shape, 1)
    s = 1
    while s < n:
        if reverse:
            sh = jnp.where(lane < n - s, pltpu.roll(x, n - s, axis=1), fill)
        else:
            sh = jnp.where(lane >= s, pltpu.roll(x, s, axis=1), fill)
        x = combine(x, sh)
        s *= 2
    return x


def _gate_kernel(pre_ref, o_ref):
    zeros = jnp.zeros((GATE_ROWS - 3, TM), F32)
    for c in range(GATE_CHUNKS):
        lanes = slice(c * TM, (c + 1) * TM)
        for d in range(2):
            blk = pre_ref[0, 2 * M_HEADS * d:2 * M_HEADS * (d + 1), lanes]
            log_f = jnp.minimum(blk, 0.0) - jnp.log1p(jnp.exp(-jnp.abs(blk)))
            b = pltpu.roll(_scan_lanes(log_f, jnp.add, 0.0, reverse=d == 1), M_HEADS, axis=0)
            g = blk - b
            big = _scan_lanes(g, jnp.maximum, -jnp.inf, reverse=d == 1)
            for h in range(M_HEADS):
                o_ref[0, d, h, :, lanes] = jnp.concatenate([b[h:h + 1], g[h:h + 1], big[h:h + 1], zeros], axis=0)


def _gate_scans(pre_t):
    n_batch, _, rows = pre_t.shape
    width = GATE_CHUNKS * TM
    return pl.pallas_call(
        _gate_kernel,
        out_shape=jax.ShapeDtypeStruct((n_batch, 2, M_HEADS, GATE_ROWS, rows), F32),
        grid=(n_batch, rows // width),
        in_specs=[pl.BlockSpec((1, N_GATES, width), lambda b, t: (b, 0, t))],
        out_specs=pl.BlockSpec((1, 2, M_HEADS, GATE_ROWS, width), lambda b, t: (b, 0, 0, 0, t)),
        compiler_params=_params("parallel", "parallel"),
        name="mlstm_gate_scans",
    )(pre_t)


def _scan_chunk(d, hh, q_ref, k_ref, v_ref, gate_ref, h_ref, cn_ref, cnb_ref, m_ref):
    sl = slice(hh * M_HEAD_DIM, (hh + 1) * M_HEAD_DIM)
    q_t = q_ref[0, 0, sl, :]
    k = k_ref[0, :, sl]
    v_t = v_ref[0, 0, sl, :]
    gates = gate_ref[0, 0, hh]
    b_row, g_row, big_row = gates[0:1], gates[1:2], gates[2:3]
    m_prev = m_ref[d, hh, 0:1, 0:1]
    last = TM - 1 if d == 0 else 0
    m_row = jnp.maximum(m_prev, big_row)
    m_last = m_row[:, last:last + 1]
    w_inter = jnp.exp(m_prev - m_row)
    den_floor = jnp.exp(-(b_row + m_row))
    w_key = jnp.exp(g_row - m_last).astype(BF16)
    g_col = jnp.transpose(gates)[:, 1:2]
    jj = lax.broadcasted_iota(jnp.int32, (TM, TM), 0)
    ii = lax.broadcasted_iota(jnp.int32, (TM, TM), 1)
    visible = jj <= ii if d == 0 else jj >= ii
    pad = jnp.zeros((N_ROWS - 1, TM), BF16)

    s_t = _dot(k, q_t) * jnp.where(visible, jnp.exp(g_col - m_row), 0.0)
    inter = _dot(cnb_ref[d, hh], q_t)
    v_one = jnp.concatenate([jnp.ones((1, TM), BF16), pad, v_t], axis=0)
    intra = _dot(v_one, s_t.astype(BF16))
    den = w_inter * inter[0:1] + intra[0:1]
    scale = 1.0 / jnp.maximum(jnp.abs(den), den_floor)
    h_t = (w_inter * scale) * inter[N_ROWS:] + scale * intra[N_ROWS:]
    h_ref[0, :, sl] = jnp.transpose(h_t).astype(BF16)

    decay = jnp.exp(m_prev - m_last)
    vw = jnp.concatenate([w_key, pad, v_t * w_key], axis=0)
    for c0 in range(0, M_HEAD_DIM, MXU_WIDTH):
        cn = decay * cn_ref[d, hh, :, c0:c0 + MXU_WIDTH] + _dot(vw, k[:, c0:c0 + MXU_WIDTH])
        cn_ref[d, hh, :, c0:c0 + MXU_WIDTH] = cn
        cnb_ref[d, hh, :, c0:c0 + MXU_WIDTH] = cn.astype(BF16)
    m_ref[d, hh] = jnp.zeros(m_ref.shape[2:], F32) + (b_row[:, last:last + 1] + m_last)


def _scan_kernel(qf_ref, kf_ref, vf_ref, gf_ref, qb_ref, kb_ref, vb_ref, gb_ref, hf_ref, hb_ref,
                 cn_ref, cnb_ref, m_ref):
    @pl.when(pl.program_id(2) == 0)
    def _():
        cn_ref[...] = jnp.zeros_like(cn_ref)
        cnb_ref[...] = jnp.zeros_like(cnb_ref)
        m_ref[...] = jnp.zeros_like(m_ref)

    for hh in range(SCAN_HEADS):
        _scan_chunk(0, hh, qf_ref, kf_ref, vf_ref, gf_ref, hf_ref, cn_ref, cnb_ref, m_ref)
        _scan_chunk(1, hh, qb_ref, kb_ref, vb_ref, gb_ref, hb_ref, cn_ref, cnb_ref, m_ref)


def _mlstm_scan(q_t, k, v_t, gates):
    n_batch, rows, _ = k.shape
    n_t = rows // TM
    width = SCAN_HEADS * M_HEAD_DIM

    def bwd_tile(c):
        return jnp.where(c == 0, 0, n_t - c)

    kf_spec = pl.BlockSpec((1, TM, width), lambda b, h, c: (b, c, h))
    kb_spec = pl.BlockSpec((1, TM, width), lambda b, h, c: (b, bwd_tile(c), h))
    tf_spec = pl.BlockSpec((1, 1, width, TM), lambda b, h, c: (b, c, h, 0))
    tb_spec = pl.BlockSpec((1, 1, width, TM), lambda b, h, c: (b, bwd_tile(c), h, 0))
    gf_spec = pl.BlockSpec((1, 1, SCAN_HEADS, GATE_ROWS, TM), lambda b, h, c: (b, 0, h, 0, c))
    gb_spec = pl.BlockSpec((1, 1, SCAN_HEADS, GATE_ROWS, TM), lambda b, h, c: (b, 1, h, 0, bwd_tile(c)))
    out = jax.ShapeDtypeStruct((n_batch, rows, D_INNER), BF16)
    return pl.pallas_call(
        _scan_kernel,
        out_shape=(out, out),
        grid=(n_batch, M_HEADS // SCAN_HEADS, n_t),
        in_specs=[tf_spec, kf_spec, tf_spec, gf_spec, tb_spec, kb_spec, tb_spec, gb_spec],
        out_specs=(kf_spec, kb_spec),
        scratch_shapes=[pltpu.VMEM((2, SCAN_HEADS, STATE_ROWS, M_HEAD_DIM), F32),
                        pltpu.VMEM((2, SCAN_HEADS, STATE_ROWS, M_HEAD_DIM), BF16),
                        pltpu.VMEM((2, SCAN_HEADS, 8, 128), F32)],
        compiler_params=_params("parallel", "parallel", "arbitrary"),
        name="mlstm_scan",
    )(q_t, k, v_t, gates, q_t, k, v_t, gates)


def _post_norm(x, y, gate, pg, pb):
    return _ln(ALPHA * x + gate * y) * pg + pb


def _readout_kernel(n_stream, hf_ref, hb_ref, xc_ref, z_ref, *refs):
    mod_ref, gn_ref, sk_ref, w_ref, pg_ref, pb_ref, o_ref = refs[n_stream:]
    parts = []
    for h in range(M_HEADS):
        sl = slice(h * M_HEAD_DIM, (h + 1) * M_HEAD_DIM)
        hn = _ln(hf_ref[0, :, sl].astype(F32) + hb_ref[0, :, sl].astype(F32))
        a = hn * gn_ref[:, sl] + sk_ref[:, sl] * xc_ref[0, :, sl].astype(F32)
        parts.append((a * _silu(z_ref[0, :, sl].astype(F32))).astype(BF16))
    y = _dot(jnp.concatenate(parts, axis=1), w_ref[...])
    o_ref[0] = _post_norm(_stream_tile(refs[:n_stream]), y, _mod_chunk(mod_ref[0], 2), pg_ref[...], pb_ref[...])


def _mlstm_readout(hf, hb, xc, z, stream, mod_l, gn_w, skip, w_out, pg, pb):
    n_batch, rows = _stream_shape(stream)
    specs, args = _stream_specs(stream)
    return pl.pallas_call(
        functools.partial(_readout_kernel, len(args)),
        out_shape=jax.ShapeDtypeStruct((n_batch, rows, D_MODEL), F32),
        grid=(n_batch, rows // TM),
        in_specs=[_tile_spec(D_INNER), _tile_spec(D_INNER), _tile_spec(D_INNER), _tile_spec(D_INNER)] + specs
        + [_mod_spec(n_batch, True),
           _const_spec((1, D_INNER)), _const_spec((1, D_INNER)), _const_spec((D_INNER, D_MODEL)),
           _const_spec((1, D_MODEL)), _const_spec((1, D_MODEL))],
        out_specs=_tile_spec(D_MODEL),
        compiler_params=_params("parallel", "parallel"),
        name="mlstm_readout",
    )(hf, hb, xc, z, *args, mod_l, gn_w, skip, w_out, pg, pb)


def _fourier_out_kernel(n_sub, ctx_row, *refs):
    mx_refs, refs = refs[:n_sub], refs[n_sub:]
    if ctx_row is not None:
        mc_ref, refs = refs[0], refs[1:]
    x_refs, (mod_ref, w_ref, pg_ref, pb_ref, o_ref) = refs[:n_sub], refs[n_sub:]
    for r in range(n_sub):
        a = jnp.concatenate([mx_refs[r][0, j] for j in range(D_MODEL // LANES)], axis=1)
        if ctx_row is not None and r == 0:
            a = jnp.where(pl.program_id(1) == 0, mc_ref[0], a)
        y = _dot(a.astype(BF16), w_ref[...])
        m = _tile_mod(mod_ref, ctx_row, r)
        o_ref[0, r * TM:(r + 1) * TM, :] = _post_norm(x_refs[r][0], y, _mod_chunk(m, 2), pg_ref[...], pb_ref[...])


def _fourier_out(mixed_x, mixed_c, xs, mod_l, w_out, pg, pb):
    n_batch = xs.shape[0]
    ctx_tile = mixed_c is not None
    ctx_tiles = CTX_LEN // TM
    n_t = mixed_x.shape[2] // TM + (ctx_tiles if ctx_tile else 0)
    n_sub = _tiles_per_step(n_t * TM)
    mixed_off = ctx_tiles if ctx_tile else 0
    x_off = 0 if ctx_tile else ctx_tiles

    def mixed_map(r):
        return lambda b, t: (b, 0, jnp.maximum(n_sub * t + r - mixed_off, 0), 0)

    def x_map(r):
        return lambda b, t: (b, n_sub * t + r + x_off, 0)

    in_specs = [pl.BlockSpec((1, D_MODEL // LANES, TM, LANES), mixed_map(r)) for r in range(n_sub)]
    args = [mixed_x] * n_sub
    if ctx_tile:
        in_specs.append(pl.BlockSpec((1, CTX_LEN, D_MODEL), lambda b, t: (b, 0, 0)))
        args.append(mixed_c)
    in_specs += [pl.BlockSpec((1, TM, D_MODEL), x_map(r)) for r in range(n_sub)]
    args += [xs] * n_sub
    in_specs += [_const_spec((MOD_ROWS, 1, 6 * D_MODEL)),
                 _const_spec((D_MODEL, D_MODEL)), _const_spec((1, D_MODEL)), _const_spec((1, D_MODEL))]
    return pl.pallas_call(
        functools.partial(_fourier_out_kernel, n_sub, n_batch if ctx_tile else None),
        out_shape=jax.ShapeDtypeStruct((n_batch, n_t * TM, D_MODEL), F32),
        grid=(n_batch, n_t // n_sub),
        in_specs=in_specs,
        out_specs=pl.BlockSpec((1, n_sub * TM, D_MODEL), lambda b, t: (b, t, 0)),
        compiler_params=_params("parallel", "parallel"),
        name="fourier_out",
    )(*args, mod_l, w_out, pg, pb)


def _tile_mod(mod_ref, ctx_row, r):
    m = mod_ref[pl.program_id(0)]
    if ctx_row is not None and r == 0:
        m = jnp.where(pl.program_id(1) == 0, mod_ref[ctx_row], m)
    return m


def _tiles_per_step(rows):
    n_t = rows // TM
    return 3 if n_t % 3 == 0 else 2


def _mlp_kernel(n_sub, ctx_row, x_ref, mod_ref, w1_ref, w2_ref, pg_ref, pb_ref, o_ref):
    for r in range(n_sub):
        rs = slice(r * TM, (r + 1) * TM)
        x = x_ref[0, rs, :]
        m = _tile_mod(mod_ref, ctx_row, r)
        h = (_ln(x) * (1.0 + _mod_chunk(m, 4)) + _mod_chunk(m, 3)).astype(BF16)
        acc = jnp.zeros((TM, D_MODEL), F32)
        for j in range(D_FF // D_MODEL):
            sl = slice(j * D_MODEL, (j + 1) * D_MODEL)
            u = jnp.square(jnp.maximum(_dot(h, w1_ref[:, sl]), 0.0)).astype(BF16)
            acc = acc + _dot(u, w2_ref[sl, :])
        o_ref[0, rs, :] = _post_norm(x, acc, _mod_chunk(m, 5), pg_ref[...], pb_ref[...])


def _mlp(xs, mod_l, w1, w2, pg, pb, ctx_tile):
    n_batch, rows, _ = xs.shape
    n_sub = _tiles_per_step(rows)
    blk = pl.BlockSpec((1, n_sub * TM, D_MODEL), lambda b, t: (b, t, 0))
    return pl.pallas_call(
        functools.partial(_mlp_kernel, n_sub, n_batch if ctx_tile else None),
        out_shape=jax.ShapeDtypeStruct((n_batch, rows, D_MODEL), F32),
        grid=(n_batch, rows // (n_sub * TM)),
        in_specs=[blk, _const_spec((MOD_ROWS, 1, 6 * D_MODEL)),
                  _const_spec((D_MODEL, D_FF)), _const_spec((D_FF, D_MODEL)),
                  _const_spec((1, D_MODEL)), _const_spec((1, D_MODEL))],
        out_specs=blk,
        compiler_params=_params("parallel", "parallel"),
        name="mlp",
    )(xs, mod_l, w1, w2, pg, pb)


def _dft_tables():
    two_pi = 2.0 * np.pi
    c = np.arange(F_GROUP_DIM)
    ang = two_pi * ((c[:, None] * c[None, :]) % F_GROUP_DIM) / F_GROUP_DIM
    s_ch = 1.0 / math.sqrt(F_GROUP_DIM)
    f_ch = np.concatenate([np.cos(ang) * s_ch, -np.sin(ang) * s_ch], axis=1)

    seq = FFT_R * FFT_R * FFT_C
    r = np.arange(FFT_R)
    eye = np.eye(FFT_J)

    def tile_mix(w):
        wr, wi = np.kron(w.real, eye), np.kron(w.imag, eye)
        return np.block([[wr, -wi], [wi, wr]])

    w8 = np.exp(-2j * np.pi * ((r[:, None] * r[None, :]) % FFT_R) / FFT_R) / math.sqrt(FFT_R)
    k_a = tile_mix(w8)
    k_b = np.stack([tile_mix(w8 * np.exp(-2j * np.pi * (r[None, :] * a1) / (FFT_R * FFT_R)))
                    for a1 in range(FFT_R)])
    c = np.arange(FFT_C)
    t1 = np.arange(FFT_R * FFT_R)
    num = (c[None, :, None] * c[None, None, :] * (FFT_R * FFT_R) + t1[:, None, None] * c[None, None, :]) % seq
    ang = two_pi * num / seq
    f_c = np.concatenate([np.cos(ang), np.sin(ang)], axis=2) / math.sqrt(FFT_C)

    p = np.arange(CTX_LEN)
    ang = two_pi * ((p[:, None] * p[None, :]) % CTX_LEN) / CTX_LEN
    sc = 1.0 / math.sqrt(CTX_LEN)
    f_ctx = np.concatenate([np.cos(ang) * sc, np.sin(ang) * sc], axis=1)
    return tuple(jnp.asarray(a, dtype=F32).astype(BF16) for a in (f_ch, k_a, k_b, f_c, f_ctx))


def _cdft_kernel(ctx_tile, *refs):
    if ctx_tile:
        x_ref, mod_ref, f_ref, zrx_ref, zix_ref, zrc_ref, zic_ref = refs
    else:
        x_ref, mod_ref, f_ref, zrx_ref, zix_ref = refs
    m = mod_ref[0]
    h = (_ln(x_ref[0]) * (1.0 + _mod_chunk(m, 1)) + _mod_chunk(m, 0)).astype(BF16)
    f = f_ref[...]
    zr, zi = [], []
    for g in range(F_GROUPS):
        z = _dot(h[:, g * F_GROUP_DIM:(g + 1) * F_GROUP_DIM], f).astype(BF16)
        zr.append(z[:, :F_GROUP_DIM])
        zi.append(z[:, F_GROUP_DIM:])
    def store_latent():
        for g in range(F_GROUPS):
            zrx_ref[0, g] = zr[g]
            zix_ref[0, g] = zi[g]

    if ctx_tile:
        t = pl.program_id(1)

        @pl.when(t == 0)
        def _():
            zrc_ref[0] = jnp.concatenate(zr, axis=1)
            zic_ref[0] = jnp.concatenate(zi, axis=1)

        pl.when(t > 0)(store_latent)
    else:
        store_latent()


def _channel_dft(xs, mod_l, f_ch, ctx_tile):
    n_batch, rows, _ = xs.shape
    seq = rows - CTX_LEN
    off = CTX_LEN // TM
    zx = jax.ShapeDtypeStruct((n_batch, F_GROUPS, seq, F_GROUP_DIM), BF16)
    zc = jax.ShapeDtypeStruct((n_batch, CTX_LEN, D_MODEL), BF16)
    if ctx_tile:
        n_t = rows // TM
        x_out = pl.BlockSpec((1, F_GROUPS, TM, F_GROUP_DIM), lambda b, t: (b, 0, jnp.maximum(t - off, 0), 0))
        c_out = pl.BlockSpec((1, CTX_LEN, D_MODEL), lambda b, t: (b, 0, 0))
        out_shape, out_specs, x_in = (zx, zx, zc, zc), (x_out, x_out, c_out, c_out), _tile_spec(D_MODEL)
    else:
        n_t = seq // TM
        x_out = pl.BlockSpec((1, F_GROUPS, TM, F_GROUP_DIM), lambda b, t: (b, 0, t, 0))
        out_shape, out_specs, x_in = (zx, zx), (x_out, x_out), _tile_spec(D_MODEL, off)
    return pl.pallas_call(
        functools.partial(_cdft_kernel, ctx_tile),
        out_shape=out_shape,
        grid=(n_batch, n_t),
        in_specs=[x_in, _mod_spec(n_batch, ctx_tile), _const_spec((F_GROUP_DIM, 2 * F_GROUP_DIM))],
        out_specs=out_specs,
        compiler_params=_params("parallel", "arbitrary"),
        name="fourier_channel_dft",
    )(xs, mod_l, f_ch)


def _seq_dft_kernel(zr_ref, zi_ref, ka_ref, kb_ref, fc_ref, o_ref, ar_ref, ai_ref, br_ref, bi_ref):
    a_rows = FFT_R * FFT_C
    n_tiles = FFT_C // FFT_J

    def mix(load, store, base, step, k):
        offs = [base + i * step for i in range(FFT_R)]
        z = jnp.concatenate([load(0, o) for o in offs] + [load(1, o) for o in offs], axis=0)
        u = _dot(k, z).astype(BF16)
        for i, o in enumerate(offs):
            store(0, o, u[i * FFT_J:(i + 1) * FFT_J])
            store(1, o, u[(FFT_R + i) * FFT_J:(FFT_R + i + 1) * FFT_J])

    def load_in(p, o):
        return (zr_ref, zi_ref)[p][0, 0, pl.ds(o, FFT_J), :]

    def load_a(p, o):
        return (ar_ref, ai_ref)[p][pl.ds(o, FFT_J), :]

    def store_a(p, o, val):
        (ar_ref, ai_ref)[p][pl.ds(o, FFT_J), :] = val

    def store_b(p, o, val):
        (br_ref, bi_ref)[p][pl.ds(o, FFT_J), :] = val

    def stage_a(b, carry):
        for ct in range(n_tiles):
            mix(load_in, store_a, pl.multiple_of(b * FFT_C, FFT_C) + ct * FFT_J, a_rows, ka_ref[...])
        return carry

    lax.fori_loop(0, FFT_R, stage_a, 0)

    def stage_b(a1, carry):
        k = kb_ref[a1]
        for ct in range(n_tiles):
            mix(load_a, store_b, pl.multiple_of(a1 * a_rows, a_rows) + ct * FFT_J, FFT_C, k)
        return carry

    lax.fori_loop(0, FFT_R, stage_b, 0)

    def stage_c(t1, carry):
        rows = pl.ds(pl.multiple_of((t1 & (FFT_R - 1)) * a_rows + lax.shift_right_logical(t1, 3) * FFT_C, FFT_C),
                     FFT_C)
        z = jnp.concatenate([br_ref[rows, :], bi_ref[rows, :]], axis=0)
        y = _dot(fc_ref[t1], z)
        for j in range(FFT_CB // LANES):
            o_ref[0, j, pl.ds(t1, FFT_C, stride=FFT_R * FFT_R), :] = y[:, j * LANES:(j + 1) * LANES]
        return carry

    lax.fori_loop(0, FFT_R * FFT_R, stage_c, 0, unroll=DFT_UNROLL)


def _seq_dft(zr, zi, k_a, k_b, f_c):
    n_batch, n_blocks, seq, _ = zr.shape
    blk = pl.BlockSpec((1, 1, seq, FFT_CB), lambda b, j: (b, j, 0, 0), pipeline_mode=pl.Buffered(1))
    return pl.pallas_call(
        _seq_dft_kernel,
        out_shape=jax.ShapeDtypeStruct((n_batch, D_MODEL // LANES, seq, LANES), F32),
        grid=(n_batch, n_blocks),
        in_specs=[blk, blk, _const_spec(k_a.shape), _const_spec(k_b.shape), _const_spec(f_c.shape)],
        out_specs=pl.BlockSpec((1, FFT_CB // LANES, seq, LANES), lambda b, j: (b, j, 0, 0)),
        scratch_shapes=[pltpu.VMEM((seq, FFT_CB), BF16)] * 4,
        compiler_params=_params("parallel", "parallel"),
        name="fourier_seq_dft",
    )(zr, zi, k_a, k_b, f_c)


def _ctx_dft_kernel(zr_ref, zi_ref, f_ref, o_ref):
    o_ref[0] = _dot(f_ref[...], jnp.concatenate([zr_ref[0], zi_ref[0]], axis=0))


def _ctx_dft(zr, zi, f_ctx):
    n_batch = zr.shape[0]
    blk = pl.BlockSpec((1, CTX_LEN, D_MODEL), lambda b: (b, 0, 0))
    return pl.pallas_call(
        _ctx_dft_kernel,
        out_shape=jax.ShapeDtypeStruct((n_batch, CTX_LEN, D_MODEL), F32),
        grid=(n_batch,),
        in_specs=[blk, blk, _const_spec((CTX_LEN, 2 * CTX_LEN))],
        out_specs=blk,
        compiler_params=_params("parallel"),
        name="fourier_ctx_dft",
    )(zr, zi, f_ctx)


def _mlstm_layer(xs, mod_l, w_in, w_conv, b_conv, w_q, w_k, w_v, w_gate, b_gate, gn_w, skip, w_out, pg, pb):
    xm, z = _in_proj(xs, mod_l, w_in.astype(BF16))
    wg = w_gate.reshape(3, M_HEADS, M_HEAD_DIM, N_GATES).transpose(1, 0, 3, 2)
    w_qt = jnp.swapaxes(w_q, 1, 2).astype(BF16)
    w_vt = jnp.swapaxes(w_v, 1, 2).astype(BF16)
    xc, q_t, k, v_t, pre_t = _conv_qkv(
        xm, w_conv.reshape(CONV_K * CONV_K, D_INNER), b_conv.reshape(1, D_INNER),
        w_qt, w_k.astype(BF16), w_vt, wg.astype(BF16), b_gate.reshape(N_GATES, 1))
    hf, hb = _mlstm_scan(q_t, k, v_t, _gate_scans(pre_t))
    return _mlstm_readout(hf, hb, xc, z, xs, mod_l, gn_w.reshape(1, D_INNER), skip.reshape(1, D_INNER),
                          w_out.astype(BF16), pg, pb)


def _fourier_layer(xs, mod_l, w_out, pg, pb, need_ctx, tables):
    f_ch, k_a, k_b, f_c, f_ctx = tables
    z = _channel_dft(xs, mod_l, f_ch, need_ctx)
    mixed_x = _seq_dft(z[0], z[1], k_a, k_b, f_c)
    mixed_c = _ctx_dft(z[2], z[3], f_ctx) if need_ctx else None
    return _fourier_out(mixed_x, mixed_c, xs, mod_l, w_out.astype(BF16), pg, pb)


def kernel(x, c, ctx, c_ctx, ada_w, ada_b, post_g, post_b, m_w_in, m_w_conv, m_b_conv, m_w_q, m_w_k, m_w_v, m_w_gate, m_b_gate, m_gn_w, m_skip, m_w_out, f_w_out, mlp_w1, mlp_w2):
    n_batch = x.shape[0]
    assert n_batch < MOD_ROWS and x.shape[1] == FFT_R * FFT_R * FFT_C and ctx.shape[1] == CTX_LEN == TM
    xs = (ctx, x)
    s_in = jnp.concatenate([c, c_ctx[None, :], jnp.zeros((MOD_ROWS - n_batch - 1, D_MODEL), F32)], axis=0)
    mod = _ada_table(s_in, ada_w, ada_b)
    tables = _dft_tables()
    for i in range(DEPTH):
        is_mlstm = i % N_MIXERS == 0
        j = i // N_MIXERS
        need_ctx = i < DEPTH - 1
        mod_l = mod[i].reshape(MOD_ROWS, 1, 6 * D_MODEL)
        pg = post_g[i].reshape(2, 1, D_MODEL)
        pb = post_b[i].reshape(2, 1, D_MODEL)
        if is_mlstm:
            xs = _mlstm_layer(xs, mod_l, m_w_in[j], m_w_conv[j], m_b_conv[j], m_w_q[j], m_w_k[j], m_w_v[j],
                              m_w_gate[j], m_b_gate[j], m_gn_w[j], m_skip[j], m_w_out[j], pg[0], pb[0])
            if not need_ctx:
                xs = xs[:, CTX_LEN:]
        else:
            xs = _fourier_layer(xs, mod_l, f_w_out[j], pg[0], pb[0], need_ctx, tables)
        xs = _mlp(xs, mod_l, mlp_w1[i].astype(BF16), mlp_w2[i].astype(BF16), pg[1], pb[1], need_ctx)
    return xs
```

```python
import functools
import math

import numpy as np
import jax
import jax.numpy as jnp
from jax import lax
from jax.experimental import pallas as pl
from jax.experimental.pallas import tpu as pltpu

D_MODEL = 1024
DEPTH = 4
GRID_W = 64
CTX_LEN = 256
N_MIXERS = 2
D_INNER = 2 * D_MODEL
M_HEADS = 4
M_HEAD_DIM = D_INNER // M_HEADS
CONV_K = 3
F_GROUPS = 4
F_GROUP_DIM = D_MODEL // F_GROUPS
D_FF = 4 * D_MODEL
ALPHA = float((2 * DEPTH) ** 0.25)
LN_EPS = 1e-5

TM = 256
N_GATES = 4 * M_HEADS
GATE_ROWS = 8
MOD_ROWS = 8
FFT_R = 8
FFT_C = 128
FFT_J = 16
FFT_CB = F_GROUP_DIM
LANES = 128
GATE_CHUNKS = 3
N_ROWS = 16
STATE_ROWS = N_ROWS + M_HEAD_DIM
MXU_WIDTH = 256
SCAN_HEADS = 4
DFT_UNROLL = 8
VMEM_LIMIT = 52 * 1024 * 1024

F32 = jnp.float32
BF16 = jnp.bfloat16


def _params(*sem):
    return pltpu.CompilerParams(dimension_semantics=sem, vmem_limit_bytes=VMEM_LIMIT)


def _ln(x):
    mu = jnp.mean(x, axis=-1, keepdims=True)
    xc = x - mu
    var = jnp.mean(xc * xc, axis=-1, keepdims=True)
    return xc * lax.rsqrt(var + LN_EPS)


def _silu(x):
    return x / (1.0 + jnp.exp(-x))


def _dot(a, b):
    return jnp.dot(a, b, preferred_element_type=F32)


def _dot_nt(a, b):
    return lax.dot_general(a, b, (((1,), (1,)), ((), ())), preferred_element_type=F32)


def _mod_chunk(m, j):
    return m[:, j * D_MODEL:(j + 1) * D_MODEL]


def _const_spec(shape):
    nd = len(shape)
    return pl.BlockSpec(shape, lambda *_: (0,) * nd, pipeline_mode=pl.Buffered(1))


def _mod_spec(n_batch, ctx_tile):
    if ctx_tile:
        return pl.BlockSpec((1, 1, 6 * D_MODEL), lambda b, t: (jnp.where(t == 0, n_batch, b), 0, 0))
    return pl.BlockSpec((1, 1, 6 * D_MODEL), lambda b, t: (b, 0, 0))


def _tile_spec(width, off=0):
    return pl.BlockSpec((1, TM, width), lambda b, t: (b, t + off, 0))


def _stream_specs(stream, n_sub):
    split = isinstance(stream, tuple)
    off = CTX_LEN // TM if split else 0

    def tile_map(r):
        return lambda b, t: (b, jnp.maximum(n_sub * t + r - off, 0), 0)

    specs = [pl.BlockSpec((1, TM, D_MODEL), tile_map(r)) for r in range(n_sub)]
    if split:
        return [pl.BlockSpec((1, CTX_LEN, D_MODEL), lambda b, t: (b, 0, 0))] + specs, [stream[0]] + [stream[1]] * n_sub
    return specs, [stream] * n_sub


def _stream_shape(stream):
    if isinstance(stream, tuple):
        return stream[1].shape[0], stream[0].shape[1] + stream[1].shape[1]
    return stream.shape[0], stream.shape[1]


def _stream_tile(refs, n_sub, r):
    if len(refs) > n_sub:
        x = refs[1 + r][0]
        return jnp.where(pl.program_id(1) == 0, refs[0][0], x) if r == 0 else x
    return refs[r][0]


def _ada_kernel(s_ref, w_ref, b_ref, o_ref):
    s = _silu(s_ref[...])
    o_ref[0] = jnp.dot(s, w_ref[0], preferred_element_type=F32, precision=lax.Precision.HIGHEST) + b_ref[0]


def _ada_table(s_in, ada_w, ada_b):
    n_chunks = 6
    return pl.pallas_call(
        _ada_kernel,
        out_shape=jax.ShapeDtypeStruct((DEPTH, MOD_ROWS, 6 * D_MODEL), F32),
        grid=(DEPTH, n_chunks),
        in_specs=[pl.BlockSpec((MOD_ROWS, D_MODEL), lambda l, j: (0, 0)),
                  pl.BlockSpec((1, D_MODEL, D_MODEL), lambda l, j: (l, 0, j)),
                  pl.BlockSpec((1, 1, D_MODEL), lambda l, j: (l, 0, j))],
        out_specs=pl.BlockSpec((1, MOD_ROWS, D_MODEL), lambda l, j: (l, 0, j)),
        compiler_params=_params("parallel", "parallel"),
        name="ada_table",
    )(s_in, ada_w, ada_b.reshape(DEPTH, 1, 6 * D_MODEL))


def _in_kernel(n_stream, n_sub, ctx_row, *refs):
    mod_ref, w_ref, xm_ref, z_ref = refs[n_stream:]
    for r in range(n_sub):
        rs = slice(r * TM, (r + 1) * TM)
        x = _stream_tile(refs[:n_stream], n_sub, r)
        m = _tile_mod(mod_ref, ctx_row, r)
        h = (_ln(x) * (1.0 + _mod_chunk(m, 1)) + _mod_chunk(m, 0)).astype(BF16)
        for j in range(2):
            sl = slice(j * D_MODEL, (j + 1) * D_MODEL)
            xm_ref[0, rs, sl] = _dot(h, w_ref[:, sl]).astype(BF16)
            z_ref[0, rs, sl] = _dot(h, w_ref[:, D_INNER + j * D_MODEL:D_INNER + (j + 1) * D_MODEL]).astype(BF16)


def _in_proj(stream, mod_l, w_in):
    n_batch, rows = _stream_shape(stream)
    n_sub = _tiles_per_step(rows)
    specs, args = _stream_specs(stream, n_sub)
    out = jax.ShapeDtypeStruct((n_batch, rows, D_INNER), BF16)
    out_spec = pl.BlockSpec((1, n_sub * TM, D_INNER), lambda b, t: (b, t, 0))
    return pl.pallas_call(
        functools.partial(_in_kernel, len(args), n_sub, n_batch),
        out_shape=(out, out),
        grid=(n_batch, rows // (n_sub * TM)),
        in_specs=specs + [_const_spec((MOD_ROWS, 1, 6 * D_MODEL)), _const_spec((D_MODEL, 2 * D_INNER))],
        out_specs=(out_spec, out_spec),
        compiler_params=_params("parallel", "parallel"),
        name="mlstm_in_proj",
    )(*args, mod_l, w_in)


def _conv_qkv_kernel(xm_ref, prev_ref, next_ref, wc_ref, bc_ref, wq_ref, wk_ref, wv_ref, wg_ref, bg_ref,
                     xc_ref, q_ref, k_ref, v_ref, pre_ref):
    t = pl.program_id(1)
    n_t = pl.num_programs(1)
    is_ctx = t == 0
    row = lax.broadcasted_iota(jnp.int32, (TM, 1), 0)
    period = jnp.where(is_ctx, CTX_LEN - 1, GRID_W - 1)
    pos = row & period
    has_left = pos != 0
    has_right = pos != period
    prev_ok = t >= 2
    next_ok = jnp.logical_and(t >= 1, t <= n_t - 2)
    row_w = jnp.where(is_ctx, 0.0, 1.0)
    pre_t = jnp.zeros((N_GATES, TM), F32) + bg_ref[...]
    for h in range(M_HEADS):
        sl = slice(h * M_HEAD_DIM, (h + 1) * M_HEAD_DIM)
        xm = xm_ref[0, :, sl]
        e = jnp.concatenate([jnp.where(prev_ok, prev_ref[0, :, sl], 0).astype(F32),
                             xm.astype(F32),
                             jnp.where(next_ok, next_ref[0, :, sl], 0).astype(F32)], axis=0)
        p = []
        for dc in range(CONV_K):
            acc = None
            for dr in range(CONV_K):
                w = wc_ref[CONV_K * dr + dc:CONV_K * dr + dc + 1, sl]
                if dr != CONV_K // 2:
                    w = w * row_w
                term = w * e[dr * GRID_W:dr * GRID_W + TM]
                acc = term if acc is None else acc + term
            p.append(acc)
        y = (p[1] + bc_ref[:, sl]
             + jnp.where(has_left, pltpu.roll(p[0], 1, axis=0), 0.0)
             + jnp.where(has_right, pltpu.roll(p[2], TM - 1, axis=0), 0.0))
        xc = _silu(y).astype(BF16)
        xc_ref[0, :, sl] = xc
        q_t = _dot_nt(wq_ref[h], xc).astype(BF16)
        k = (_dot(xc, wk_ref[h]) * (M_HEAD_DIM ** -0.5)).astype(BF16)
        v_t = _dot_nt(wv_ref[h], xm).astype(BF16)
        q_ref[0, 0, sl, :] = q_t
        k_ref[0, :, sl] = k
        v_ref[0, 0, sl, :] = v_t
        pre_t = pre_t + _dot(wg_ref[h, 0], q_t) + _dot_nt(wg_ref[h, 1], k) + _dot(wg_ref[h, 2], v_t)
    pre_ref[0] = pre_t


def _conv_qkv(xm, w_conv, b_conv, w_q, w_k, w_v, w_g, b_g):
    n_batch, rows, _ = xm.shape
    n_halo = rows // GRID_W
    per = TM // GRID_W
    wide = jax.ShapeDtypeStruct((n_batch, rows, D_INNER), BF16)
    wide_t = jax.ShapeDtypeStruct((n_batch, rows // TM, D_INNER, TM), BF16)
    col_spec = pl.BlockSpec((1, 1, D_INNER, TM), lambda b, t: (b, t, 0, 0))
    return pl.pallas_call(
        _conv_qkv_kernel,
        out_shape=(wide, wide_t, wide, wide_t, jax.ShapeDtypeStruct((n_batch, N_GATES, rows), F32)),
        grid=(n_batch, rows // TM),
        in_specs=[_tile_spec(D_INNER),
                  pl.BlockSpec((1, GRID_W, D_INNER), lambda b, t: (b, jnp.maximum(per * t - 1, 0), 0)),
                  pl.BlockSpec((1, GRID_W, D_INNER), lambda b, t: (b, jnp.minimum(per * t + per, n_halo - 1), 0)),
                  _const_spec((CONV_K * CONV_K, D_INNER)),
                  _const_spec((1, D_INNER)),
                  _const_spec((M_HEADS, M_HEAD_DIM, M_HEAD_DIM)),
                  _const_spec((M_HEADS, M_HEAD_DIM, M_HEAD_DIM)),
                  _const_spec((M_HEADS, M_HEAD_DIM, M_HEAD_DIM)),
                  _const_spec((M_HEADS, 3, N_GATES, M_HEAD_DIM)),
                  _const_spec((N_GATES, 1))],
        out_specs=(_tile_spec(D_INNER), col_spec, _tile_spec(D_INNER), col_spec,
                   pl.BlockSpec((1, N_GATES, TM), lambda b, t: (b, 0, t))),
        compiler_params=_params("parallel", "parallel"),
        name="mlstm_conv_qkv",
    )(xm, xm, xm, w_conv, b_conv, w_q, w_k, w_v, w_g, b_g)


def _scan_lanes(x, combine, fill, reverse):
    n = x.shape[-1]
    lane = lax.broadcasted_iota(jnp.int32, x.shape, 1)
    s = 1
    while s < n:
        if reverse:
            sh = jnp.where(lane < n - s, pltpu.roll(x, n - s, axis=1), fill)
        else:
            sh = jnp.where(lane >= s, pltpu.roll(x, s, axis=1), fill)
        x = combine(x, sh)
        s *= 2
    return x


def _gate_kernel(pre_ref, o_ref):
    zeros = jnp.zeros((GATE_ROWS - 3, TM), F32)
    for c in range(GATE_CHUNKS):
        lanes = slice(c * TM, (c + 1) * TM)
        for d in range(2):
            blk = pre_ref[0, 2 * M_HEADS * d:2 * M_HEADS * (d + 1), lanes]
            log_f = jnp.minimum(blk, 0.0) - jnp.log1p(jnp.exp(-jnp.abs(blk)))
            b = pltpu.roll(_scan_lanes(log_f, jnp.add, 0.0, reverse=d == 1), M_HEADS, axis=0)
            g = blk - b
            big = _scan_lanes(g, jnp.maximum, -jnp.inf, reverse=d == 1)
            for h in range(M_HEADS):
                o_ref[0, d, h, :, lanes] = jnp.concatenate([b[h:h + 1], g[h:h + 1], big[h:h + 1], zeros], axis=0)


def _gate_scans(pre_t):
    n_batch, _, rows = pre_t.shape
    width = GATE_CHUNKS * TM
    return pl.pallas_call(
        _gate_kernel,
        out_shape=jax.ShapeDtypeStruct((n_batch, 2, M_HEADS, GATE_ROWS, rows), F32),
        grid=(n_batch, rows // width),
        in_specs=[pl.BlockSpec((1, N_GATES, width), lambda b, t: (b, 0, t))],
        out_specs=pl.BlockSpec((1, 2, M_HEADS, GATE_ROWS, width), lambda b, t: (b, 0, 0, 0, t)),
        compiler_params=_params("parallel", "parallel"),
        name="mlstm_gate_scans",
    )(pre_t)


def _scan_chunk(d, hh, q_ref, k_ref, v_ref, gate_ref, h_ref, cn_ref, cnb_ref, m_ref):
    sl = slice(hh * M_HEAD_DIM, (hh + 1) * M_HEAD_DIM)
    q_t = q_ref[0, 0, sl, :]
    k = k_ref[0, :, sl]
    v_t = v_ref[0, 0, sl, :]
    gates = gate_ref[0, 0, hh]
    b_row, g_row, big_row = gates[0:1], gates[1:2], gates[2:3]
    m_prev = m_ref[d, hh, 0:1, 0:1]
    last = TM - 1 if d == 0 else 0
    m_row = jnp.maximum(m_prev, big_row)
    m_last = m_row[:, last:last + 1]
    w_inter = jnp.exp(m_prev - m_row)
    den_floor = jnp.exp(-(b_row + m_row))
    w_key = jnp.exp(g_row - m_last).astype(BF16)
    g_col = jnp.transpose(gates)[:, 1:2]
    jj = lax.broadcasted_iota(jnp.int32, (TM, TM), 0)
    ii = lax.broadcasted_iota(jnp.int32, (TM, TM), 1)
    visible = jj <= ii if d == 0 else jj >= ii
    pad = jnp.zeros((N_ROWS - 1, TM), BF16)

    s_t = _dot(k, q_t) * jnp.where(visible, jnp.exp(g_col - m_row), 0.0)
    inter = _dot(cnb_ref[d, hh], q_t)
    v_one = jnp.concatenate([jnp.ones((1, TM), BF16), pad, v_t], axis=0)
    intra = _dot(v_one, s_t.astype(BF16))
    den = w_inter * inter[0:1] + intra[0:1]
    scale = 1.0 / jnp.maximum(jnp.abs(den), den_floor)
    h_t = (w_inter * scale) * inter[N_ROWS:] + scale * intra[N_ROWS:]
    h_ref[0, :, sl] = jnp.transpose(h_t).astype(BF16)

    decay = jnp.exp(m_prev - m_last)
    vw = jnp.concatenate([w_key, pad, v_t * w_key], axis=0)
    for c0 in range(0, M_HEAD_DIM, MXU_WIDTH):
        cn = decay * cn_ref[d, hh, :, c0:c0 + MXU_WIDTH] + _dot(vw, k[:, c0:c0 + MXU_WIDTH])
        cn_ref[d, hh, :, c0:c0 + MXU_WIDTH] = cn
        cnb_ref[d, hh, :, c0:c0 + MXU_WIDTH] = cn.astype(BF16)
    m_ref[d, hh] = jnp.zeros(m_ref.shape[2:], F32) + (b_row[:, last:last + 1] + m_last)


def _scan_kernel(qf_ref, kf_ref, vf_ref, gf_ref, qb_ref, kb_ref, vb_ref, gb_ref, hf_ref, hb_ref,
                 cn_ref, cnb_ref, m_ref):
    @pl.when(pl.program_id(2) == 0)
    def _():
        cn_ref[...] = jnp.zeros_like(cn_ref)
        cnb_ref[...] = jnp.zeros_like(cnb_ref)
        m_ref[...] = jnp.zeros_like(m_ref)

    for hh in range(SCAN_HEADS):
        _scan_chunk(0, hh, qf_ref, kf_ref, vf_ref, gf_ref, hf_ref, cn_ref, cnb_ref, m_ref)
        _scan_chunk(1, hh, qb_ref, kb_ref, vb_ref, gb_ref, hb_ref, cn_ref, cnb_ref, m_ref)


def _mlstm_scan(q_t, k, v_t, gates):
    n_batch, rows, _ = k.shape
    n_t = rows // TM
    width = SCAN_HEADS * M_HEAD_DIM

    def bwd_tile(c):
        return jnp.where(c == 0, 0, n_t - c)

    kf_spec = pl.BlockSpec((1, TM, width), lambda b, h, c: (b, c, h))
    kb_spec = pl.BlockSpec((1, TM, width), lambda b, h, c: (b, bwd_tile(c), h))
    tf_spec = pl.BlockSpec((1, 1, width, TM), lambda b, h, c: (b, c, h, 0))
    tb_spec = pl.BlockSpec((1, 1, width, TM), lambda b, h, c: (b, bwd_tile(c), h, 0))
    gf_spec = pl.BlockSpec((1, 1, SCAN_HEADS, GATE_ROWS, TM), lambda b, h, c: (b, 0, h, 0, c))
    gb_spec = pl.BlockSpec((1, 1, SCAN_HEADS, GATE_ROWS, TM), lambda b, h, c: (b, 1, h, 0, bwd_tile(c)))
    out = jax.ShapeDtypeStruct((n_batch, rows, D_INNER), BF16)
    return pl.pallas_call(
        _scan_kernel,
        out_shape=(out, out),
        grid=(n_batch, M_HEADS // SCAN_HEADS, n_t),
        in_specs=[tf_spec, kf_spec, tf_spec, gf_spec, tb_spec, kb_spec, tb_spec, gb_spec],
        out_specs=(kf_spec, kb_spec),
        scratch_shapes=[pltpu.VMEM((2, SCAN_HEADS, STATE_ROWS, M_HEAD_DIM), F32),
                        pltpu.VMEM((2, SCAN_HEADS, STATE_ROWS, M_HEAD_DIM), BF16),
                        pltpu.VMEM((2, SCAN_HEADS, 8, 128), F32)],
        compiler_params=_params("parallel", "parallel", "arbitrary"),
        name="mlstm_scan",
    )(q_t, k, v_t, gates, q_t, k, v_t, gates)


def _post_norm(x, y, gate, pg, pb):
    return _ln(ALPHA * x + gate * y) * pg + pb


def _readout_kernel(n_stream, n_sub, ctx_row, hf_ref, hb_ref, xc_ref, z_ref, *refs):
    mod_ref, gn_ref, sk_ref, w_ref, pg_ref, pb_ref, o_ref = refs[n_stream:]
    for r in range(n_sub):
        rs = slice(r * TM, (r + 1) * TM)
        parts = []
        for h in range(M_HEADS):
            sl = slice(h * M_HEAD_DIM, (h + 1) * M_HEAD_DIM)
            hn = _ln(hf_ref[0, rs, sl].astype(F32) + hb_ref[0, rs, sl].astype(F32))
            a = hn * gn_ref[:, sl] + sk_ref[:, sl] * xc_ref[0, rs, sl].astype(F32)
            parts.append((a * _silu(z_ref[0, rs, sl].astype(F32))).astype(BF16))
        y = _dot(jnp.concatenate(parts, axis=1), w_ref[...])
        m = _tile_mod(mod_ref, ctx_row, r)
        o_ref[0, rs, :] = _post_norm(_stream_tile(refs[:n_stream], n_sub, r), y, _mod_chunk(m, 2),
                                     pg_ref[...], pb_ref[...])


def _mlstm_readout(hf, hb, xc, z, stream, mod_l, gn_w, skip, w_out, pg, pb):
    n_batch, rows = _stream_shape(stream)
    n_sub = _tiles_per_step(rows)
    specs, args = _stream_specs(stream, n_sub)
    wide = pl.BlockSpec((1, n_sub * TM, D_INNER), lambda b, t: (b, t, 0))
    return pl.pallas_call(
        functools.partial(_readout_kernel, len(args), n_sub, n_batch),
        out_shape=jax.ShapeDtypeStruct((n_batch, rows, D_MODEL), F32),
        grid=(n_batch, rows // (n_sub * TM)),
        in_specs=[wide, wide, wide, wide] + specs
        + [_const_spec((MOD_ROWS, 1, 6 * D_MODEL)),
           _const_spec((1, D_INNER)), _const_spec((1, D_INNER)), _const_spec((D_INNER, D_MODEL)),
           _const_spec((1, D_MODEL)), _const_spec((1, D_MODEL))],
        out_specs=pl.BlockSpec((1, n_sub * TM, D_MODEL), lambda b, t: (b, t, 0)),
        compiler_params=_params("parallel", "parallel"),
        name="mlstm_readout",
    )(hf, hb, xc, z, *args, mod_l, gn_w, skip, w_out, pg, pb)


def _fourier_out_kernel(n_sub, ctx_row, *refs):
    mx_refs, refs = refs[:n_sub], refs[n_sub:]
    if ctx_row is not None:
        mc_ref, refs = refs[0], refs[1:]
    x_refs, (mod_ref, w_ref, pg_ref, pb_ref, o_ref) = refs[:n_sub], refs[n_sub:]
    for r in range(n_sub):
        a = jnp.concatenate([mx_refs[r][0, j] for j in range(D_MODEL // LANES)], axis=1)
        if ctx_row is not None and r == 0:
            a = jnp.where(pl.program_id(1) == 0, mc_ref[0], a)
        y = _dot(a.astype(BF16), w_ref[...])
        m = _tile_mod(mod_ref, ctx_row, r)
        o_ref[0, r * TM:(r + 1) * TM, :] = _post_norm(x_refs[r][0], y, _mod_chunk(m, 2), pg_ref[...], pb_ref[...])


def _fourier_out(mixed_x, mixed_c, xs, mod_l, w_out, pg, pb):
    n_batch = xs.shape[0]
    ctx_tile = mixed_c is not None
    ctx_tiles = CTX_LEN // TM
    n_t = mixed_x.shape[2] // TM + (ctx_tiles if ctx_tile else 0)
    n_sub = _tiles_per_step(n_t * TM)
    mixed_off = ctx_tiles if ctx_tile else 0
    x_off = 0 if ctx_tile else ctx_tiles

    def mixed_map(r):
        return lambda b, t: (b, 0, jnp.maximum(n_sub * t + r - mixed_off, 0), 0)

    def x_map(r):
        return lambda b, t: (b, n_sub * t + r + x_off, 0)

    in_specs = [pl.BlockSpec((1, D_MODEL // LANES, TM, LANES), mixed_map(r)) for r in range(n_sub)]
    args = [mixed_x] * n_sub
    if ctx_tile:
        in_specs.append(pl.BlockSpec((1, CTX_LEN, D_MODEL), lambda b, t: (b, 0, 0)))
        args.append(mixed_c)
    in_specs += [pl.BlockSpec((1, TM, D_MODEL), x_map(r)) for r in range(n_sub)]
    args += [xs] * n_sub
    in_specs += [_const_spec((MOD_ROWS, 1, 6 * D_MODEL)),
                 _const_spec((D_MODEL, D_MODEL)), _const_spec((1, D_MODEL)), _const_spec((1, D_MODEL))]
    return pl.pallas_call(
        functools.partial(_fourier_out_kernel, n_sub, n_batch if ctx_tile else None),
        out_shape=jax.ShapeDtypeStruct((n_batch, n_t * TM, D_MODEL), F32),
        grid=(n_batch, n_t // n_sub),
        in_specs=in_specs,
        out_specs=pl.BlockSpec((1, n_sub * TM, D_MODEL), lambda b, t: (b, t, 0)),
        compiler_params=_params("parallel", "parallel"),
        name="fourier_out",
    )(*args, mod_l, w_out, pg, pb)


def _tile_mod(mod_ref, ctx_row, r):
    m = mod_ref[pl.program_id(0)]
    if ctx_row is not None and r == 0:
        m = jnp.where(pl.program_id(1) == 0, mod_ref[ctx_row], m)
    return m


def _tiles_per_step(rows):
    n_t = rows // TM
    return 3 if n_t % 3 == 0 else 2


def _mlp_kernel(n_sub, ctx_row, x_ref, mod_ref, w1_ref, w2_ref, pg_ref, pb_ref, o_ref):
    for r in range(n_sub):
        rs = slice(r * TM, (r + 1) * TM)
        x = x_ref[0, rs, :]
        m = _tile_mod(mod_ref, ctx_row, r)
        h = (_ln(x) * (1.0 + _mod_chunk(m, 4)) + _mod_chunk(m, 3)).astype(BF16)
        acc = jnp.zeros((TM, D_MODEL), F32)
        for j in range(D_FF // D_MODEL):
            sl = slice(j * D_MODEL, (j + 1) * D_MODEL)
            u = jnp.square(jnp.maximum(_dot(h, w1_ref[:, sl]), 0.0)).astype(BF16)
            acc = acc + _dot(u, w2_ref[sl, :])
        o_ref[0, rs, :] = _post_norm(x, acc, _mod_chunk(m, 5), pg_ref[...], pb_ref[...])


def _mlp(xs, mod_l, w1, w2, pg, pb, ctx_tile):
    n_batch, rows, _ = xs.shape
    n_sub = _tiles_per_step(rows)
    blk = pl.BlockSpec((1, n_sub * TM, D_MODEL), lambda b, t: (b, t, 0))
    return pl.pallas_call(
        functools.partial(_mlp_kernel, n_sub, n_batch if ctx_tile else None),
        out_shape=jax.ShapeDtypeStruct((n_batch, rows, D_MODEL), F32),
        grid=(n_batch, rows // (n_sub * TM)),
        in_specs=[blk, _const_spec((MOD_ROWS, 1, 6 * D_MODEL)),
                  _const_spec((D_MODEL, D_FF)), _const_spec((D_FF, D_MODEL)),
                  _const_spec((1, D_MODEL)), _const_spec((1, D_MODEL))],
        out_specs=blk,
        compiler_params=_params("parallel", "parallel"),
        name="mlp",
    )(xs, mod_l, w1, w2, pg, pb)


def _dft_tables():
    two_pi = 2.0 * np.pi
    c = np.arange(F_GROUP_DIM)
    ang = two_pi * ((c[:, None] * c[None, :]) % F_GROUP_DIM) / F_GROUP_DIM
    s_ch = 1.0 / math.sqrt(F_GROUP_DIM)
    f_ch = np.concatenate([np.cos(ang) * s_ch, -np.sin(ang) * s_ch], axis=1)

    seq = FFT_R * FFT_R * FFT_C
    r = np.arange(FFT_R)
    eye = np.eye(FFT_J)

    def tile_mix(w):
        wr, wi = np.kron(w.real, eye), np.kron(w.imag, eye)
        return np.block([[wr, -wi], [wi, wr]])

    w8 = np.exp(-2j * np.pi * ((r[:, None] * r[None, :]) % FFT_R) / FFT_R) / math.sqrt(FFT_R)
    k_a = tile_mix(w8)
    k_b = np.stack([tile_mix(w8 * np.exp(-2j * np.pi * (r[None, :] * a1) / (FFT_R * FFT_R)))
                    for a1 in range(FFT_R)])
    c = np.arange(FFT_C)
    t1 = np.arange(FFT_R * FFT_R)
    num = (c[None, :, None] * c[None, None, :] * (FFT_R * FFT_R) + t1[:, None, None] * c[None, None, :]) % seq
    ang = two_pi * num / seq
    f_c = np.concatenate([np.cos(ang), np.sin(ang)], axis=2) / math.sqrt(FFT_C)

    p = np.arange(CTX_LEN)
    ang = two_pi * ((p[:, None] * p[None, :]) % CTX_LEN) / CTX_LEN
    sc = 1.0 / math.sqrt(CTX_LEN)
    f_ctx = np.concatenate([np.cos(ang) * sc, np.sin(ang) * sc], axis=1)
    return tuple(jnp.asarray(a, dtype=F32).astype(BF16) for a in (f_ch, k_a, k_b, f_c, f_ctx))


def _cdft_kernel(ctx_tile, *refs):
    if ctx_tile:
        x_ref, mod_ref, f_ref, zrx_ref, zix_ref, zrc_ref, zic_ref = refs
    else:
        x_ref, mod_ref, f_ref, zrx_ref, zix_ref = refs
    m = mod_ref[0]
    h = (_ln(x_ref[0]) * (1.0 + _mod_chunk(m, 1)) + _mod_chunk(m, 0)).astype(BF16)
    f = f_ref[...]
    zr, zi = [], []
    for g in range(F_GROUPS):
        z = _dot(h[:, g * F_GROUP_DIM:(g + 1) * F_GROUP_DIM], f).astype(BF16)
        zr.append(z[:, :F_GROUP_DIM])
        zi.append(z[:, F_GROUP_DIM:])
    def store_latent():
        for g in range(F_GROUPS):
            zrx_ref[0, g] = zr[g]
            zix_ref[0, g] = zi[g]

    if ctx_tile:
        t = pl.program_id(1)

        @pl.when(t == 0)
        def _():
            zrc_ref[0] = jnp.concatenate(zr, axis=1)
            zic_ref[0] = jnp.concatenate(zi, axis=1)

        pl.when(t > 0)(store_latent)
    else:
        store_latent()


def _channel_dft(xs, mod_l, f_ch, ctx_tile):
    n_batch, rows, _ = xs.shape
    seq = rows - CTX_LEN
    off = CTX_LEN // TM
    zx = jax.ShapeDtypeStruct((n_batch, F_GROUPS, seq, F_GROUP_DIM), BF16)
    zc = jax.ShapeDtypeStruct((n_batch, CTX_LEN, D_MODEL), BF16)
    if ctx_tile:
        n_t = rows // TM
        x_out = pl.BlockSpec((1, F_GROUPS, TM, F_GROUP_DIM), lambda b, t: (b, 0, jnp.maximum(t - off, 0), 0))
        c_out = pl.BlockSpec((1, CTX_LEN, D_MODEL), lambda b, t: (b, 0, 0))
        out_shape, out_specs, x_in = (zx, zx, zc, zc), (x_out, x_out, c_out, c_out), _tile_spec(D_MODEL)
    else:
        n_t = seq // TM
        x_out = pl.BlockSpec((1, F_GROUPS, TM, F_GROUP_DIM), lambda b, t: (b, 0, t, 0))
        out_shape, out_specs, x_in = (zx, zx), (x_out, x_out), _tile_spec(D_MODEL, off)
    return pl.pallas_call(
        functools.partial(_cdft_kernel, ctx_tile),
        out_shape=out_shape,
        grid=(n_batch, n_t),
        in_specs=[x_in, _mod_spec(n_batch, ctx_tile), _const_spec((F_GROUP_DIM, 2 * F_GROUP_DIM))],
        out_specs=out_specs,
        compiler_params=_params("parallel", "arbitrary"),
        name="fourier_channel_dft",
    )(xs, mod_l, f_ch)


def _seq_dft_kernel(zr_ref, zi_ref, ka_ref, kb_ref, fc_ref, o_ref, ar_ref, ai_ref, br_ref, bi_ref):
    a_rows = FFT_R * FFT_C
    n_tiles = FFT_C // FFT_J

    def mix(load, store, base, step, k):
        offs = [base + i * step for i in range(FFT_R)]
        z = jnp.concatenate([load(0, o) for o in offs] + [load(1, o) for o in offs], axis=0)
        u = _dot(k, z).astype(BF16)
        for i, o in enumerate(offs):
            store(0, o, u[i * FFT_J:(i + 1) * FFT_J])
            store(1, o, u[(FFT_R + i) * FFT_J:(FFT_R + i + 1) * FFT_J])

    def load_in(p, o):
        return (zr_ref, zi_ref)[p][0, 0, pl.ds(o, FFT_J), :]

    def load_a(p, o):
        return (ar_ref, ai_ref)[p][pl.ds(o, FFT_J), :]

    def store_a(p, o, val):
        (ar_ref, ai_ref)[p][pl.ds(o, FFT_J), :] = val

    def store_b(p, o, val):
        (br_ref, bi_ref)[p][pl.ds(o, FFT_J), :] = val

    def stage_a(b, carry):
        for ct in range(n_tiles):
            mix(load_in, store_a, pl.multiple_of(b * FFT_C, FFT_C) + ct * FFT_J, a_rows, ka_ref[...])
        return carry

    lax.fori_loop(0, FFT_R, stage_a, 0)

    def stage_b(a1, carry):
        k = kb_ref[a1]
        for ct in range(n_tiles):
            mix(load_a, store_b, pl.multiple_of(a1 * a_rows, a_rows) + ct * FFT_J, FFT_C, k)
        return carry

    lax.fori_loop(0, FFT_R, stage_b, 0)

    def stage_c(t1, carry):
        rows = pl.ds(pl.multiple_of((t1 & (FFT_R - 1)) * a_rows + lax.shift_right_logical(t1, 3) * FFT_C, FFT_C),
                     FFT_C)
        z = jnp.concatenate([br_ref[rows, :], bi_ref[rows, :]], axis=0)
        y = _dot(fc_ref[t1], z)
        for j in range(FFT_CB // LANES):
            o_ref[0, j, pl.ds(t1, FFT_C, stride=FFT_R * FFT_R), :] = y[:, j * LANES:(j + 1) * LANES]
        return carry

    lax.fori_loop(0, FFT_R * FFT_R, stage_c, 0, unroll=DFT_UNROLL)


def _seq_dft(zr, zi, k_a, k_b, f_c):
    n_batch, n_blocks, seq, _ = zr.shape
    blk = pl.BlockSpec((1, 1, seq, FFT_CB), lambda b, j: (b, j, 0, 0), pipeline_mode=pl.Buffered(1))
    return pl.pallas_call(
        _seq_dft_kernel,
        out_shape=jax.ShapeDtypeStruct((n_batch, D_MODEL // LANES, seq, LANES), F32),
        grid=(n_batch, n_blocks),
        in_specs=[blk, blk, _const_spec(k_a.shape), _const_spec(k_b.shape), _const_spec(f_c.shape)],
        out_specs=pl.BlockSpec((1, FFT_CB // LANES, seq, LANES), lambda b, j: (b, j, 0, 0)),
        scratch_shapes=[pltpu.VMEM((seq, FFT_CB), BF16)] * 4,
        compiler_params=_params("parallel", "parallel"),
        name="fourier_seq_dft",
    )(zr, zi, k_a, k_b, f_c)


def _ctx_dft_kernel(zr_ref, zi_ref, f_ref, o_ref):
    o_ref[0] = _dot(f_ref[...], jnp.concatenate([zr_ref[0], zi_ref[0]], axis=0))


def _ctx_dft(zr, zi, f_ctx):
    n_batch = zr.shape[0]
    blk = pl.BlockSpec((1, CTX_LEN, D_MODEL), lambda b: (b, 0, 0))
    return pl.pallas_call(
        _ctx_dft_kernel,
        out_shape=jax.ShapeDtypeStruct((n_batch, CTX_LEN, D_MODEL), F32),
        grid=(n_batch,),
        in_specs=[blk, blk, _const_spec((CTX_LEN, 2 * CTX_LEN))],
        out_specs=blk,
        compiler_params=_params("parallel"),
        name="fourier_ctx_dft",
    )(zr, zi, f_ctx)


def _mlstm_layer(xs, mod_l, w_in, w_conv, b_conv, w_q, w_k, w_v, w_gate, b_gate, gn_w, skip, w_out, pg, pb):
    xm, z = _in_proj(xs, mod_l, w_in.astype(BF16))
    wg = w_gate.reshape(3, M_HEADS, M_HEAD_DIM, N_GATES).transpose(1, 0, 3, 2)
    w_qt = jnp.swapaxes(w_q, 1, 2).astype(BF16)
    w_vt = jnp.swapaxes(w_v, 1, 2).astype(BF16)
    xc, q_t, k, v_t, pre_t = _conv_qkv(
        xm, w_conv.reshape(CONV_K * CONV_K, D_INNER), b_conv.reshape(1, D_INNER),
        w_qt, w_k.astype(BF16), w_vt, wg.astype(BF16), b_gate.reshape(N_GATES, 1))
    hf, hb = _mlstm_scan(q_t, k, v_t, _gate_scans(pre_t))
    return _mlstm_readout(hf, hb, xc, z, xs, mod_l, gn_w.reshape(1, D_INNER), skip.reshape(1, D_INNER),
                          w_out.astype(BF16), pg, pb)


def _fourier_layer(xs, mod_l, w_out, pg, pb, need_ctx, tables):
    f_ch, k_a, k_b, f_c, f_ctx = tables
    z = _channel_dft(xs, mod_l, f_ch, need_ctx)
    mixed_x = _seq_dft(z[0], z[1], k_a, k_b, f_c)
    mixed_c = _ctx_dft(z[2], z[3], f_ctx) if need_ctx else None
    return _fourier_out(mixed_x, mixed_c, xs, mod_l, w_out.astype(BF16), pg, pb)


def kernel(x, c, ctx, c_ctx, ada_w, ada_b, post_g, post_b, m_w_in, m_w_conv, m_b_conv, m_w_q, m_w_k, m_w_v, m_w_gate, m_b_gate, m_gn_w, m_skip, m_w_out, f_w_out, mlp_w1, mlp_w2):
    n_batch = x.shape[0]
    assert n_batch < MOD_ROWS and x.shape[1] == FFT_R * FFT_R * FFT_C and ctx.shape[1] == CTX_LEN == TM
    xs = (ctx, x)
    s_in = jnp.concatenate([c, c_ctx[None, :], jnp.zeros((MOD_ROWS - n_batch - 1, D_MODEL), F32)], axis=0)
    mod = _ada_table(s_in, ada_w, ada_b)
    tables = _dft_tables()
    for i in range(DEPTH):
        is_mlstm = i % N_MIXERS == 0
        j = i // N_MIXERS
        need_ctx = i < DEPTH - 1
        mod_l = mod[i].reshape(MOD_ROWS, 1, 6 * D_MODEL)
        pg = post_g[i].reshape(2, 1, D_MODEL)
        pb = post_b[i].reshape(2, 1, D_MODEL)
        if is_mlstm:
            xs = _mlstm_layer(xs, mod_l, m_w_in[j], m_w_conv[j], m_b_conv[j], m_w_q[j], m_w_k[j], m_w_v[j],
                              m_w_gate[j], m_b_gate[j], m_gn_w[j], m_skip[j], m_w_out[j], pg[0], pb[0])
            if not need_ctx:
                xs = xs[:, CTX_LEN:]
        else:
            xs = _fourier_layer(xs, mod_l, f_w_out[j], pg[0], pb[0], need_ctx, tables)
        xs = _mlp(xs, mod_l, mlp_w1[i].astype(BF16), mlp_w2[i].astype(BF16), pg[1], pb[1], need_ctx)
    return xs
```

```python
import functools
import math

import numpy as np
import jax
import jax.numpy as jnp
from jax import lax
from jax.experimental import pallas as pl
from jax.experimental.pallas import tpu as pltpu

D_MODEL = 1024
DEPTH = 4
GRID_W = 64
CTX_LEN = 256
N_MIXERS = 2
D_INNER = 2 * D_MODEL
M_HEADS = 4
M_HEAD_DIM = D_INNER // M_HEADS
CONV_K = 3
F_GROUPS = 4
F_GROUP_DIM = D_MODEL // F_GROUPS
D_FF = 4 * D_MODEL
ALPHA = float((2 * DEPTH) ** 0.25)
LN_EPS = 1e-5

TM = 256
N_GATES = 4 * M_HEADS
GATE_ROWS = 8
MOD_ROWS = 8
FFT_R = 8
FFT_C = 128
FFT_J = 16
FFT_CB = F_GROUP_DIM
LANES = 128
GATE_CHUNKS = 3
N_ROWS = 16
STATE_ROWS = N_ROWS + M_HEAD_DIM
MXU_WIDTH = 256
SCAN_HEADS = 2
DFT_UNROLL = 8
VMEM_LIMIT = 52 * 1024 * 1024

F32 = jnp.float32
BF16 = jnp.bfloat16


def _params(*sem):
    return pltpu.CompilerParams(dimension_semantics=sem, vmem_limit_bytes=VMEM_LIMIT)


def _ln(x):
    mu = jnp.mean(x, axis=-1, keepdims=True)
    xc = x - mu
    var = jnp.mean(xc * xc, axis=-1, keepdims=True)
    return xc * lax.rsqrt(var + LN_EPS)


def _silu(x):
    return x / (1.0 + jnp.exp(-x))


def _dot(a, b):
    return jnp.dot(a, b, preferred_element_type=F32)


def _dot_nt(a, b):
    return lax.dot_general(a, b, (((1,), (1,)), ((), ())), preferred_element_type=F32)


def _mod_chunk(m, j):
    return m[:, j * D_MODEL:(j + 1) * D_MODEL]


def _const_spec(shape):
    nd = len(shape)
    return pl.BlockSpec(shape, lambda *_: (0,) * nd, pipeline_mode=pl.Buffered(1))


def _mod_spec(n_batch, ctx_tile):
    if ctx_tile:
        return pl.BlockSpec((1, 1, 6 * D_MODEL), lambda b, t: (jnp.where(t == 0, n_batch, b), 0, 0))
    return pl.BlockSpec((1, 1, 6 * D_MODEL), lambda b, t: (b, 0, 0))


def _tile_spec(width, off=0):
    return pl.BlockSpec((1, TM, width), lambda b, t: (b, t + off, 0))


def _stream_specs(stream, n_sub):
    split = isinstance(stream, tuple)
    off = CTX_LEN // TM if split else 0

    def tile_map(r):
        return lambda b, t: (b, jnp.maximum(n_sub * t + r - off, 0), 0)

    specs = [pl.BlockSpec((1, TM, D_MODEL), tile_map(r)) for r in range(n_sub)]
    if split:
        return [pl.BlockSpec((1, CTX_LEN, D_MODEL), lambda b, t: (b, 0, 0))] + specs, [stream[0]] + [stream[1]] * n_sub
    return specs, [stream] * n_sub


def _stream_shape(stream):
    if isinstance(stream, tuple):
        return stream[1].shape[0], stream[0].shape[1] + stream[1].shape[1]
    return stream.shape[0], stream.shape[1]


def _stream_tile(refs, n_sub, r):
    if len(refs) > n_sub:
        x = refs[1 + r][0]
        return jnp.where(pl.program_id(1) == 0, refs[0][0], x) if r == 0 else x
    return refs[r][0]


def _ada_kernel(s_ref, w_ref, b_ref, o_ref):
    s = _silu(s_ref[...])
    o_ref[0] = jnp.dot(s, w_ref[0], preferred_element_type=F32, precision=lax.Precision.HIGHEST) + b_ref[0]


def _ada_table(s_in, ada_w, ada_b):
    n_chunks = 6
    return pl.pallas_call(
        _ada_kernel,
        out_shape=jax.ShapeDtypeStruct((DEPTH, MOD_ROWS, 6 * D_MODEL), F32),
        grid=(DEPTH, n_chunks),
        in_specs=[pl.BlockSpec((MOD_ROWS, D_MODEL), lambda l, j: (0, 0)),
                  pl.BlockSpec((1, D_MODEL, D_MODEL), lambda l, j: (l, 0, j)),
                  pl.BlockSpec((1, 1, D_MODEL), lambda l, j: (l, 0, j))],
        out_specs=pl.BlockSpec((1, MOD_ROWS, D_MODEL), lambda l, j: (l, 0, j)),
        compiler_params=_params("parallel", "parallel"),
        name="ada_table",
    )(s_in, ada_w, ada_b.reshape(DEPTH, 1, 6 * D_MODEL))


def _in_kernel(n_stream, n_sub, ctx_row, *refs):
    mod_ref, w_ref, xm_ref, z_ref = refs[n_stream:]
    for r in range(n_sub):
        rs = slice(r * TM, (r + 1) * TM)
        x = _stream_tile(refs[:n_stream], n_sub, r)
        m = _tile_mod(mod_ref, ctx_row, r)
        h = (_ln(x) * (1.0 + _mod_chunk(m, 1)) + _mod_chunk(m, 0)).astype(BF16)
        for j in range(2):
            sl = slice(j * D_MODEL, (j + 1) * D_MODEL)
            xm_ref[0, rs, sl] = _dot(h, w_ref[:, sl]).astype(BF16)
            z_ref[0, rs, sl] = _dot(h, w_ref[:, D_INNER + j * D_MODEL:D_INNER + (j + 1) * D_MODEL]).astype(BF16)


def _in_proj(stream, mod_l, w_in):
    n_batch, rows = _stream_shape(stream)
    n_sub = _tiles_per_step(rows)
    specs, args = _stream_specs(stream, n_sub)
    out = jax.ShapeDtypeStruct((n_batch, rows, D_INNER), BF16)
    out_spec = pl.BlockSpec((1, n_sub * TM, D_INNER), lambda b, t: (b, t, 0))
    return pl.pallas_call(
        functools.partial(_in_kernel, len(args), n_sub, n_batch),
        out_shape=(out, out),
        grid=(n_batch, rows // (n_sub * TM)),
        in_specs=specs + [_const_spec((MOD_ROWS, 1, 6 * D_MODEL)), _const_spec((D_MODEL, 2 * D_INNER))],
        out_specs=(out_spec, out_spec),
        compiler_params=_params("parallel", "parallel"),
        name="mlstm_in_proj",
    )(*args, mod_l, w_in)


def _conv_qkv_kernel(n_sub, xm_ref, prev_ref, next_ref, wc_ref, bc_ref, wq_ref, wk_ref, wv_ref, wg_ref, bg_ref,
                     xc_ref, q_ref, k_ref, v_ref, pre_ref):
    n_t = pl.num_programs(1) * n_sub
    row = lax.broadcasted_iota(jnp.int32, (TM, 1), 0)
    for r in range(n_sub):
        t = pl.program_id(1) * n_sub + r
        rs = slice(r * TM, (r + 1) * TM)
        is_ctx = t == 0
        period = jnp.where(is_ctx, CTX_LEN - 1, GRID_W - 1)
        pos = row & period
        has_left = pos != 0
        has_right = pos != period
        prev_ok = t >= 2
        next_ok = jnp.logical_and(t >= 1, t <= n_t - 2)
        row_w = jnp.where(is_ctx, 0.0, 1.0)
        pre_t = jnp.zeros((N_GATES, TM), F32) + bg_ref[...]
        for h in range(M_HEADS):
            sl = slice(h * M_HEAD_DIM, (h + 1) * M_HEAD_DIM)
            xm = xm_ref[0, rs, sl]
            prev = prev_ref[0, :, sl] if r == 0 else xm_ref[0, r * TM - GRID_W:r * TM, sl]
            nxt = next_ref[0, :, sl] if r == n_sub - 1 else xm_ref[0, (r + 1) * TM:(r + 1) * TM + GRID_W, sl]
            e = jnp.concatenate([jnp.where(prev_ok, prev, 0).astype(F32),
                                 xm.astype(F32),
                                 jnp.where(next_ok, nxt, 0).astype(F32)], axis=0)
            p = []
            for dc in range(CONV_K):
                acc = None
                for dr in range(CONV_K):
                    w = wc_ref[CONV_K * dr + dc:CONV_K * dr + dc + 1, sl]
                    if dr != CONV_K // 2:
                        w = w * row_w
                    term = w * e[dr * GRID_W:dr * GRID_W + TM]
                    acc = term if acc is None else acc + term
                p.append(acc)
            y = (p[1] + bc_ref[:, sl]
                 + jnp.where(has_left, pltpu.roll(p[0], 1, axis=0), 0.0)
                 + jnp.where(has_right, pltpu.roll(p[2], TM - 1, axis=0), 0.0))
            xc = _silu(y).astype(BF16)
            xc_ref[0, rs, sl] = xc
            q_t = _dot_nt(wq_ref[h], xc).astype(BF16)
            k = (_dot(xc, wk_ref[h]) * (M_HEAD_DIM ** -0.5)).astype(BF16)
            v_t = _dot_nt(wv_ref[h], xm).astype(BF16)
            q_ref[0, r, sl, :] = q_t
            k_ref[0, rs, sl] = k
            v_ref[0, r, sl, :] = v_t
            pre_t = pre_t + _dot(wg_ref[h, 0], q_t) + _dot_nt(wg_ref[h, 1], k) + _dot(wg_ref[h, 2], v_t)
        pre_ref[0, :, rs] = pre_t


def _conv_qkv(xm, w_conv, b_conv, w_q, w_k, w_v, w_g, b_g):
    n_batch, rows, _ = xm.shape
    n_sub = _tiles_per_step(rows)
    n_halo = rows // GRID_W
    per = n_sub * TM // GRID_W
    wide = jax.ShapeDtypeStruct((n_batch, rows, D_INNER), BF16)
    wide_t = jax.ShapeDtypeStruct((n_batch, rows // TM, D_INNER, TM), BF16)
    row_spec = pl.BlockSpec((1, n_sub * TM, D_INNER), lambda b, t: (b, t, 0))
    col_spec = pl.BlockSpec((1, n_sub, D_INNER, TM), lambda b, t: (b, t, 0, 0))
    return pl.pallas_call(
        functools.partial(_conv_qkv_kernel, n_sub),
        out_shape=(wide, wide_t, wide, wide_t, jax.ShapeDtypeStruct((n_batch, N_GATES, rows), F32)),
        grid=(n_batch, rows // (n_sub * TM)),
        in_specs=[row_spec,
                  pl.BlockSpec((1, GRID_W, D_INNER), lambda b, t: (b, jnp.maximum(per * t - 1, 0), 0)),
                  pl.BlockSpec((1, GRID_W, D_INNER), lambda b, t: (b, jnp.minimum(per * t + per, n_halo - 1), 0)),
                  _const_spec((CONV_K * CONV_K, D_INNER)),
                  _const_spec((1, D_INNER)),
                  _const_spec((M_HEADS, M_HEAD_DIM, M_HEAD_DIM)),
                  _const_spec((M_HEADS, M_HEAD_DIM, M_HEAD_DIM)),
                  _const_spec((M_HEADS, M_HEAD_DIM, M_HEAD_DIM)),
                  _const_spec((M_HEADS, 3, N_GATES, M_HEAD_DIM)),
                  _const_spec((N_GATES, 1))],
        out_specs=(row_spec, col_spec, row_spec, col_spec,
                   pl.BlockSpec((1, N_GATES, n_sub * TM), lambda b, t: (b, 0, t))),
        compiler_params=_params("parallel", "parallel"),
        name="mlstm_conv_qkv",
    )(xm, xm, xm, w_conv, b_conv, w_q, w_k, w_v, w_g, b_g)


def _scan_lanes(x, combine, fill, reverse):
    n = x.shape[-1]
    lane = lax.broadcasted_iota(jnp.int32, x.shape, 1)
    s = 1
    while s < n:
        if reverse:
            sh = jnp.where(lane < n - s, pltpu.roll(x, n - s, axis=1), fill)
        else:
            sh = jnp.where(lane >= s, pltpu.roll(x, s, axis=1), fill)
        x = combine(x, sh)
        s *= 2
    return x


def _gate_kernel(pre_ref, o_ref):
    zeros = jnp.zeros((GATE_ROWS - 3, TM), F32)
    for c in range(GATE_CHUNKS):
        lanes = slice(c * TM, (c + 1) * TM)
        for d in range(2):
            blk = pre_ref[0, 2 * M_HEADS * d:2 * M_HEADS * (d + 1), lanes]
            log_f = jnp.minimum(blk, 0.0) - jnp.log1p(jnp.exp(-jnp.abs(blk)))
            b = pltpu.roll(_scan_lanes(log_f, jnp.add, 0.0, reverse=d == 1), M_HEADS, axis=0)
            g = blk - b
            big = _scan_lanes(g, jnp.maximum, -jnp.inf, reverse=d == 1)
            for h in range(M_HEADS):
                o_ref[0, d, h, :, lanes] = jnp.concatenate([b[h:h + 1], g[h:h + 1], big[h:h + 1], zeros], axis=0)


def _gate_scans(pre_t):
    n_batch, _, rows = pre_t.shape
    width = GATE_CHUNKS * TM
    return pl.pallas_call(
        _gate_kernel,
        out_shape=jax.ShapeDtypeStruct((n_batch, 2, M_HEADS, GATE_ROWS, rows), F32),
        grid=(n_batch, rows // width),
        in_specs=[pl.BlockSpec((1, N_GATES, width), lambda b, t: (b, 0, t))],
        out_specs=pl.BlockSpec((1, 2, M_HEADS, GATE_ROWS, width), lambda b, t: (b, 0, 0, 0, t)),
        compiler_params=_params("parallel", "parallel"),
        name="mlstm_gate_scans",
    )(pre_t)


def _scan_chunk(d, hh, q_ref, k_ref, v_ref, gate_ref, h_ref, cn_ref, cnb_ref, m_ref):
    sl = slice(hh * M_HEAD_DIM, (hh + 1) * M_HEAD_DIM)
    q_t = q_ref[0, 0, sl, :]
    k = k_ref[0, :, sl]
    v_t = v_ref[0, 0, sl, :]
    gates = gate_ref[0, 0, hh]
    b_row, g_row, big_row = gates[0:1], gates[1:2], gates[2:3]
    m_prev = m_ref[d, hh, 0:1, 0:1]
    last = TM - 1 if d == 0 else 0
    m_row = jnp.maximum(m_prev, big_row)
    m_last = m_row[:, last:last + 1]
    w_inter = jnp.exp(m_prev - m_row)
    den_floor = jnp.exp(-(b_row + m_row))
    w_key = jnp.exp(g_row - m_last).astype(BF16)
    g_col = jnp.transpose(gates)[:, 1:2]
    jj = lax.broadcasted_iota(jnp.int32, (TM, TM), 0)
    ii = lax.broadcasted_iota(jnp.int32, (TM, TM), 1)
    visible = jj <= ii if d == 0 else jj >= ii
    pad = jnp.zeros((N_ROWS - 1, TM), BF16)

    s_t = _dot(k, q_t) * jnp.where(visible, jnp.exp(g_col - m_row), 0.0)
    inter = _dot(cnb_ref[d, hh], q_t)
    v_one = jnp.concatenate([jnp.ones((1, TM), BF16), pad, v_t], axis=0)
    intra = _dot(v_one, s_t.astype(BF16))
    den = w_inter * inter[0:1] + intra[0:1]
    scale = 1.0 / jnp.maximum(jnp.abs(den), den_floor)
    h_t = (w_inter * scale) * inter[N_ROWS:] + scale * intra[N_ROWS:]
    h_ref[0, :, sl] = jnp.transpose(h_t).astype(BF16)

    decay = jnp.exp(m_prev - m_last)
    vw = jnp.concatenate([w_key, pad, v_t * w_key], axis=0)
    for c0 in range(0, M_HEAD_DIM, MXU_WIDTH):
        cn = decay * cn_ref[d, hh, :, c0:c0 + MXU_WIDTH] + _dot(vw, k[:, c0:c0 + MXU_WIDTH])
        cn_ref[d, hh, :, c0:c0 + MXU_WIDTH] = cn
        cnb_ref[d, hh, :, c0:c0 + MXU_WIDTH] = cn.astype(BF16)
    m_ref[d, hh] = jnp.zeros(m_ref.shape[2:], F32) + (b_row[:, last:last + 1] + m_last)


def _scan_kernel(qf_ref, kf_ref, vf_ref, gf_ref, qb_ref, kb_ref, vb_ref, gb_ref, hf_ref, hb_ref,
                 cn_ref, cnb_ref, m_ref):
    @pl.when(pl.program_id(2) == 0)
    def _():
        cn_ref[...] = jnp.zeros_like(cn_ref)
        cnb_ref[...] = jnp.zeros_like(cnb_ref)
        m_ref[...] = jnp.zeros_like(m_ref)

    for hh in range(SCAN_HEADS):
        _scan_chunk(0, hh, qf_ref, kf_ref, vf_ref, gf_ref, hf_ref, cn_ref, cnb_ref, m_ref)
        _scan_chunk(1, hh, qb_ref, kb_ref, vb_ref, gb_ref, hb_ref, cn_ref, cnb_ref, m_ref)


def _mlstm_scan(q_t, k, v_t, gates):
    n_batch, rows, _ = k.shape
    n_t = rows // TM
    width = SCAN_HEADS * M_HEAD_DIM

    def bwd_tile(c):
        return jnp.where(c == 0, 0, n_t - c)

    kf_spec = pl.BlockSpec((1, TM, width), lambda b, h, c: (b, c, h))
    kb_spec = pl.BlockSpec((1, TM, width), lambda b, h, c: (b, bwd_tile(c), h))
    tf_spec = pl.BlockSpec((1, 1, width, TM), lambda b, h, c: (b, c, h, 0))
    tb_spec = pl.BlockSpec((1, 1, width, TM), lambda b, h, c: (b, bwd_tile(c), h, 0))
    gf_spec = pl.BlockSpec((1, 1, SCAN_HEADS, GATE_ROWS, TM), lambda b, h, c: (b, 0, h, 0, c))
    gb_spec = pl.BlockSpec((1, 1, SCAN_HEADS, GATE_ROWS, TM), lambda b, h, c: (b, 1, h, 0, bwd_tile(c)))
    out = jax.ShapeDtypeStruct((n_batch, rows, D_INNER), BF16)
    return pl.pallas_call(
        _scan_kernel,
        out_shape=(out, out),
        grid=(n_batch, M_HEADS // SCAN_HEADS, n_t),
        in_specs=[tf_spec, kf_spec, tf_spec, gf_spec, tb_spec, kb_spec, tb_spec, gb_spec],
        out_specs=(kf_spec, kb_spec),
        scratch_shapes=[pltpu.VMEM((2, SCAN_HEADS, STATE_ROWS, M_HEAD_DIM), F32),
                        pltpu.VMEM((2, SCAN_HEADS, STATE_ROWS, M_HEAD_DIM), BF16),
                        pltpu.VMEM((2, SCAN_HEADS, 8, 128), F32)],
        compiler_params=_params("parallel", "parallel", "arbitrary"),
        name="mlstm_scan",
    )(q_t, k, v_t, gates, q_t, k, v_t, gates)


def _post_norm(x, y, gate, pg, pb):
    return _ln(ALPHA * x + gate * y) * pg + pb


def _readout_kernel(n_stream, n_sub, ctx_row, hf_ref, hb_ref, xc_ref, z_ref, *refs):
    mod_ref, gn_ref, sk_ref, w_ref, pg_ref, pb_ref, o_ref = refs[n_stream:]
    for r in range(n_sub):
        rs = slice(r * TM, (r + 1) * TM)
        parts = []
        for h in range(M_HEADS):
            sl = slice(h * M_HEAD_DIM, (h + 1) * M_HEAD_DIM)
            hn = _ln(hf_ref[0, rs, sl].astype(F32) + hb_ref[0, rs, sl].astype(F32))
            a = hn * gn_ref[:, sl] + sk_ref[:, sl] * xc_ref[0, rs, sl].astype(F32)
            parts.append((a * _silu(z_ref[0, rs, sl].astype(F32))).astype(BF16))
        y = _dot(jnp.concatenate(parts, axis=1), w_ref[...])
        m = _tile_mod(mod_ref, ctx_row, r)
        o_ref[0, rs, :] = _post_norm(_stream_tile(refs[:n_stream], n_sub, r), y, _mod_chunk(m, 2),
                                     pg_ref[...], pb_ref[...])


def _mlstm_readout(hf, hb, xc, z, stream, mod_l, gn_w, skip, w_out, pg, pb):
    n_batch, rows = _stream_shape(stream)
    n_sub = _tiles_per_step(rows)
    specs, args = _stream_specs(stream, n_sub)
    wide = pl.BlockSpec((1, n_sub * TM, D_INNER), lambda b, t: (b, t, 0))
    return pl.pallas_call(
        functools.partial(_readout_kernel, len(args), n_sub, n_batch),
        out_shape=jax.ShapeDtypeStruct((n_batch, rows, D_MODEL), F32),
        grid=(n_batch, rows // (n_sub * TM)),
        in_specs=[wide, wide, wide, wide] + specs
        + [_const_spec((MOD_ROWS, 1, 6 * D_MODEL)),
           _const_spec((1, D_INNER)), _const_spec((1, D_INNER)), _const_spec((D_INNER, D_MODEL)),
           _const_spec((1, D_MODEL)), _const_spec((1, D_MODEL))],
        out_specs=pl.BlockSpec((1, n_sub * TM, D_MODEL), lambda b, t: (b, t, 0)),
        compiler_params=_params("parallel", "parallel"),
        name="mlstm_readout",
    )(hf, hb, xc, z, *args, mod_l, gn_w, skip, w_out, pg, pb)


def _fourier_out_kernel(n_sub, ctx_row, *refs):
    mx_refs, refs = refs[:n_sub], refs[n_sub:]
    if ctx_row is not None:
        mc_ref, refs = refs[0], refs[1:]
    x_refs, (mod_ref, w_ref, pg_ref, pb_ref, o_ref) = refs[:n_sub], refs[n_sub:]
    for r in range(n_sub):
        a = jnp.concatenate([mx_refs[r][0, j] for j in range(D_MODEL // LANES)], axis=1)
        if ctx_row is not None and r == 0:
            a = jnp.where(pl.program_id(1) == 0, mc_ref[0], a)
        y = _dot(a.astype(BF16), w_ref[...])
        m = _tile_mod(mod_ref, ctx_row, r)
        o_ref[0, r * TM:(r + 1) * TM, :] = _post_norm(x_refs[r][0], y, _mod_chunk(m, 2), pg_ref[...], pb_ref[...])


def _fourier_out(mixed_x, mixed_c, xs, mod_l, w_out, pg, pb):
    n_batch = xs.shape[0]
    ctx_tile = mixed_c is not None
    ctx_tiles = CTX_LEN // TM
    n_t = mixed_x.shape[2] // TM + (ctx_tiles if ctx_tile else 0)
    n_sub = _tiles_per_step(n_t * TM)
    mixed_off = ctx_tiles if ctx_tile else 0
    x_off = 0 if ctx_tile else ctx_tiles

    def mixed_map(r):
        return lambda b, t: (b, 0, jnp.maximum(n_sub * t + r - mixed_off, 0), 0)

    def x_map(r):
        return lambda b, t: (b, n_sub * t + r + x_off, 0)

    in_specs = [pl.BlockSpec((1, D_MODEL // LANES, TM, LANES), mixed_map(r)) for r in range(n_sub)]
    args = [mixed_x] * n_sub
    if ctx_tile:
        in_specs.append(pl.BlockSpec((1, CTX_LEN, D_MODEL), lambda b, t: (b, 0, 0)))
        args.append(mixed_c)
    in_specs += [pl.BlockSpec((1, TM, D_MODEL), x_map(r)) for r in range(n_sub)]
    args += [xs] * n_sub
    in_specs += [_const_spec((MOD_ROWS, 1, 6 * D_MODEL)),
                 _const_spec((D_MODEL, D_MODEL)), _const_spec((1, D_MODEL)), _const_spec((1, D_MODEL))]
    return pl.pallas_call(
        functools.partial(_fourier_out_kernel, n_sub, n_batch if ctx_tile else None),
        out_shape=jax.ShapeDtypeStruct((n_batch, n_t * TM, D_MODEL), F32),
        grid=(n_batch, n_t // n_sub),
        in_specs=in_specs,
        out_specs=pl.BlockSpec((1, n_sub * TM, D_MODEL), lambda b, t: (b, t, 0)),
        compiler_params=_params("parallel", "parallel"),
        name="fourier_out",
    )(*args, mod_l, w_out, pg, pb)


def _tile_mod(mod_ref, ctx_row, r):
    m = mod_ref[pl.program_id(0)]
    if ctx_row is not None and r == 0:
        m = jnp.where(pl.program_id(1) == 0, mod_ref[ctx_row], m)
    return m


def _tiles_per_step(rows):
    n_t = rows // TM
    return 3 if n_t % 3 == 0 else 2


def _mlp_kernel(n_sub, ctx_row, x_ref, mod_ref, w1_ref, w2_ref, pg_ref, pb_ref, o_ref):
    for r in range(n_sub):
        rs = slice(r * TM, (r + 1) * TM)
        x = x_ref[0, rs, :]
        m = _tile_mod(mod_ref, ctx_row, r)
        h = (_ln(x) * (1.0 + _mod_chunk(m, 4)) + _mod_chunk(m, 3)).astype(BF16)
        acc = jnp.zeros((TM, D_MODEL), F32)
        for j in range(D_FF // D_MODEL):
            sl = slice(j * D_MODEL, (j + 1) * D_MODEL)
            u = jnp.square(jnp.maximum(_dot(h, w1_ref[:, sl]), 0.0)).astype(BF16)
            acc = acc + _dot(u, w2_ref[sl, :])
        o_ref[0, rs, :] = _post_norm(x, acc, _mod_chunk(m, 5), pg_ref[...], pb_ref[...])


def _mlp(xs, mod_l, w1, w2, pg, pb, ctx_tile):
    n_batch, rows, _ = xs.shape
    n_sub = _tiles_per_step(rows)
    blk = pl.BlockSpec((1, n_sub * TM, D_MODEL), lambda b, t: (b, t, 0))
    return pl.pallas_call(
        functools.partial(_mlp_kernel, n_sub, n_batch if ctx_tile else None),
        out_shape=jax.ShapeDtypeStruct((n_batch, rows, D_MODEL), F32),
        grid=(n_batch, rows // (n_sub * TM)),
        in_specs=[blk, _const_spec((MOD_ROWS, 1, 6 * D_MODEL)),
                  _const_spec((D_MODEL, D_FF)), _const_spec((D_FF, D_MODEL)),
                  _const_spec((1, D_MODEL)), _const_spec((1, D_MODEL))],
        out_specs=blk,
        compiler_params=_params("parallel", "parallel"),
        name="mlp",
    )(xs, mod_l, w1, w2, pg, pb)


def _dft_tables():
    two_pi = 2.0 * np.pi
    c = np.arange(F_GROUP_DIM)
    ang = two_pi * ((c[:, None] * c[None, :]) % F_GROUP_DIM) / F_GROUP_DIM
    s_ch = 1.0 / math.sqrt(F_GROUP_DIM)
    f_ch = np.concatenate([np.cos(ang) * s_ch, -np.sin(ang) * s_ch], axis=1)

    seq = FFT_R * FFT_R * FFT_C
    r = np.arange(FFT_R)
    eye = np.eye(FFT_J)

    def tile_mix(w):
        wr, wi = np.kron(w.real, eye), np.kron(w.imag, eye)
        return np.block([[wr, -wi], [wi, wr]])

    w8 = np.exp(-2j * np.pi * ((r[:, None] * r[None, :]) % FFT_R) / FFT_R) / math.sqrt(FFT_R)
    k_a = tile_mix(w8)
    k_b = np.stack([tile_mix(w8 * np.exp(-2j * np.pi * (r[None, :] * a1) / (FFT_R * FFT_R)))
                    for a1 in range(FFT_R)])
    c = np.arange(FFT_C)
    t1 = np.arange(FFT_R * FFT_R)
    num = (c[None, :, None] * c[None, None, :] * (FFT_R * FFT_R) + t1[:, None, None] * c[None, None, :]) % seq
    ang = two_pi * num / seq
    f_c = np.concatenate([np.cos(ang), np.sin(ang)], axis=2) / math.sqrt(FFT_C)

    p = np.arange(CTX_LEN)
    ang = two_pi * ((p[:, None] * p[None, :]) % CTX_LEN) / CTX_LEN
    sc = 1.0 / math.sqrt(CTX_LEN)
    f_ctx = np.concatenate([np.cos(ang) * sc, np.sin(ang) * sc], axis=1)
    return tuple(jnp.asarray(a, dtype=F32).astype(BF16) for a in (f_ch, k_a, k_b, f_c, f_ctx))


def _cdft_kernel(ctx_tile, *refs):
    if ctx_tile:
        x_ref, mod_ref, f_ref, zrx_ref, zix_ref, zrc_ref, zic_ref = refs
    else:
        x_ref, mod_ref, f_ref, zrx_ref, zix_ref = refs
    m = mod_ref[0]
    h = (_ln(x_ref[0]) * (1.0 + _mod_chunk(m, 1)) + _mod_chunk(m, 0)).astype(BF16)
    f = f_ref[...]
    zr, zi = [], []
    for g in range(F_GROUPS):
        z = _dot(h[:, g * F_GROUP_DIM:(g + 1) * F_GROUP_DIM], f).astype(BF16)
        zr.append(z[:, :F_GROUP_DIM])
        zi.append(z[:, F_GROUP_DIM:])
    def store_latent():
        for g in range(F_GROUPS):
            zrx_ref[0, g] = zr[g]
            zix_ref[0, g] = zi[g]

    if ctx_tile:
        t = pl.program_id(1)

        @pl.when(t == 0)
        def _():
            zrc_ref[0] = jnp.concatenate(zr, axis=1)
            zic_ref[0] = jnp.concatenate(zi, axis=1)

        pl.when(t > 0)(store_latent)
    else:
        store_latent()


def _channel_dft(xs, mod_l, f_ch, ctx_tile):
    n_batch, rows, _ = xs.shape
    seq = rows - CTX_LEN
    off = CTX_LEN // TM
    zx = jax.ShapeDtypeStruct((n_batch, F_GROUPS, seq, F_GROUP_DIM), BF16)
    zc = jax.ShapeDtypeStruct((n_batch, CTX_LEN, D_MODEL), BF16)
    if ctx_tile:
        n_t = rows // TM
        x_out = pl.BlockSpec((1, F_GROUPS, TM, F_GROUP_DIM), lambda b, t: (b, 0, jnp.maximum(t - off, 0), 0))
        c_out = pl.BlockSpec((1, CTX_LEN, D_MODEL), lambda b, t: (b, 0, 0))
        out_shape, out_specs, x_in = (zx, zx, zc, zc), (x_out, x_out, c_out, c_out), _tile_spec(D_MODEL)
    else:
        n_t = seq // TM
        x_out = pl.BlockSpec((1, F_GROUPS, TM, F_GROUP_DIM), lambda b, t: (b, 0, t, 0))
        out_shape, out_specs, x_in = (zx, zx), (x_out, x_out), _tile_spec(D_MODEL, off)
    return pl.pallas_call(
        functools.partial(_cdft_kernel, ctx_tile),
        out_shape=out_shape,
        grid=(n_batch, n_t),
        in_specs=[x_in, _mod_spec(n_batch, ctx_tile), _const_spec((F_GROUP_DIM, 2 * F_GROUP_DIM))],
        out_specs=out_specs,
        compiler_params=_params("parallel", "arbitrary"),
        name="fourier_channel_dft",
    )(xs, mod_l, f_ch)


def _seq_dft_kernel(zr_ref, zi_ref, ka_ref, kb_ref, fc_ref, o_ref, ar_ref, ai_ref, br_ref, bi_ref):
    a_rows = FFT_R * FFT_C
    n_tiles = FFT_C // FFT_J

    def mix(load, store, base, step, k):
        offs = [base + i * step for i in range(FFT_R)]
        z = jnp.concatenate([load(0, o) for o in offs] + [load(1, o) for o in offs], axis=0)
        u = _dot(k, z).astype(BF16)
        for i, o in enumerate(offs):
            store(0, o, u[i * FFT_J:(i + 1) * FFT_J])
            store(1, o, u[(FFT_R + i) * FFT_J:(FFT_R + i + 1) * FFT_J])

    def load_in(p, o):
        return (zr_ref, zi_ref)[p][0, 0, pl.ds(o, FFT_J), :]

    def load_a(p, o):
        return (ar_ref, ai_ref)[p][pl.ds(o, FFT_J), :]

    def store_a(p, o, val):
        (ar_ref, ai_ref)[p][pl.ds(o, FFT_J), :] = val

    def store_b(p, o, val):
        (br_ref, bi_ref)[p][pl.ds(o, FFT_J), :] = val

    def stage_a(b, carry):
        for ct in range(n_tiles):
            mix(load_in, store_a, pl.multiple_of(b * FFT_C, FFT_C) + ct * FFT_J, a_rows, ka_ref[...])
        return carry

    lax.fori_loop(0, FFT_R, stage_a, 0)

    def stage_b(a1, carry):
        k = kb_ref[a1]
        for ct in range(n_tiles):
            mix(load_a, store_b, pl.multiple_of(a1 * a_rows, a_rows) + ct * FFT_J, FFT_C, k)
        return carry

    lax.fori_loop(0, FFT_R, stage_b, 0)

    def stage_c(t1, carry):
        rows = pl.ds(pl.multiple_of((t1 & (FFT_R - 1)) * a_rows + lax.shift_right_logical(t1, 3) * FFT_C, FFT_C),
                     FFT_C)
        z = jnp.concatenate([br_ref[rows, :], bi_ref[rows, :]], axis=0)
        y = _dot(fc_ref[t1], z)
        for j in range(FFT_CB // LANES):
            o_ref[0, j, pl.ds(t1, FFT_C, stride=FFT_R * FFT_R), :] = y[:, j * LANES:(j + 1) * LANES]
        return carry

    lax.fori_loop(0, FFT_R * FFT_R, stage_c, 0, unroll=DFT_UNROLL)


def _seq_dft(zr, zi, k_a, k_b, f_c):
    n_batch, n_blocks, seq, _ = zr.shape
    blk = pl.BlockSpec((1, 1, seq, FFT_CB), lambda b, j: (b, j, 0, 0), pipeline_mode=pl.Buffered(1))
    return pl.pallas_call(
        _seq_dft_kernel,
        out_shape=jax.ShapeDtypeStruct((n_batch, D_MODEL // LANES, seq, LANES), F32),
        grid=(n_batch, n_blocks),
        in_specs=[blk, blk, _const_spec(k_a.shape), _const_spec(k_b.shape), _const_spec(f_c.shape)],
        out_specs=pl.BlockSpec((1, FFT_CB // LANES, seq, LANES), lambda b, j: (b, j, 0, 0)),
        scratch_shapes=[pltpu.VMEM((seq, FFT_CB), BF16)] * 4,
        compiler_params=_params("parallel", "parallel"),
        name="fourier_seq_dft",
    )(zr, zi, k_a, k_b, f_c)


def _ctx_dft_kernel(zr_ref, zi_ref, f_ref, o_ref):
    o_ref[0] = _dot(f_ref[...], jnp.concatenate([zr_ref[0], zi_ref[0]], axis=0))


def _ctx_dft(zr, zi, f_ctx):
    n_batch = zr.shape[0]
    blk = pl.BlockSpec((1, CTX_LEN, D_MODEL), lambda b: (b, 0, 0))
    return pl.pallas_call(
        _ctx_dft_kernel,
        out_shape=jax.ShapeDtypeStruct((n_batch, CTX_LEN, D_MODEL), F32),
        grid=(n_batch,),
        in_specs=[blk, blk, _const_spec((CTX_LEN, 2 * CTX_LEN))],
        out_specs=blk,
        compiler_params=_params("parallel"),
        name="fourier_ctx_dft",
    )(zr, zi, f_ctx)


def _mlstm_layer(xs, mod_l, w_in, w_conv, b_conv, w_q, w_k, w_v, w_gate, b_gate, gn_w, skip, w_out, pg, pb):
    xm, z = _in_proj(xs, mod_l, w_in.astype(BF16))
    wg = w_gate.reshape(3, M_HEADS, M_HEAD_DIM, N_GATES).transpose(1, 0, 3, 2)
    w_qt = jnp.swapaxes(w_q, 1, 2).astype(BF16)
    w_vt = jnp.swapaxes(w_v, 1, 2).astype(BF16)
    xc, q_t, k, v_t, pre_t = _conv_qkv(
        xm, w_conv.reshape(CONV_K * CONV_K, D_INNER), b_conv.reshape(1, D_INNER),
        w_qt, w_k.astype(BF16), w_vt, wg.astype(BF16), b_gate.reshape(N_GATES, 1))
    hf, hb = _mlstm_scan(q_t, k, v_t, _gate_scans(pre_t))
    return _mlstm_readout(hf, hb, xc, z, xs, mod_l, gn_w.reshape(1, D_INNER), skip.reshape(1, D_INNER),
                          w_out.astype(BF16), pg, pb)


def _fourier_layer(xs, mod_l, w_out, pg, pb, need_ctx, tables):
    f_ch, k_a, k_b, f_c, f_ctx = tables
    z = _channel_dft(xs, mod_l, f_ch, need_ctx)
    mixed_x = _seq_dft(z[0], z[1], k_a, k_b, f_c)
    mixed_c = _ctx_dft(z[2], z[3], f_ctx) if need_ctx else None
    return _fourier_out(mixed_x, mixed_c, xs, mod_l, w_out.astype(BF16), pg, pb)


def kernel(x, c, ctx, c_ctx, ada_w, ada_b, post_g, post_b, m_w_in, m_w_conv, m_b_conv, m_w_q, m_w_k, m_w_v, m_w_gate, m_b_gate, m_gn_w, m_skip, m_w_out, f_w_out, mlp_w1, mlp_w2):
    n_batch = x.shape[0]
    assert n_batch < MOD_ROWS and x.shape[1] == FFT_R * FFT_R * FFT_C and ctx.shape[1] == CTX_LEN == TM
    xs = (ctx, x)
    s_in = jnp.concatenate([c, c_ctx[None, :], jnp.zeros((MOD_ROWS - n_batch - 1, D_MODEL), F32)], axis=0)
    mod = _ada_table(s_in, ada_w, ada_b)
    tables = _dft_tables()
    for i in range(DEPTH):
        is_mlstm = i % N_MIXERS == 0
        j = i // N_MIXERS
        need_ctx = i < DEPTH - 1
        mod_l = mod[i].reshape(MOD_ROWS, 1, 6 * D_MODEL)
        pg = post_g[i].reshape(2, 1, D_MODEL)
        pb = post_b[i].reshape(2, 1, D_MODEL)
        if is_mlstm:
            xs = _mlstm_layer(xs, mod_l, m_w_in[j], m_w_conv[j], m_b_conv[j], m_w_q[j], m_w_k[j], m_w_v[j],
                              m_w_gate[j], m_b_gate[j], m_gn_w[j], m_skip[j], m_w_out[j], pg[0], pb[0])
            if not need_ctx:
                xs = xs[:, CTX_LEN:]
        else:
            xs = _fourier_layer(xs, mod_l, f_w_out[j], pg[0], pb[0], need_ctx, tables)
        xs = _mlp(xs, mod_l, mlp_w1[i].astype(BF16), mlp_w2[i].astype(BF16), pg[1], pb[1], need_ctx)
    return xs
```

```python
import functools
import math

import numpy as np
import jax
import jax.numpy as jnp
from jax import lax
from jax.experimental import pallas as pl
from jax.experimental.pallas import tpu as pltpu

D_MODEL = 1024
DEPTH = 4
GRID_W = 64
CTX_LEN = 256
N_MIXERS = 2
D_INNER = 2 * D_MODEL
M_HEADS = 4
M_HEAD_DIM = D_INNER // M_HEADS
CONV_K = 3
F_GROUPS = 4
F_GROUP_DIM = D_MODEL // F_GROUPS
D_FF = 4 * D_MODEL
ALPHA = float((2 * DEPTH) ** 0.25)
LN_EPS = 1e-5

TM = 256
N_GATES = 4 * M_HEADS
GATE_ROWS = 8
MOD_ROWS = 8
FFT_R = 8
FFT_C = 128
FFT_J = 16
FFT_CB = F_GROUP_DIM
LANES = 128
GATE_CHUNKS = 3
N_ROWS = 16
STATE_ROWS = N_ROWS + M_HEAD_DIM
MXU_WIDTH = 256
SCAN_HEADS = 2
DFT_UNROLL = 8
VMEM_LIMIT = 52 * 1024 * 1024
SEQ_DFT_VMEM_LIMIT = 58 * 1024 * 1024

F32 = jnp.float32
BF16 = jnp.bfloat16


def _params(*sem, vmem_limit=VMEM_LIMIT):
    return pltpu.CompilerParams(dimension_semantics=sem, vmem_limit_bytes=vmem_limit)


def _ln(x):
    mu = jnp.mean(x, axis=-1, keepdims=True)
    xc = x - mu
    var = jnp.mean(xc * xc, axis=-1, keepdims=True)
    return xc * lax.rsqrt(var + LN_EPS)


def _silu(x):
    t = 0.5 * x
    return t + t * jnp.tanh(t)


def _dot(a, b):
    return jnp.dot(a, b, preferred_element_type=F32)


def _dot_nt(a, b):
    return lax.dot_general(a, b, (((1,), (1,)), ((), ())), preferred_element_type=F32)


def _mod_chunk(m, j):
    return m[:, j * D_MODEL:(j + 1) * D_MODEL]


def _const_spec(shape):
    nd = len(shape)
    return pl.BlockSpec(shape, lambda *_: (0,) * nd, pipeline_mode=pl.Buffered(1))


def _mod_spec(n_batch, ctx_tile):
    if ctx_tile:
        return pl.BlockSpec((1, 1, 6 * D_MODEL), lambda b, t: (jnp.where(t == 0, n_batch, b), 0, 0))
    return pl.BlockSpec((1, 1, 6 * D_MODEL), lambda b, t: (b, 0, 0))


def _tile_spec(width, off=0):
    return pl.BlockSpec((1, TM, width), lambda b, t: (b, t + off, 0))


def _stream_specs(stream, n_sub):
    split = isinstance(stream, tuple)
    off = CTX_LEN // TM if split else 0

    def tile_map(r):
        return lambda b, t: (b, jnp.maximum(n_sub * t + r - off, 0), 0)

    specs = [pl.BlockSpec((1, TM, D_MODEL), tile_map(r)) for r in range(n_sub)]
    if split:
        return [pl.BlockSpec((1, CTX_LEN, D_MODEL), lambda b, t: (b, 0, 0))] + specs, [stream[0]] + [stream[1]] * n_sub
    return specs, [stream] * n_sub


def _stream_shape(stream):
    if isinstance(stream, tuple):
        return stream[1].shape[0], stream[0].shape[1] + stream[1].shape[1]
    return stream.shape[0], stream.shape[1]


def _stream_tile(refs, n_sub, r):
    if len(refs) > n_sub:
        x = refs[1 + r][0]
        return jnp.where(pl.program_id(1) == 0, refs[0][0], x) if r == 0 else x
    return refs[r][0]


def _ada_kernel(s_ref, w_ref, b_ref, o_ref):
    s = _silu(s_ref[...])
    o_ref[0] = jnp.dot(s, w_ref[0], preferred_element_type=F32, precision=lax.Precision.HIGHEST) + b_ref[0]


def _ada_table(s_in, ada_w, ada_b):
    n_chunks = 6
    return pl.pallas_call(
        _ada_kernel,
        out_shape=jax.ShapeDtypeStruct((DEPTH, MOD_ROWS, 6 * D_MODEL), F32),
        grid=(DEPTH, n_chunks),
        in_specs=[pl.BlockSpec((MOD_ROWS, D_MODEL), lambda l, j: (0, 0)),
                  pl.BlockSpec((1, D_MODEL, D_MODEL), lambda l, j: (l, 0, j)),
                  pl.BlockSpec((1, 1, D_MODEL), lambda l, j: (l, 0, j))],
        out_specs=pl.BlockSpec((1, MOD_ROWS, D_MODEL), lambda l, j: (l, 0, j)),
        compiler_params=_params("parallel", "parallel"),
        name="ada_table",
    )(s_in, ada_w, ada_b.reshape(DEPTH, 1, 6 * D_MODEL))


def _in_kernel(n_stream, n_sub, ctx_row, *refs):
    mod_ref, w_ref, xm_ref, z_ref = refs[n_stream:]
    for r in range(n_sub):
        rs = slice(r * TM, (r + 1) * TM)
        x = _stream_tile(refs[:n_stream], n_sub, r)
        m = _tile_mod(mod_ref, ctx_row, r)
        h = (_ln(x) * (1.0 + _mod_chunk(m, 1)) + _mod_chunk(m, 0)).astype(BF16)
        for j in range(2):
            sl = slice(j * D_MODEL, (j + 1) * D_MODEL)
            xm_ref[0, rs, sl] = _dot(h, w_ref[:, sl]).astype(BF16)
            z_ref[0, rs, sl] = _dot(h, w_ref[:, D_INNER + j * D_MODEL:D_INNER + (j + 1) * D_MODEL]).astype(BF16)


def _in_proj(stream, mod_l, w_in):
    n_batch, rows = _stream_shape(stream)
    n_sub = _tiles_per_step(rows)
    specs, args = _stream_specs(stream, n_sub)
    out = jax.ShapeDtypeStruct((n_batch, rows, D_INNER), BF16)
    out_spec = pl.BlockSpec((1, n_sub * TM, D_INNER), lambda b, t: (b, t, 0))
    return pl.pallas_call(
        functools.partial(_in_kernel, len(args), n_sub, n_batch),
        out_shape=(out, out),
        grid=(n_batch, rows // (n_sub * TM)),
        in_specs=specs + [_const_spec((MOD_ROWS, 1, 6 * D_MODEL)), _const_spec((D_MODEL, 2 * D_INNER))],
        out_specs=(out_spec, out_spec),
        compiler_params=_params("parallel", "parallel"),
        name="mlstm_in_proj",
    )(*args, mod_l, w_in)


def _conv_qkv_kernel(n_sub, xm_ref, prev_ref, next_ref, wc_ref, bc_ref, wq_ref, wk_ref, wv_ref, wg_ref, bg_ref,
                     xc_ref, q_ref, k_ref, v_ref, pre_ref):
    n_t = pl.num_programs(1) * n_sub
    row = lax.broadcasted_iota(jnp.int32, (TM, 1), 0)
    for r in range(n_sub):
        t = pl.program_id(1) * n_sub + r
        rs = slice(r * TM, (r + 1) * TM)
        is_ctx = t == 0
        period = jnp.where(is_ctx, CTX_LEN - 1, GRID_W - 1)
        pos = row & period
        has_left = pos != 0
        has_right = pos != period
        prev_ok = t >= 2
        next_ok = jnp.logical_and(t >= 1, t <= n_t - 2)
        row_w = jnp.where(is_ctx, 0.0, 1.0)
        pre_t = jnp.zeros((N_GATES, TM), F32) + bg_ref[...]
        for h in range(M_HEADS):
            sl = slice(h * M_HEAD_DIM, (h + 1) * M_HEAD_DIM)
            xm = xm_ref[0, rs, sl]
            prev = prev_ref[0, :, sl] if r == 0 else xm_ref[0, r * TM - GRID_W:r * TM, sl]
            nxt = next_ref[0, :, sl] if r == n_sub - 1 else xm_ref[0, (r + 1) * TM:(r + 1) * TM + GRID_W, sl]
            e = jnp.concatenate([jnp.where(prev_ok, prev, 0).astype(F32),
                                 xm.astype(F32),
                                 jnp.where(next_ok, nxt, 0).astype(F32)], axis=0)
            p = []
            for dc in range(CONV_K):
                acc = None
                for dr in range(CONV_K):
                    w = wc_ref[CONV_K * dr + dc:CONV_K * dr + dc + 1, sl]
                    if dr != CONV_K // 2:
                        w = w * row_w
                    term = w * e[dr * GRID_W:dr * GRID_W + TM]
                    acc = term if acc is None else acc + term
                p.append(acc)
            y = (p[1] + bc_ref[:, sl]
                 + jnp.where(has_left, pltpu.roll(p[0], 1, axis=0), 0.0)
                 + jnp.where(has_right, pltpu.roll(p[2], TM - 1, axis=0), 0.0))
            xc = _silu(y).astype(BF16)
            xc_ref[0, rs, sl] = xc
            q_t = _dot_nt(wq_ref[h], xc).astype(BF16)
            k = (_dot(xc, wk_ref[h]) * (M_HEAD_DIM ** -0.5)).astype(BF16)
            v_t = _dot_nt(wv_ref[h], xm).astype(BF16)
            q_ref[0, r, sl, :] = q_t
            k_ref[0, rs, sl] = k
            v_ref[0, r, sl, :] = v_t
            pre_t = pre_t + _dot(wg_ref[h, 0], q_t) + _dot_nt(wg_ref[h, 1], k) + _dot(wg_ref[h, 2], v_t)
        pre_ref[0, :, rs] = pre_t


def _conv_qkv(xm, w_conv, b_conv, w_q, w_k, w_v, w_g, b_g):
    n_batch, rows, _ = xm.shape
    n_sub = _tiles_per_step(rows)
    n_halo = rows // GRID_W
    per = n_sub * TM // GRID_W
    wide = jax.ShapeDtypeStruct((n_batch, rows, D_INNER), BF16)
    wide_t = jax.ShapeDtypeStruct((n_batch, rows // TM, D_INNER, TM), BF16)
    row_spec = pl.BlockSpec((1, n_sub * TM, D_INNER), lambda b, t: (b, t, 0))
    col_spec = pl.BlockSpec((1, n_sub, D_INNER, TM), lambda b, t: (b, t, 0, 0))
    return pl.pallas_call(
        functools.partial(_conv_qkv_kernel, n_sub),
        out_shape=(wide, wide_t, wide, wide_t, jax.ShapeDtypeStruct((n_batch, N_GATES, rows), F32)),
        grid=(n_batch, rows // (n_sub * TM)),
        in_specs=[row_spec,
                  pl.BlockSpec((1, GRID_W, D_INNER), lambda b, t: (b, jnp.maximum(per * t - 1, 0), 0)),
                  pl.BlockSpec((1, GRID_W, D_INNER), lambda b, t: (b, jnp.minimum(per * t + per, n_halo - 1), 0)),
                  _const_spec((CONV_K * CONV_K, D_INNER)),
                  _const_spec((1, D_INNER)),
                  _const_spec((M_HEADS, M_HEAD_DIM, M_HEAD_DIM)),
                  _const_spec((M_HEADS, M_HEAD_DIM, M_HEAD_DIM)),
                  _const_spec((M_HEADS, M_HEAD_DIM, M_HEAD_DIM)),
                  _const_spec((M_HEADS, 3, N_GATES, M_HEAD_DIM)),
                  _const_spec((N_GATES, 1))],
        out_specs=(row_spec, col_spec, row_spec, col_spec,
                   pl.BlockSpec((1, N_GATES, n_sub * TM), lambda b, t: (b, 0, t))),
        compiler_params=_params("parallel", "parallel"),
        name="mlstm_conv_qkv",
    )(xm, xm, xm, w_conv, b_conv, w_q, w_k, w_v, w_g, b_g)


def _scan_lanes(x, combine, fill, reverse):
    n = x.shape[-1]
    lane = lax.broadcasted_iota(jnp.int32, x.shape, 1)
    s = 1
    while s < n:
        if reverse:
            sh = jnp.where(lane < n - s, pltpu.roll(x, n - s, axis=1), fill)
        else:
            sh = jnp.where(lane >= s, pltpu.roll(x, s, axis=1), fill)
        x = combine(x, sh)
        s *= 2
    return x


def _gate_kernel(pre_ref, o_ref):
    zeros = jnp.zeros((GATE_ROWS - 3, TM), F32)
    for c in range(GATE_CHUNKS):
        lanes = slice(c * TM, (c + 1) * TM)
        for d in range(2):
            blk = pre_ref[0, 2 * M_HEADS * d:2 * M_HEADS * (d + 1), lanes]
            log_f = jnp.minimum(blk, 0.0) - jnp.log1p(jnp.exp(-jnp.abs(blk)))
            b = pltpu.roll(_scan_lanes(log_f, jnp.add, 0.0, reverse=d == 1), M_HEADS, axis=0)
            g = blk - b
            big = _scan_lanes(g, jnp.maximum, -jnp.inf, reverse=d == 1)
            for h in range(M_HEADS):
                o_ref[0, d, h, :, lanes] = jnp.concatenate([b[h:h + 1], g[h:h + 1], big[h:h + 1], zeros], axis=0)


def _gate_scans(pre_t):
    n_batch, _, rows = pre_t.shape
    width = GATE_CHUNKS * TM
    return pl.pallas_call(
        _gate_kernel,
        out_shape=jax.ShapeDtypeStruct((n_batch, 2, M_HEADS, GATE_ROWS, rows), F32),
        grid=(n_batch, rows // width),
        in_specs=[pl.BlockSpec((1, N_GATES, width), lambda b, t: (b, 0, t))],
        out_specs=pl.BlockSpec((1, 2, M_HEADS, GATE_ROWS, width), lambda b, t: (b, 0, 0, 0, t)),
        compiler_params=_params("parallel", "parallel"),
        name="mlstm_gate_scans",
    )(pre_t)


def _scan_chunk(d, hh, q_ref, k_ref, v_ref, gate_ref, h_ref, cn_ref, cnb_ref, m_ref):
    sl = slice(hh * M_HEAD_DIM, (hh + 1) * M_HEAD_DIM)
    q_t = q_ref[0, 0, sl, :]
    k = k_ref[0, :, sl]
    v_t = v_ref[0, 0, sl, :]
    gates = gate_ref[0, 0, hh]
    b_row, g_row, big_row = gates[0:1], gates[1:2], gates[2:3]
    m_prev = m_ref[d, hh, 0:1, 0:1]
    last = TM - 1 if d == 0 else 0
    m_row = jnp.maximum(m_prev, big_row)
    m_last = m_row[:, last:last + 1]
    w_inter = jnp.exp(m_prev - m_row)
    den_floor = jnp.exp(-(b_row + m_row))
    w_key = jnp.exp(g_row - m_last).astype(BF16)
    g_col = jnp.transpose(gates)[:, 1:2]
    jj = lax.broadcasted_iota(jnp.int32, (TM, TM), 0)
    ii = lax.broadcasted_iota(jnp.int32, (TM, TM), 1)
    visible = jj <= ii if d == 0 else jj >= ii
    pad = jnp.zeros((N_ROWS - 1, TM), BF16)

    s_t = _dot(k, q_t) * jnp.where(visible, jnp.exp(g_col - m_row), 0.0)
    inter = _dot(cnb_ref[d, hh], q_t)
    v_one = jnp.concatenate([jnp.ones((1, TM), BF16), pad, v_t], axis=0)
    intra = _dot(v_one, s_t.astype(BF16))
    den = w_inter * inter[0:1] + intra[0:1]
    scale = 1.0 / jnp.maximum(jnp.abs(den), den_floor)
    h_t = (w_inter * scale) * inter[N_ROWS:] + scale * intra[N_ROWS:]
    h_ref[0, :, sl] = jnp.transpose(h_t).astype(BF16)

    decay = jnp.exp(m_prev - m_last)
    vw = jnp.concatenate([w_key, pad, v_t * w_key], axis=0)
    for c0 in range(0, M_HEAD_DIM, MXU_WIDTH):
        cn = decay * cn_ref[d, hh, :, c0:c0 + MXU_WIDTH] + _dot(vw, k[:, c0:c0 + MXU_WIDTH])
        cn_ref[d, hh, :, c0:c0 + MXU_WIDTH] = cn
        cnb_ref[d, hh, :, c0:c0 + MXU_WIDTH] = cn.astype(BF16)
    m_ref[d, hh] = jnp.zeros(m_ref.shape[2:], F32) + (b_row[:, last:last + 1] + m_last)


def _scan_kernel(qf_ref, kf_ref, vf_ref, gf_ref, qb_ref, kb_ref, vb_ref, gb_ref, hf_ref, hb_ref,
                 cn_ref, cnb_ref, m_ref):
    @pl.when(pl.program_id(2) == 0)
    def _():
        cn_ref[...] = jnp.zeros_like(cn_ref)
        cnb_ref[...] = jnp.zeros_like(cnb_ref)
        m_ref[...] = jnp.zeros_like(m_ref)

    for hh in range(SCAN_HEADS):
        _scan_chunk(0, hh, qf_ref, kf_ref, vf_ref, gf_ref, hf_ref, cn_ref, cnb_ref, m_ref)
        _scan_chunk(1, hh, qb_ref, kb_ref, vb_ref, gb_ref, hb_ref, cn_ref, cnb_ref, m_ref)


def _mlstm_scan(q_t, k, v_t, gates):
    n_batch, rows, _ = k.shape
    n_t = rows // TM
    width = SCAN_HEADS * M_HEAD_DIM

    def bwd_tile(c):
        return jnp.where(c == 0, 0, n_t - c)

    kf_spec = pl.BlockSpec((1, TM, width), lambda b, h, c: (b, c, h))
    kb_spec = pl.BlockSpec((1, TM, width), lambda b, h, c: (b, bwd_tile(c), h))
    tf_spec = pl.BlockSpec((1, 1, width, TM), lambda b, h, c: (b, c, h, 0))
    tb_spec = pl.BlockSpec((1, 1, width, TM), lambda b, h, c: (b, bwd_tile(c), h, 0))
    gf_spec = pl.BlockSpec((1, 1, SCAN_HEADS, GATE_ROWS, TM), lambda b, h, c: (b, 0, h, 0, c))
    gb_spec = pl.BlockSpec((1, 1, SCAN_HEADS, GATE_ROWS, TM), lambda b, h, c: (b, 1, h, 0, bwd_tile(c)))
    out = jax.ShapeDtypeStruct((n_batch, rows, D_INNER), BF16)
    return pl.pallas_call(
        _scan_kernel,
        out_shape=(out, out),
        grid=(n_batch, M_HEADS // SCAN_HEADS, n_t),
        in_specs=[tf_spec, kf_spec, tf_spec, gf_spec, tb_spec, kb_spec, tb_spec, gb_spec],
        out_specs=(kf_spec, kb_spec),
        scratch_shapes=[pltpu.VMEM((2, SCAN_HEADS, STATE_ROWS, M_HEAD_DIM), F32),
                        pltpu.VMEM((2, SCAN_HEADS, STATE_ROWS, M_HEAD_DIM), BF16),
                        pltpu.VMEM((2, SCAN_HEADS, 8, 128), F32)],
        compiler_params=_params("parallel", "parallel", "arbitrary"),
        name="mlstm_scan",
    )(q_t, k, v_t, gates, q_t, k, v_t, gates)


def _post_norm(x, y, gate, pg, pb):
    return _ln(ALPHA * x + gate * y) * pg + pb


def _readout_kernel(n_stream, n_sub, ctx_row, hf_ref, hb_ref, xc_ref, z_ref, *refs):
    mod_ref, gn_ref, sk_ref, w_ref, pg_ref, pb_ref, o_ref = refs[n_stream:]
    for r in range(n_sub):
        rs = slice(r * TM, (r + 1) * TM)
        parts = []
        for h in range(M_HEADS):
            sl = slice(h * M_HEAD_DIM, (h + 1) * M_HEAD_DIM)
            hn = _ln(hf_ref[0, rs, sl].astype(F32) + hb_ref[0, rs, sl].astype(F32))
            a = hn * gn_ref[:, sl] + sk_ref[:, sl] * xc_ref[0, rs, sl].astype(F32)
            parts.append((a * _silu(z_ref[0, rs, sl].astype(F32))).astype(BF16))
        y = _dot(jnp.concatenate(parts, axis=1), w_ref[...])
        m = _tile_mod(mod_ref, ctx_row, r)
        o_ref[0, rs, :] = _post_norm(_stream_tile(refs[:n_stream], n_sub, r), y, _mod_chunk(m, 2),
                                     pg_ref[...], pb_ref[...])


def _mlstm_readout(hf, hb, xc, z, stream, mod_l, gn_w, skip, w_out, pg, pb):
    n_batch, rows = _stream_shape(stream)
    n_sub = _tiles_per_step(rows)
    specs, args = _stream_specs(stream, n_sub)
    wide = pl.BlockSpec((1, n_sub * TM, D_INNER), lambda b, t: (b, t, 0))
    return pl.pallas_call(
        functools.partial(_readout_kernel, len(args), n_sub, n_batch),
        out_shape=jax.ShapeDtypeStruct((n_batch, rows, D_MODEL), F32),
        grid=(n_batch, rows // (n_sub * TM)),
        in_specs=[wide, wide, wide, wide] + specs
        + [_const_spec((MOD_ROWS, 1, 6 * D_MODEL)),
           _const_spec((1, D_INNER)), _const_spec((1, D_INNER)), _const_spec((D_INNER, D_MODEL)),
           _const_spec((1, D_MODEL)), _const_spec((1, D_MODEL))],
        out_specs=pl.BlockSpec((1, n_sub * TM, D_MODEL), lambda b, t: (b, t, 0)),
        compiler_params=_params("parallel", "parallel"),
        name="mlstm_readout",
    )(hf, hb, xc, z, *args, mod_l, gn_w, skip, w_out, pg, pb)


def _fourier_out_kernel(n_sub, ctx_row, *refs):
    mx_refs, refs = refs[:n_sub], refs[n_sub:]
    if ctx_row is not None:
        mc_ref, refs = refs[0], refs[1:]
    x_refs, (mod_ref, w_ref, pg_ref, pb_ref, o_ref) = refs[:n_sub], refs[n_sub:]
    for r in range(n_sub):
        a = jnp.concatenate([mx_refs[r][0, j] for j in range(D_MODEL // LANES)], axis=1)
        if ctx_row is not None and r == 0:
            a = jnp.where(pl.program_id(1) == 0, mc_ref[0], a)
        y = _dot(a.astype(BF16), w_ref[...])
        m = _tile_mod(mod_ref, ctx_row, r)
        o_ref[0, r * TM:(r + 1) * TM, :] = _post_norm(x_refs[r][0], y, _mod_chunk(m, 2), pg_ref[...], pb_ref[...])


def _fourier_out(mixed_x, mixed_c, xs, mod_l, w_out, pg, pb):
    n_batch = xs.shape[0]
    ctx_tile = mixed_c is not None
    ctx_tiles = CTX_LEN // TM
    n_t = mixed_x.shape[2] // TM + (ctx_tiles if ctx_tile else 0)
    n_sub = _tiles_per_step(n_t * TM)
    mixed_off = ctx_tiles if ctx_tile else 0
    x_off = 0 if ctx_tile else ctx_tiles

    def mixed_map(r):
        return lambda b, t: (b, 0, jnp.maximum(n_sub * t + r - mixed_off, 0), 0)

    def x_map(r):
        return lambda b, t: (b, n_sub * t + r + x_off, 0)

    in_specs = [pl.BlockSpec((1, D_MODEL // LANES, TM, LANES), mixed_map(r)) for r in range(n_sub)]
    args = [mixed_x] * n_sub
    if ctx_tile:
        in_specs.append(pl.BlockSpec((1, CTX_LEN, D_MODEL), lambda b, t: (b, 0, 0)))
        args.append(mixed_c)
    in_specs += [pl.BlockSpec((1, TM, D_MODEL), x_map(r)) for r in range(n_sub)]
    args += [xs] * n_sub
    in_specs += [_const_spec((MOD_ROWS, 1, 6 * D_MODEL)),
                 _const_spec((D_MODEL, D_MODEL)), _const_spec((1, D_MODEL)), _const_spec((1, D_MODEL))]
    return pl.pallas_call(
        functools.partial(_fourier_out_kernel, n_sub, n_batch if ctx_tile else None),
        out_shape=jax.ShapeDtypeStruct((n_batch, n_t * TM, D_MODEL), F32),
        grid=(n_batch, n_t // n_sub),
        in_specs=in_specs,
        out_specs=pl.BlockSpec((1, n_sub * TM, D_MODEL), lambda b, t: (b, t, 0)),
        compiler_params=_params("parallel", "parallel"),
        name="fourier_out",
    )(*args, mod_l, w_out, pg, pb)


def _tile_mod(mod_ref, ctx_row, r):
    m = mod_ref[pl.program_id(0)]
    if ctx_row is not None and r == 0:
        m = jnp.where(pl.program_id(1) == 0, mod_ref[ctx_row], m)
    return m


def _tiles_per_step(rows):
    n_t = rows // TM
    return 3 if n_t % 3 == 0 else 2


def _mlp_kernel(n_sub, ctx_row, x_ref, mod_ref, w1_ref, w2_ref, pg_ref, pb_ref, o_ref):
    for r in range(n_sub):
        rs = slice(r * TM, (r + 1) * TM)
        x = x_ref[0, rs, :]
        m = _tile_mod(mod_ref, ctx_row, r)
        h = (_ln(x) * (1.0 + _mod_chunk(m, 4)) + _mod_chunk(m, 3)).astype(BF16)
        acc = jnp.zeros((TM, D_MODEL), F32)
        for j in range(D_FF // D_MODEL):
            sl = slice(j * D_MODEL, (j + 1) * D_MODEL)
            u = jnp.square(jnp.maximum(_dot(h, w1_ref[:, sl]), 0.0)).astype(BF16)
            acc = acc + _dot(u, w2_ref[sl, :])
        o_ref[0, rs, :] = _post_norm(x, acc, _mod_chunk(m, 5), pg_ref[...], pb_ref[...])


def _mlp(xs, mod_l, w1, w2, pg, pb, ctx_tile):
    n_batch, rows, _ = xs.shape
    n_sub = _tiles_per_step(rows)
    blk = pl.BlockSpec((1, n_sub * TM, D_MODEL), lambda b, t: (b, t, 0))
    return pl.pallas_call(
        functools.partial(_mlp_kernel, n_sub, n_batch if ctx_tile else None),
        out_shape=jax.ShapeDtypeStruct((n_batch, rows, D_MODEL), F32),
        grid=(n_batch, rows // (n_sub * TM)),
        in_specs=[blk, _const_spec((MOD_ROWS, 1, 6 * D_MODEL)),
                  _const_spec((D_MODEL, D_FF)), _const_spec((D_FF, D_MODEL)),
                  _const_spec((1, D_MODEL)), _const_spec((1, D_MODEL))],
        out_specs=blk,
        compiler_params=_params("parallel", "parallel"),
        name="mlp",
    )(xs, mod_l, w1, w2, pg, pb)


def _dft_tables():
    two_pi = 2.0 * np.pi
    c = np.arange(F_GROUP_DIM)
    ang = two_pi * ((c[:, None] * c[None, :]) % F_GROUP_DIM) / F_GROUP_DIM
    s_ch = 1.0 / math.sqrt(F_GROUP_DIM)
    f_ch = np.concatenate([np.cos(ang) * s_ch, -np.sin(ang) * s_ch], axis=1)

    seq = FFT_R * FFT_R * FFT_C
    r = np.arange(FFT_R)
    eye = np.eye(FFT_J)

    def tile_mix(w):
        wr, wi = np.kron(w.real, eye), np.kron(w.imag, eye)
        return np.block([[wr, -wi], [wi, wr]])

    w8 = np.exp(-2j * np.pi * ((r[:, None] * r[None, :]) % FFT_R) / FFT_R) / math.sqrt(FFT_R)
    k_a = tile_mix(w8)
    k_b = np.stack([tile_mix(w8 * np.exp(-2j * np.pi * (r[None, :] * a1) / (FFT_R * FFT_R)))
                    for a1 in range(FFT_R)])
    c = np.arange(FFT_C)
    t1 = np.arange(FFT_R * FFT_R)
    num = (c[None, :, None] * c[None, None, :] * (FFT_R * FFT_R) + t1[:, None, None] * c[None, None, :]) % seq
    ang = two_pi * num / seq
    f_c = np.concatenate([np.cos(ang), np.sin(ang)], axis=2) / math.sqrt(FFT_C)

    p = np.arange(CTX_LEN)
    ang = two_pi * ((p[:, None] * p[None, :]) % CTX_LEN) / CTX_LEN
    sc = 1.0 / math.sqrt(CTX_LEN)
    f_ctx = np.concatenate([np.cos(ang) * sc, np.sin(ang) * sc], axis=1)
    return tuple(jnp.asarray(a, dtype=F32).astype(BF16) for a in (f_ch, k_a, k_b, f_c, f_ctx))


def _cdft_kernel(ctx_tile, *refs):
    if ctx_tile:
        x_ref, mod_ref, f_ref, zrx_ref, zix_ref, zrc_ref, zic_ref = refs
    else:
        x_ref, mod_ref, f_ref, zrx_ref, zix_ref = refs
    m = mod_ref[0]
    h = (_ln(x_ref[0]) * (1.0 + _mod_chunk(m, 1)) + _mod_chunk(m, 0)).astype(BF16)
    f = f_ref[...]
    zr, zi = [], []
    for g in range(F_GROUPS):
        z = _dot(h[:, g * F_GROUP_DIM:(g + 1) * F_GROUP_DIM], f).astype(BF16)
        zr.append(z[:, :F_GROUP_DIM])
        zi.append(z[:, F_GROUP_DIM:])
    def store_latent():
        for g in range(F_GROUPS):
            zrx_ref[0, g] = zr[g]
            zix_ref[0, g] = zi[g]

    if ctx_tile:
        t = pl.program_id(1)

        @pl.when(t == 0)
        def _():
            zrc_ref[0] = jnp.concatenate(zr, axis=1)
            zic_ref[0] = jnp.concatenate(zi, axis=1)

        pl.when(t > 0)(store_latent)
    else:
        store_latent()


def _channel_dft(xs, mod_l, f_ch, ctx_tile):
    n_batch, rows, _ = xs.shape
    seq = rows - CTX_LEN
    off = CTX_LEN // TM
    zx = jax.ShapeDtypeStruct((n_batch, F_GROUPS, seq, F_GROUP_DIM), BF16)
    zc = jax.ShapeDtypeStruct((n_batch, CTX_LEN, D_MODEL), BF16)
    if ctx_tile:
        n_t = rows // TM
        x_out = pl.BlockSpec((1, F_GROUPS, TM, F_GROUP_DIM), lambda b, t: (b, 0, jnp.maximum(t - off, 0), 0))
        c_out = pl.BlockSpec((1, CTX_LEN, D_MODEL), lambda b, t: (b, 0, 0))
        out_shape, out_specs, x_in = (zx, zx, zc, zc), (x_out, x_out, c_out, c_out), _tile_spec(D_MODEL)
    else:
        n_t = seq // TM
        x_out = pl.BlockSpec((1, F_GROUPS, TM, F_GROUP_DIM), lambda b, t: (b, 0, t, 0))
        out_shape, out_specs, x_in = (zx, zx), (x_out, x_out), _tile_spec(D_MODEL, off)
    return pl.pallas_call(
        functools.partial(_cdft_kernel, ctx_tile),
        out_shape=out_shape,
        grid=(n_batch, n_t),
        in_specs=[x_in, _mod_spec(n_batch, ctx_tile), _const_spec((F_GROUP_DIM, 2 * F_GROUP_DIM))],
        out_specs=out_specs,
        compiler_params=_params("parallel", "arbitrary"),
        name="fourier_channel_dft",
    )(xs, mod_l, f_ch)


def _seq_dft_kernel(zr_ref, zi_ref, ka_ref, kb_ref, fc_ref, o_ref, ar_ref, ai_ref, br_ref, bi_ref):
    a_rows = FFT_R * FFT_C
    n_tiles = FFT_C // FFT_J

    def mix(load, store, base, step, k):
        offs = [base + i * step for i in range(FFT_R)]
        z = jnp.concatenate([load(0, o) for o in offs] + [load(1, o) for o in offs], axis=0)
        u = _dot(k, z).astype(BF16)
        for i, o in enumerate(offs):
            store(0, o, u[i * FFT_J:(i + 1) * FFT_J])
            store(1, o, u[(FFT_R + i) * FFT_J:(FFT_R + i + 1) * FFT_J])

    def load_in(p, o):
        return (zr_ref, zi_ref)[p][0, 0, pl.ds(o, FFT_J), :]

    def load_a(p, o):
        return (ar_ref, ai_ref)[p][pl.ds(o, FFT_J), :]

    def store_a(p, o, val):
        (ar_ref, ai_ref)[p][pl.ds(o, FFT_J), :] = val

    def store_b(p, o, val):
        (br_ref, bi_ref)[p][pl.ds(o, FFT_J), :] = val

    def stage_a(b, carry):
        for ct in range(n_tiles):
            mix(load_in, store_a, pl.multiple_of(b * FFT_C, FFT_C) + ct * FFT_J, a_rows, ka_ref[...])
        return carry

    lax.fori_loop(0, FFT_R, stage_a, 0)

    def stage_b(a1, carry):
        k = kb_ref[a1]
        for ct in range(n_tiles):
            mix(load_a, store_b, pl.multiple_of(a1 * a_rows, a_rows) + ct * FFT_J, FFT_C, k)
        return carry

    lax.fori_loop(0, FFT_R, stage_b, 0)

    def stage_c(t1, carry):
        rows = pl.ds(pl.multiple_of((t1 & (FFT_R - 1)) * a_rows + lax.shift_right_logical(t1, 3) * FFT_C, FFT_C),
                     FFT_C)
        z = jnp.concatenate([br_ref[rows, :], bi_ref[rows, :]], axis=0)
        y = _dot(fc_ref[t1], z)
        for j in range(FFT_CB // LANES):
            o_ref[0, j, pl.ds(t1, FFT_C, stride=FFT_R * FFT_R), :] = y[:, j * LANES:(j + 1) * LANES]
        return carry

    lax.fori_loop(0, FFT_R * FFT_R, stage_c, 0, unroll=DFT_UNROLL)


def _seq_dft(zr, zi, k_a, k_b, f_c):
    n_batch, n_blocks, seq, _ = zr.shape
    blk = pl.BlockSpec((1, 1, seq, FFT_CB), lambda b, j: (b, j, 0, 0))
    return pl.pallas_call(
        _seq_dft_kernel,
        out_shape=jax.ShapeDtypeStruct((n_batch, D_MODEL // LANES, seq, LANES), F32),
        grid=(n_batch, n_blocks),
        in_specs=[blk, blk, _const_spec(k_a.shape), _const_spec(k_b.shape), _const_spec(f_c.shape)],
        out_specs=pl.BlockSpec((1, FFT_CB // LANES, seq, LANES), lambda b, j: (b, j, 0, 0)),
        scratch_shapes=[pltpu.VMEM((seq, FFT_CB), BF16)] * 4,
        compiler_params=_params("parallel", "parallel", vmem_limit=SEQ_DFT_VMEM_LIMIT),
        name="fourier_seq_dft",
    )(zr, zi, k_a, k_b, f_c)


def _ctx_dft_kernel(zr_ref, zi_ref, f_ref, o_ref):
    o_ref[0] = _dot(f_ref[...], jnp.concatenate([zr_ref[0], zi_ref[0]], axis=0))


def _ctx_dft(zr, zi, f_ctx):
    n_batch = zr.shape[0]
    blk = pl.BlockSpec((1, CTX_LEN, D_MODEL), lambda b: (b, 0, 0))
    return pl.pallas_call(
        _ctx_dft_kernel,
        out_shape=jax.ShapeDtypeStruct((n_batch, CTX_LEN, D_MODEL), F32),
        grid=(n_batch,),
        in_specs=[blk, blk, _const_spec((CTX_LEN, 2 * CTX_LEN))],
        out_specs=blk,
        compiler_params=_params("parallel"),
        name="fourier_ctx_dft",
    )(zr, zi, f_ctx)


def _mlstm_layer(xs, mod_l, w_in, w_conv, b_conv, w_q, w_k, w_v, w_gate, b_gate, gn_w, skip, w_out, pg, pb):
    xm, z = _in_proj(xs, mod_l, w_in.astype(BF16))
    wg = w_gate.reshape(3, M_HEADS, M_HEAD_DIM, N_GATES).transpose(1, 0, 3, 2)
    w_qt = jnp.swapaxes(w_q, 1, 2).astype(BF16)
    w_vt = jnp.swapaxes(w_v, 1, 2).astype(BF16)
    xc, q_t, k, v_t, pre_t = _conv_qkv(
        xm, w_conv.reshape(CONV_K * CONV_K, D_INNER), b_conv.reshape(1, D_INNER),
        w_qt, w_k.astype(BF16), w_vt, wg.astype(BF16), b_gate.reshape(N_GATES, 1))
    hf, hb = _mlstm_scan(q_t, k, v_t, _gate_scans(pre_t))
    return _mlstm_readout(hf, hb, xc, z, xs, mod_l, gn_w.reshape(1, D_INNER), skip.reshape(1, D_INNER),
                          w_out.astype(BF16), pg, pb)


def _fourier_layer(xs, mod_l, w_out, pg, pb, need_ctx, tables):
    f_ch, k_a, k_b, f_c, f_ctx = tables
    z = _channel_dft(xs, mod_l, f_ch, need_ctx)
    mixed_x = _seq_dft(z[0], z[1], k_a, k_b, f_c)
    mixed_c = _ctx_dft(z[2], z[3], f_ctx) if need_ctx else None
    return _fourier_out(mixed_x, mixed_c, xs, mod_l, w_out.astype(BF16), pg, pb)


def kernel(x, c, ctx, c_ctx, ada_w, ada_b, post_g, post_b, m_w_in, m_w_conv, m_b_conv, m_w_q, m_w_k, m_w_v, m_w_gate, m_b_gate, m_gn_w, m_skip, m_w_out, f_w_out, mlp_w1, mlp_w2):
    n_batch = x.shape[0]
    assert n_batch < MOD_ROWS and x.shape[1] == FFT_R * FFT_R * FFT_C and ctx.shape[1] == CTX_LEN == TM
    xs = (ctx, x)
    s_in = jnp.concatenate([c, c_ctx[None, :], jnp.zeros((MOD_ROWS - n_batch - 1, D_MODEL), F32)], axis=0)
    mod = _ada_table(s_in, ada_w, ada_b)
    tables = _dft_tables()
    for i in range(DEPTH):
        is_mlstm = i % N_MIXERS == 0
        j = i // N_MIXERS
        need_ctx = i < DEPTH - 1
        mod_l = mod[i].reshape(MOD_ROWS, 1, 6 * D_MODEL)
        pg = post_g[i].reshape(2, 1, D_MODEL)
        pb = post_b[i].reshape(2, 1, D_MODEL)
        if is_mlstm:
            xs = _mlstm_layer(xs, mod_l, m_w_in[j], m_w_conv[j], m_b_conv[j], m_w_q[j], m_w_k[j], m_w_v[j],
                              m_w_gate[j], m_b_gate[j], m_gn_w[j], m_skip[j], m_w_out[j], pg[0], pb[0])
            if not need_ctx:
                xs = xs[:, CTX_LEN:]
        else:
            xs = _fourier_layer(xs, mod_l, f_w_out[j], pg[0], pb[0], need_ctx, tables)
        xs = _mlp(xs, mod_l, mlp_w1[i].astype(BF16), mlp_w2[i].astype(BF16), pg[1], pb[1], need_ctx)
    return xs
```

```python
import functools
import math

import numpy as np
import jax
import jax.numpy as jnp
from jax import lax
from jax.experimental import pallas as pl
from jax.experimental.pallas import tpu as pltpu

D_MODEL = 1024
DEPTH = 4
GRID_W = 64
CTX_LEN = 256
N_MIXERS = 2
D_INNER = 2 * D_MODEL
M_HEADS = 4
M_HEAD_DIM = D_INNER // M_HEADS
CONV_K = 3
F_GROUPS = 4
F_GROUP_DIM = D_MODEL // F_GROUPS
D_FF = 4 * D_MODEL
ALPHA = float((2 * DEPTH) ** 0.25)
LN_EPS = 1e-5

TM = 256
N_GATES = 4 * M_HEADS
GATE_ROWS = 8
MOD_ROWS = 8
FFT_R = 8
FFT_C = 128
FFT_J = 16
FFT_CB = F_GROUP_DIM
LANES = 128
GATE_CHUNKS = 3
N_ROWS = 16
STATE_ROWS = N_ROWS + M_HEAD_DIM
MXU_WIDTH = 256
SCAN_HEADS = 2
DFT_UNROLL = 8
VMEM_LIMIT = 52 * 1024 * 1024
SEQ_DFT_VMEM_LIMIT = 58 * 1024 * 1024

F32 = jnp.float32
BF16 = jnp.bfloat16


def _params(*sem, vmem_limit=VMEM_LIMIT):
    return pltpu.CompilerParams(dimension_semantics=sem, vmem_limit_bytes=vmem_limit)


def _ln(x):
    mu = jnp.mean(x, axis=-1, keepdims=True)
    xc = x - mu
    var = jnp.mean(xc * xc, axis=-1, keepdims=True)
    return xc * lax.rsqrt(var + LN_EPS)


def _silu(x):
    t = 0.5 * x
    return t + t * jnp.tanh(t)


def _dot(a, b):
    return jnp.dot(a, b, preferred_element_type=F32)


def _dot_nt(a, b):
    return lax.dot_general(a, b, (((1,), (1,)), ((), ())), preferred_element_type=F32)


def _mod_chunk(m, j):
    return m[:, j * D_MODEL:(j + 1) * D_MODEL]


def _const_spec(shape):
    nd = len(shape)
    return pl.BlockSpec(shape, lambda *_: (0,) * nd, pipeline_mode=pl.Buffered(1))


def _mod_spec(n_batch, ctx_tile):
    if ctx_tile:
        return pl.BlockSpec((1, 1, 6 * D_MODEL), lambda b, t: (jnp.where(t == 0, n_batch, b), 0, 0))
    return pl.BlockSpec((1, 1, 6 * D_MODEL), lambda b, t: (b, 0, 0))


def _tile_spec(width, off=0):
    return pl.BlockSpec((1, TM, width), lambda b, t: (b, t + off, 0))


def _stream_specs(stream, n_sub):
    split = isinstance(stream, tuple)
    off = CTX_LEN // TM if split else 0

    def tile_map(r):
        return lambda b, t: (b, jnp.maximum(n_sub * t + r - off, 0), 0)

    specs = [pl.BlockSpec((1, TM, D_MODEL), tile_map(r)) for r in range(n_sub)]
    if split:
        return [pl.BlockSpec((1, CTX_LEN, D_MODEL), lambda b, t: (b, 0, 0))] + specs, [stream[0]] + [stream[1]] * n_sub
    return specs, [stream] * n_sub


def _stream_shape(stream):
    if isinstance(stream, tuple):
        return stream[1].shape[0], stream[0].shape[1] + stream[1].shape[1]
    return stream.shape[0], stream.shape[1]


def _stream_tile(refs, n_sub, r):
    if len(refs) > n_sub:
        x = refs[1 + r][0]
        return jnp.where(pl.program_id(1) == 0, refs[0][0], x) if r == 0 else x
    return refs[r][0]


def _ada_kernel(s_ref, w_ref, b_ref, o_ref):
    s = _silu(s_ref[...])
    o_ref[0] = jnp.dot(s, w_ref[0], preferred_element_type=F32, precision=lax.Precision.HIGHEST) + b_ref[0]


def _ada_table(s_in, ada_w, ada_b):
    n_chunks = 6
    return pl.pallas_call(
        _ada_kernel,
        out_shape=jax.ShapeDtypeStruct((DEPTH, MOD_ROWS, 6 * D_MODEL), F32),
        grid=(DEPTH, n_chunks),
        in_specs=[pl.BlockSpec((MOD_ROWS, D_MODEL), lambda l, j: (0, 0)),
                  pl.BlockSpec((1, D_MODEL, D_MODEL), lambda l, j: (l, 0, j)),
                  pl.BlockSpec((1, 1, D_MODEL), lambda l, j: (l, 0, j))],
        out_specs=pl.BlockSpec((1, MOD_ROWS, D_MODEL), lambda l, j: (l, 0, j)),
        compiler_params=_params("parallel", "parallel"),
        name="ada_table",
    )(s_in, ada_w, ada_b.reshape(DEPTH, 1, 6 * D_MODEL))


def _in_kernel(n_stream, n_sub, ctx_row, *refs):
    mod_ref, w_ref, xm_ref, z_ref = refs[n_stream:]
    for r in range(n_sub):
        rs = slice(r * TM, (r + 1) * TM)
        x = _stream_tile(refs[:n_stream], n_sub, r)
        m = _tile_mod(mod_ref, ctx_row, r)
        h = (_ln(x) * (1.0 + _mod_chunk(m, 1)) + _mod_chunk(m, 0)).astype(BF16)
        for j in range(2):
            sl = slice(j * D_MODEL, (j + 1) * D_MODEL)
            xm_ref[0, rs, sl] = _dot(h, w_ref[:, sl]).astype(BF16)
            z_ref[0, rs, sl] = _dot(h, w_ref[:, D_INNER + j * D_MODEL:D_INNER + (j + 1) * D_MODEL]).astype(BF16)


def _in_proj(stream, mod_l, w_in):
    n_batch, rows = _stream_shape(stream)
    n_sub = _tiles_per_step(rows)
    specs, args = _stream_specs(stream, n_sub)
    out = jax.ShapeDtypeStruct((n_batch, rows, D_INNER), BF16)
    out_spec = pl.BlockSpec((1, n_sub * TM, D_INNER), lambda b, t: (b, t, 0))
    return pl.pallas_call(
        functools.partial(_in_kernel, len(args), n_sub, n_batch),
        out_shape=(out, out),
        grid=(n_batch, rows // (n_sub * TM)),
        in_specs=specs + [_const_spec((MOD_ROWS, 1, 6 * D_MODEL)), _const_spec((D_MODEL, 2 * D_INNER))],
        out_specs=(out_spec, out_spec),
        compiler_params=_params("parallel", "parallel"),
        name="mlstm_in_proj",
    )(*args, mod_l, w_in)


def _conv_qkv_kernel(n_sub, xm_ref, prev_ref, next_ref, wc_ref, bc_ref, wq_ref, wk_ref, wv_ref, wg_ref, bg_ref,
                     xc_ref, q_ref, k_ref, v_ref, pre_ref):
    n_t = pl.num_programs(1) * n_sub
    row = lax.broadcasted_iota(jnp.int32, (TM, 1), 0)
    for r in range(n_sub):
        t = pl.program_id(1) * n_sub + r
        rs = slice(r * TM, (r + 1) * TM)
        is_ctx = t == 0
        period = jnp.where(is_ctx, CTX_LEN - 1, GRID_W - 1)
        pos = row & period
        has_left = pos != 0
        has_right = pos != period
        prev_ok = t >= 2
        next_ok = jnp.logical_and(t >= 1, t <= n_t - 2)
        row_w = jnp.where(is_ctx, 0.0, 1.0)
        pre_t = jnp.zeros((N_GATES, TM), F32) + bg_ref[...]
        for h in range(M_HEADS):
            sl = slice(h * M_HEAD_DIM, (h + 1) * M_HEAD_DIM)
            xm = xm_ref[0, rs, sl]
            prev = prev_ref[0, :, sl] if r == 0 else xm_ref[0, r * TM - GRID_W:r * TM, sl]
            nxt = next_ref[0, :, sl] if r == n_sub - 1 else xm_ref[0, (r + 1) * TM:(r + 1) * TM + GRID_W, sl]
            e = jnp.concatenate([jnp.where(prev_ok, prev, 0), xm, jnp.where(next_ok, nxt, 0)], axis=0)
            p = []
            for dc in range(CONV_K):
                acc = None
                for dr in range(CONV_K):
                    w = wc_ref[CONV_K * dr + dc:CONV_K * dr + dc + 1, sl]
                    if dr != CONV_K // 2:
                        w = w * row_w
                    term = w.astype(BF16) * e[dr * GRID_W:dr * GRID_W + TM]
                    acc = term if acc is None else acc + term
                p.append(acc.astype(F32))
            y = (p[1] + bc_ref[:, sl]
                 + jnp.where(has_left, pltpu.roll(p[0], 1, axis=0), 0.0)
                 + jnp.where(has_right, pltpu.roll(p[2], TM - 1, axis=0), 0.0))
            xc = _silu(y).astype(BF16)
            xc_ref[0, rs, sl] = xc
            q_t = _dot_nt(wq_ref[h], xc).astype(BF16)
            k = (_dot(xc, wk_ref[h]) * (M_HEAD_DIM ** -0.5)).astype(BF16)
            v_t = _dot_nt(wv_ref[h], xm).astype(BF16)
            q_ref[0, r, sl, :] = q_t
            k_ref[0, rs, sl] = k
            v_ref[0, r, sl, :] = v_t
            pre_t = pre_t + _dot(wg_ref[h, 0], q_t) + _dot_nt(wg_ref[h, 1], k) + _dot(wg_ref[h, 2], v_t)
        pre_ref[0, :, rs] = pre_t


def _conv_qkv(xm, w_conv, b_conv, w_q, w_k, w_v, w_g, b_g):
    n_batch, rows, _ = xm.shape
    n_sub = _tiles_per_step(rows)
    n_halo = rows // GRID_W
    per = n_sub * TM // GRID_W
    wide = jax.ShapeDtypeStruct((n_batch, rows, D_INNER), BF16)
    wide_t = jax.ShapeDtypeStruct((n_batch, rows // TM, D_INNER, TM), BF16)
    row_spec = pl.BlockSpec((1, n_sub * TM, D_INNER), lambda b, t: (b, t, 0))
    col_spec = pl.BlockSpec((1, n_sub, D_INNER, TM), lambda b, t: (b, t, 0, 0))
    return pl.pallas_call(
        functools.partial(_conv_qkv_kernel, n_sub),
        out_shape=(wide, wide_t, wide, wide_t, jax.ShapeDtypeStruct((n_batch, N_GATES, rows), F32)),
        grid=(n_batch, rows // (n_sub * TM)),
        in_specs=[row_spec,
                  pl.BlockSpec((1, GRID_W, D_INNER), lambda b, t: (b, jnp.maximum(per * t - 1, 0), 0)),
                  pl.BlockSpec((1, GRID_W, D_INNER), lambda b, t: (b, jnp.minimum(per * t + per, n_halo - 1), 0)),
                  _const_spec((CONV_K * CONV_K, D_INNER)),
                  _const_spec((1, D_INNER)),
                  _const_spec((M_HEADS, M_HEAD_DIM, M_HEAD_DIM)),
                  _const_spec((M_HEADS, M_HEAD_DIM, M_HEAD_DIM)),
                  _const_spec((M_HEADS, M_HEAD_DIM, M_HEAD_DIM)),
                  _const_spec((M_HEADS, 3, N_GATES, M_HEAD_DIM)),
                  _const_spec((N_GATES, 1))],
        out_specs=(row_spec, col_spec, row_spec, col_spec,
                   pl.BlockSpec((1, N_GATES, n_sub * TM), lambda b, t: (b, 0, t))),
        compiler_params=_params("parallel", "parallel"),
        name="mlstm_conv_qkv",
    )(xm, xm, xm, w_conv, b_conv, w_q, w_k, w_v, w_g, b_g)


def _scan_lanes(x, combine, fill, reverse):
    n = x.shape[-1]
    lane = lax.broadcasted_iota(jnp.int32, x.shape, 1)
    s = 1
    while s < n:
        if reverse:
            sh = jnp.where(lane < n - s, pltpu.roll(x, n - s, axis=1), fill)
        else:
            sh = jnp.where(lane >= s, pltpu.roll(x, s, axis=1), fill)
        x = combine(x, sh)
        s *= 2
    return x


def _gate_kernel(pre_ref, o_ref):
    zeros = jnp.zeros((GATE_ROWS - 3, TM), F32)
    for c in range(GATE_CHUNKS):
        lanes = slice(c * TM, (c + 1) * TM)
        for d in range(2):
            blk = pre_ref[0, 2 * M_HEADS * d:2 * M_HEADS * (d + 1), lanes]
            log_f = jnp.minimum(blk, 0.0) - jnp.log1p(jnp.exp(-jnp.abs(blk)))
            b = pltpu.roll(_scan_lanes(log_f, jnp.add, 0.0, reverse=d == 1), M_HEADS, axis=0)
            g = blk - b
            big = _scan_lanes(g, jnp.maximum, -jnp.inf, reverse=d == 1)
            for h in range(M_HEADS):
                o_ref[0, d, h, :, lanes] = jnp.concatenate([b[h:h + 1], g[h:h + 1], big[h:h + 1], zeros], axis=0)


def _gate_scans(pre_t):
    n_batch, _, rows = pre_t.shape
    width = GATE_CHUNKS * TM
    return pl.pallas_call(
        _gate_kernel,
        out_shape=jax.ShapeDtypeStruct((n_batch, 2, M_HEADS, GATE_ROWS, rows), F32),
        grid=(n_batch, rows // width),
        in_specs=[pl.BlockSpec((1, N_GATES, width), lambda b, t: (b, 0, t))],
        out_specs=pl.BlockSpec((1, 2, M_HEADS, GATE_ROWS, width), lambda b, t: (b, 0, 0, 0, t)),
        compiler_params=_params("parallel", "parallel"),
        name="mlstm_gate_scans",
    )(pre_t)


def _scan_chunk(d, hh, q_ref, k_ref, v_ref, gate_ref, h_ref, cn_ref, cnb_ref, m_ref):
    sl = slice(hh * M_HEAD_DIM, (hh + 1) * M_HEAD_DIM)
    q_t = q_ref[0, 0, sl, :]
    k = k_ref[0, :, sl]
    v_t = v_ref[0, 0, sl, :]
    gates = gate_ref[0, 0, hh]
    b_row, g_row, big_row = gates[0:1], gates[1:2], gates[2:3]
    m_prev = m_ref[d, hh, 0:1, 0:1]
    last = TM - 1 if d == 0 else 0
    m_row = jnp.maximum(m_prev, big_row)
    m_last = m_row[:, last:last + 1]
    w_inter = jnp.exp(m_prev - m_row)
    den_floor = jnp.exp(-(b_row + m_row))
    w_key = jnp.exp(g_row - m_last).astype(BF16)
    g_col = jnp.transpose(gates)[:, 1:2]
    jj = lax.broadcasted_iota(jnp.int32, (TM, TM), 0)
    ii = lax.broadcasted_iota(jnp.int32, (TM, TM), 1)
    visible = jj <= ii if d == 0 else jj >= ii
    pad = jnp.zeros((N_ROWS - 1, TM), BF16)

    s_t = _dot(k, q_t) * jnp.where(visible, jnp.exp(g_col - m_row), 0.0)
    inter = _dot(cnb_ref[d, hh], q_t)
    v_one = jnp.concatenate([jnp.ones((1, TM), BF16), pad, v_t], axis=0)
    intra = _dot(v_one, s_t.astype(BF16))
    den = w_inter * inter[0:1] + intra[0:1]
    scale = 1.0 / jnp.maximum(jnp.abs(den), den_floor)
    h_t = (w_inter * scale) * inter[N_ROWS:] + scale * intra[N_ROWS:]
    h_ref[0, :, sl] = jnp.transpose(h_t).astype(BF16)

    decay = jnp.exp(m_prev - m_last)
    vw = jnp.concatenate([w_key, pad, v_t * w_key], axis=0)
    for c0 in range(0, M_HEAD_DIM, MXU_WIDTH):
        cn = decay * cn_ref[d, hh, :, c0:c0 + MXU_WIDTH] + _dot(vw, k[:, c0:c0 + MXU_WIDTH])
        cn_ref[d, hh, :, c0:c0 + MXU_WIDTH] = cn
        cnb_ref[d, hh, :, c0:c0 + MXU_WIDTH] = cn.astype(BF16)
    m_ref[d, hh] = jnp.zeros(m_ref.shape[2:], F32) + (b_row[:, last:last + 1] + m_last)


def _scan_kernel(qf_ref, kf_ref, vf_ref, gf_ref, qb_ref, kb_ref, vb_ref, gb_ref, hf_ref, hb_ref,
                 cn_ref, cnb_ref, m_ref):
    @pl.when(pl.program_id(2) == 0)
    def _():
        cn_ref[...] = jnp.zeros_like(cn_ref)
        cnb_ref[...] = jnp.zeros_like(cnb_ref)
        m_ref[...] = jnp.zeros_like(m_ref)

    for hh in range(SCAN_HEADS):
        _scan_chunk(0, hh, qf_ref, kf_ref, vf_ref, gf_ref, hf_ref, cn_ref, cnb_ref, m_ref)
        _scan_chunk(1, hh, qb_ref, kb_ref, vb_ref, gb_ref, hb_ref, cn_ref, cnb_ref, m_ref)


def _mlstm_scan(q_t, k, v_t, gates):
    n_batch, rows, _ = k.shape
    n_t = rows // TM
    width = SCAN_HEADS * M_HEAD_DIM

    def bwd_tile(c):
        return jnp.where(c == 0, 0, n_t - c)

    kf_spec = pl.BlockSpec((1, TM, width), lambda b, h, c: (b, c, h))
    kb_spec = pl.BlockSpec((1, TM, width), lambda b, h, c: (b, bwd_tile(c), h))
    tf_spec = pl.BlockSpec((1, 1, width, TM), lambda b, h, c: (b, c, h, 0))
    tb_spec = pl.BlockSpec((1, 1, width, TM), lambda b, h, c: (b, bwd_tile(c), h, 0))
    gf_spec = pl.BlockSpec((1, 1, SCAN_HEADS, GATE_ROWS, TM), lambda b, h, c: (b, 0, h, 0, c))
    gb_spec = pl.BlockSpec((1, 1, SCAN_HEADS, GATE_ROWS, TM), lambda b, h, c: (b, 1, h, 0, bwd_tile(c)))
    out = jax.ShapeDtypeStruct((n_batch, rows, D_INNER), BF16)
    return pl.pallas_call(
        _scan_kernel,
        out_shape=(out, out),
        grid=(n_batch, M_HEADS // SCAN_HEADS, n_t),
        in_specs=[tf_spec, kf_spec, tf_spec, gf_spec, tb_spec, kb_spec, tb_spec, gb_spec],
        out_specs=(kf_spec, kb_spec),
        scratch_shapes=[pltpu.VMEM((2, SCAN_HEADS, STATE_ROWS, M_HEAD_DIM), F32),
                        pltpu.VMEM((2, SCAN_HEADS, STATE_ROWS, M_HEAD_DIM), BF16),
                        pltpu.VMEM((2, SCAN_HEADS, 8, 128), F32)],
        compiler_params=_params("parallel", "parallel", "arbitrary"),
        name="mlstm_scan",
    )(q_t, k, v_t, gates, q_t, k, v_t, gates)


def _post_norm(x, y, gate, pg, pb):
    return _ln(ALPHA * x + gate * y) * pg + pb


def _readout_kernel(n_stream, n_sub, ctx_row, hf_ref, hb_ref, xc_ref, z_ref, *refs):
    mod_ref, gn_ref, sk_ref, w_ref, pg_ref, pb_ref, o_ref = refs[n_stream:]
    for r in range(n_sub):
        rs = slice(r * TM, (r + 1) * TM)
        parts = []
        for h in range(M_HEADS):
            sl = slice(h * M_HEAD_DIM, (h + 1) * M_HEAD_DIM)
            hn = _ln(hf_ref[0, rs, sl].astype(F32) + hb_ref[0, rs, sl].astype(F32))
            a = hn * gn_ref[:, sl] + sk_ref[:, sl] * xc_ref[0, rs, sl].astype(F32)
            parts.append((a * _silu(z_ref[0, rs, sl]).astype(F32)).astype(BF16))
        y = _dot(jnp.concatenate(parts, axis=1), w_ref[...])
        m = _tile_mod(mod_ref, ctx_row, r)
        o_ref[0, rs, :] = _post_norm(_stream_tile(refs[:n_stream], n_sub, r), y, _mod_chunk(m, 2),
                                     pg_ref[...], pb_ref[...])


def _mlstm_readout(hf, hb, xc, z, stream, mod_l, gn_w, skip, w_out, pg, pb):
    n_batch, rows = _stream_shape(stream)
    n_sub = _tiles_per_step(rows)
    specs, args = _stream_specs(stream, n_sub)
    wide = pl.BlockSpec((1, n_sub * TM, D_INNER), lambda b, t: (b, t, 0))
    return pl.pallas_call(
        functools.partial(_readout_kernel, len(args), n_sub, n_batch),
        out_shape=jax.ShapeDtypeStruct((n_batch, rows, D_MODEL), F32),
        grid=(n_batch, rows // (n_sub * TM)),
        in_specs=[wide, wide, wide, wide] + specs
        + [_const_spec((MOD_ROWS, 1, 6 * D_MODEL)),
           _const_spec((1, D_INNER)), _const_spec((1, D_INNER)), _const_spec((D_INNER, D_MODEL)),
           _const_spec((1, D_MODEL)), _const_spec((1, D_MODEL))],
        out_specs=pl.BlockSpec((1, n_sub * TM, D_MODEL), lambda b, t: (b, t, 0)),
        compiler_params=_params("parallel", "parallel"),
        name="mlstm_readout",
    )(hf, hb, xc, z, *args, mod_l, gn_w, skip, w_out, pg, pb)


def _fourier_out_kernel(n_sub, ctx_row, *refs):
    mx_refs, refs = refs[:n_sub], refs[n_sub:]
    if ctx_row is not None:
        mc_ref, refs = refs[0], refs[1:]
    x_refs, (mod_ref, w_ref, pg_ref, pb_ref, o_ref) = refs[:n_sub], refs[n_sub:]
    for r in range(n_sub):
        a = jnp.concatenate([mx_refs[r][0, j] for j in range(D_MODEL // LANES)], axis=1)
        if ctx_row is not None and r == 0:
            a = jnp.where(pl.program_id(1) == 0, mc_ref[0], a)
        y = _dot(a.astype(BF16), w_ref[...])
        m = _tile_mod(mod_ref, ctx_row, r)
        o_ref[0, r * TM:(r + 1) * TM, :] = _post_norm(x_refs[r][0], y, _mod_chunk(m, 2), pg_ref[...], pb_ref[...])


def _fourier_out(mixed_x, mixed_c, xs, mod_l, w_out, pg, pb):
    n_batch = xs.shape[0]
    ctx_tile = mixed_c is not None
    ctx_tiles = CTX_LEN // TM
    n_t = mixed_x.shape[2] // TM + (ctx_tiles if ctx_tile else 0)
    n_sub = _tiles_per_step(n_t * TM)
    mixed_off = ctx_tiles if ctx_tile else 0
    x_off = 0 if ctx_tile else ctx_tiles

    def mixed_map(r):
        return lambda b, t: (b, 0, jnp.maximum(n_sub * t + r - mixed_off, 0), 0)

    def x_map(r):
        return lambda b, t: (b, n_sub * t + r + x_off, 0)

    in_specs = [pl.BlockSpec((1, D_MODEL // LANES, TM, LANES), mixed_map(r)) for r in range(n_sub)]
    args = [mixed_x] * n_sub
    if ctx_tile:
        in_specs.append(pl.BlockSpec((1, CTX_LEN, D_MODEL), lambda b, t: (b, 0, 0)))
        args.append(mixed_c)
    in_specs += [pl.BlockSpec((1, TM, D_MODEL), x_map(r)) for r in range(n_sub)]
    args += [xs] * n_sub
    in_specs += [_const_spec((MOD_ROWS, 1, 6 * D_MODEL)),
                 _const_spec((D_MODEL, D_MODEL)), _const_spec((1, D_MODEL)), _const_spec((1, D_MODEL))]
    return pl.pallas_call(
        functools.partial(_fourier_out_kernel, n_sub, n_batch if ctx_tile else None),
        out_shape=jax.ShapeDtypeStruct((n_batch, n_t * TM, D_MODEL), F32),
        grid=(n_batch, n_t // n_sub),
        in_specs=in_specs,
        out_specs=pl.BlockSpec((1, n_sub * TM, D_MODEL), lambda b, t: (b, t, 0)),
        compiler_params=_params("parallel", "parallel"),
        name="fourier_out",
    )(*args, mod_l, w_out, pg, pb)


def _tile_mod(mod_ref, ctx_row, r):
    m = mod_ref[pl.program_id(0)]
    if ctx_row is not None and r == 0:
        m = jnp.where(pl.program_id(1) == 0, mod_ref[ctx_row], m)
    return m


def _tiles_per_step(rows):
    n_t = rows // TM
    return 3 if n_t % 3 == 0 else 2


def _mlp_kernel(n_sub, ctx_row, x_ref, mod_ref, w1_ref, w2_ref, pg_ref, pb_ref, o_ref):
    for r in range(n_sub):
        rs = slice(r * TM, (r + 1) * TM)
        x = x_ref[0, rs, :]
        m = _tile_mod(mod_ref, ctx_row, r)
        h = (_ln(x) * (1.0 + _mod_chunk(m, 4)) + _mod_chunk(m, 3)).astype(BF16)
        acc = jnp.zeros((TM, D_MODEL), F32)
        for j in range(D_FF // D_MODEL):
            sl = slice(j * D_MODEL, (j + 1) * D_MODEL)
            u = jnp.square(jnp.maximum(_dot(h, w1_ref[:, sl]), 0.0)).astype(BF16)
            acc = acc + _dot(u, w2_ref[sl, :])
        o_ref[0, rs, :] = _post_norm(x, acc, _mod_chunk(m, 5), pg_ref[...], pb_ref[...])


def _mlp(xs, mod_l, w1, w2, pg, pb, ctx_tile):
    n_batch, rows, _ = xs.shape
    n_sub = _tiles_per_step(rows)
    blk = pl.BlockSpec((1, n_sub * TM, D_MODEL), lambda b, t: (b, t, 0))
    return pl.pallas_call(
        functools.partial(_mlp_kernel, n_sub, n_batch if ctx_tile else None),
        out_shape=jax.ShapeDtypeStruct((n_batch, rows, D_MODEL), F32),
        grid=(n_batch, rows // (n_sub * TM)),
        in_specs=[blk, _const_spec((MOD_ROWS, 1, 6 * D_MODEL)),
                  _const_spec((D_MODEL, D_FF)), _const_spec((D_FF, D_MODEL)),
                  _const_spec((1, D_MODEL)), _const_spec((1, D_MODEL))],
        out_specs=blk,
        compiler_params=_params("parallel", "parallel"),
        name="mlp",
    )(xs, mod_l, w1, w2, pg, pb)


def _dft_tables():
    two_pi = 2.0 * np.pi
    c = np.arange(F_GROUP_DIM)
    ang = two_pi * ((c[:, None] * c[None, :]) % F_GROUP_DIM) / F_GROUP_DIM
    s_ch = 1.0 / math.sqrt(F_GROUP_DIM)
    f_ch = np.concatenate([np.cos(ang) * s_ch, -np.sin(ang) * s_ch], axis=1)

    seq = FFT_R * FFT_R * FFT_C
    r = np.arange(FFT_R)
    eye = np.eye(FFT_J)

    def tile_mix(w):
        wr, wi = np.kron(w.real, eye), np.kron(w.imag, eye)
        return np.block([[wr, -wi], [wi, wr]])

    w8 = np.exp(-2j * np.pi * ((r[:, None] * r[None, :]) % FFT_R) / FFT_R) / math.sqrt(FFT_R)
    k_a = tile_mix(w8)
    k_b = np.stack([tile_mix(w8 * np.exp(-2j * np.pi * (r[None, :] * a1) / (FFT_R * FFT_R)))
                    for a1 in range(FFT_R)])
    c = np.arange(FFT_C)
    t1 = np.arange(FFT_R * FFT_R)
    num = (c[None, :, None] * c[None, None, :] * (FFT_R * FFT_R) + t1[:, None, None] * c[None, None, :]) % seq
    ang = two_pi * num / seq
    f_c = np.concatenate([np.cos(ang), np.sin(ang)], axis=2) / math.sqrt(FFT_C)

    p = np.arange(CTX_LEN)
    ang = two_pi * ((p[:, None] * p[None, :]) % CTX_LEN) / CTX_LEN
    sc = 1.0 / math.sqrt(CTX_LEN)
    f_ctx = np.concatenate([np.cos(ang) * sc, np.sin(ang) * sc], axis=1)
    return tuple(jnp.asarray(a, dtype=F32).astype(BF16) for a in (f_ch, k_a, k_b, f_c, f_ctx))


def _cdft_kernel(ctx_tile, *refs):
    if ctx_tile:
        x_ref, mod_ref, f_ref, zrx_ref, zix_ref, zrc_ref, zic_ref = refs
    else:
        x_ref, mod_ref, f_ref, zrx_ref, zix_ref = refs
    m = mod_ref[0]
    h = (_ln(x_ref[0]) * (1.0 + _mod_chunk(m, 1)) + _mod_chunk(m, 0)).astype(BF16)
    f = f_ref[...]
    zr, zi = [], []
    for g in range(F_GROUPS):
        z = _dot(h[:, g * F_GROUP_DIM:(g + 1) * F_GROUP_DIM], f).astype(BF16)
        zr.append(z[:, :F_GROUP_DIM])
        zi.append(z[:, F_GROUP_DIM:])
    def store_latent():
        for g in range(F_GROUPS):
            zrx_ref[0, g] = zr[g]
            zix_ref[0, g] = zi[g]

    if ctx_tile:
        t = pl.program_id(1)

        @pl.when(t == 0)
        def _():
            zrc_ref[0] = jnp.concatenate(zr, axis=1)
            zic_ref[0] = jnp.concatenate(zi, axis=1)

        pl.when(t > 0)(store_latent)
    else:
        store_latent()


def _channel_dft(xs, mod_l, f_ch, ctx_tile):
    n_batch, rows, _ = xs.shape
    seq = rows - CTX_LEN
    off = CTX_LEN // TM
    zx = jax.ShapeDtypeStruct((n_batch, F_GROUPS, seq, F_GROUP_DIM), BF16)
    zc = jax.ShapeDtypeStruct((n_batch, CTX_LEN, D_MODEL), BF16)
    if ctx_tile:
        n_t = rows // TM
        x_out = pl.BlockSpec((1, F_GROUPS, TM, F_GROUP_DIM), lambda b, t: (b, 0, jnp.maximum(t - off, 0), 0))
        c_out = pl.BlockSpec((1, CTX_LEN, D_MODEL), lambda b, t: (b, 0, 0))
        out_shape, out_specs, x_in = (zx, zx, zc, zc), (x_out, x_out, c_out, c_out), _tile_spec(D_MODEL)
    else:
        n_t = seq // TM
        x_out = pl.BlockSpec((1, F_GROUPS, TM, F_GROUP_DIM), lambda b, t: (b, 0, t, 0))
        out_shape, out_specs, x_in = (zx, zx), (x_out, x_out), _tile_spec(D_MODEL, off)
    return pl.pallas_call(
        functools.partial(_cdft_kernel, ctx_tile),
        out_shape=out_shape,
        grid=(n_batch, n_t),
        in_specs=[x_in, _mod_spec(n_batch, ctx_tile), _const_spec((F_GROUP_DIM, 2 * F_GROUP_DIM))],
        out_specs=out_specs,
        compiler_params=_params("parallel", "arbitrary"),
        name="fourier_channel_dft",
    )(xs, mod_l, f_ch)


def _seq_dft_kernel(zr_ref, zi_ref, ka_ref, kb_ref, fc_ref, o_ref, ar_ref, ai_ref, br_ref, bi_ref):
    a_rows = FFT_R * FFT_C
    n_tiles = FFT_C // FFT_J

    def mix(load, store, base, step, k):
        offs = [base + i * step for i in range(FFT_R)]
        z = jnp.concatenate([load(0, o) for o in offs] + [load(1, o) for o in offs], axis=0)
        u = _dot(k, z).astype(BF16)
        for i, o in enumerate(offs):
            store(0, o, u[i * FFT_J:(i + 1) * FFT_J])
            store(1, o, u[(FFT_R + i) * FFT_J:(FFT_R + i + 1) * FFT_J])

    def load_in(p, o):
        return (zr_ref, zi_ref)[p][0, 0, pl.ds(o, FFT_J), :]

    def load_a(p, o):
        return (ar_ref, ai_ref)[p][pl.ds(o, FFT_J), :]

    def store_a(p, o, val):
        (ar_ref, ai_ref)[p][pl.ds(o, FFT_J), :] = val

    def store_b(p, o, val):
        (br_ref, bi_ref)[p][pl.ds(o, FFT_J), :] = val

    def stage_a(b, carry):
        for ct in range(n_tiles):
            mix(load_in, store_a, pl.multiple_of(b * FFT_C, FFT_C) + ct * FFT_J, a_rows, ka_ref[...])
        return carry

    lax.fori_loop(0, FFT_R, stage_a, 0)

    def stage_b(a1, carry):
        k = kb_ref[a1]
        for ct in range(n_tiles):
            mix(load_a, store_b, pl.multiple_of(a1 * a_rows, a_rows) + ct * FFT_J, FFT_C, k)
        return carry

    lax.fori_loop(0, FFT_R, stage_b, 0)

    def stage_c(t1, carry):
        rows = pl.ds(pl.multiple_of((t1 & (FFT_R - 1)) * a_rows + lax.shift_right_logical(t1, 3) * FFT_C, FFT_C),
                     FFT_C)
        z = jnp.concatenate([br_ref[rows, :], bi_ref[rows, :]], axis=0)
        y = _dot(fc_ref[t1], z)
        for j in range(FFT_CB // LANES):
            o_ref[0, j, pl.ds(t1, FFT_C, stride=FFT_R * FFT_R), :] = y[:, j * LANES:(j + 1) * LANES]
        return carry

    lax.fori_loop(0, FFT_R * FFT_R, stage_c, 0, unroll=DFT_UNROLL)


def _seq_dft(zr, zi, k_a, k_b, f_c):
    n_batch, n_blocks, seq, _ = zr.shape
    blk = pl.BlockSpec((1, 1, seq, FFT_CB), lambda b, j: (b, j, 0, 0))
    return pl.pallas_call(
        _seq_dft_kernel,
        out_shape=jax.ShapeDtypeStruct((n_batch, D_MODEL // LANES, seq, LANES), F32),
        grid=(n_batch, n_blocks),
        in_specs=[blk, blk, _const_spec(k_a.shape), _const_spec(k_b.shape), _const_spec(f_c.shape)],
        out_specs=pl.BlockSpec((1, FFT_CB // LANES, seq, LANES), lambda b, j: (b, j, 0, 0)),
        scratch_shapes=[pltpu.VMEM((seq, FFT_CB), BF16)] * 4,
        compiler_params=_params("parallel", "parallel", vmem_limit=SEQ_DFT_VMEM_LIMIT),
        name="fourier_seq_dft",
    )(zr, zi, k_a, k_b, f_c)


def _ctx_dft_kernel(zr_ref, zi_ref, f_ref, o_ref):
    o_ref[0] = _dot(f_ref[...], jnp.concatenate([zr_ref[0], zi_ref[0]], axis=0))


def _ctx_dft(zr, zi, f_ctx):
    n_batch = zr.shape[0]
    blk = pl.BlockSpec((1, CTX_LEN, D_MODEL), lambda b: (b, 0, 0))
    return pl.pallas_call(
        _ctx_dft_kernel,
        out_shape=jax.ShapeDtypeStruct((n_batch, CTX_LEN, D_MODEL), F32),
        grid=(n_batch,),
        in_specs=[blk, blk, _const_spec((CTX_LEN, 2 * CTX_LEN))],
        out_specs=blk,
        compiler_params=_params("parallel"),
        name="fourier_ctx_dft",
    )(zr, zi, f_ctx)


def _mlstm_layer(xs, mod_l, w_in, w_conv, b_conv, w_q, w_k, w_v, w_gate, b_gate, gn_w, skip, w_out, pg, pb):
    xm, z = _in_proj(xs, mod_l, w_in.astype(BF16))
    wg = w_gate.reshape(3, M_HEADS, M_HEAD_DIM, N_GATES).transpose(1, 0, 3, 2)
    w_qt = jnp.swapaxes(w_q, 1, 2).astype(BF16)
    w_vt = jnp.swapaxes(w_v, 1, 2).astype(BF16)
    xc, q_t, k, v_t, pre_t = _conv_qkv(
        xm, w_conv.reshape(CONV_K * CONV_K, D_INNER), b_conv.reshape(1, D_INNER),
        w_qt, w_k.astype(BF16), w_vt, wg.astype(BF16), b_gate.reshape(N_GATES, 1))
    hf, hb = _mlstm_scan(q_t, k, v_t, _gate_scans(pre_t))
    return _mlstm_readout(hf, hb, xc, z, xs, mod_l, gn_w.reshape(1, D_INNER), skip.reshape(1, D_INNER),
                          w_out.astype(BF16), pg, pb)


def _fourier_layer(xs, mod_l, w_out, pg, pb, need_ctx, tables):
    f_ch, k_a, k_b, f_c, f_ctx = tables
    z = _channel_dft(xs, mod_l, f_ch, need_ctx)
    mixed_x = _seq_dft(z[0], z[1], k_a, k_b, f_c)
    mixed_c = _ctx_dft(z[2], z[3], f_ctx) if need_ctx else None
    return _fourier_out(mixed_x, mixed_c, xs, mod_l, w_out.astype(BF16), pg, pb)


def kernel(x, c, ctx, c_ctx, ada_w, ada_b, post_g, post_b, m_w_in, m_w_conv, m_b_conv, m_w_q, m_w_k, m_w_v, m_w_gate, m_b_gate, m_gn_w, m_skip, m_w_out, f_w_out, mlp_w1, mlp_w2):
    n_batch = x.shape[0]
    assert n_batch < MOD_ROWS and x.shape[1] == FFT_R * FFT_R * FFT_C and ctx.shape[1] == CTX_LEN == TM
    xs = (ctx, x)
    s_in = jnp.concatenate([c, c_ctx[None, :], jnp.zeros((MOD_ROWS - n_batch - 1, D_MODEL), F32)], axis=0)
    mod = _ada_table(s_in, ada_w, ada_b)
    tables = _dft_tables()
    for i in range(DEPTH):
        is_mlstm = i % N_MIXERS == 0
        j = i // N_MIXERS
        need_ctx = i < DEPTH - 1
        mod_l = mod[i].reshape(MOD_ROWS, 1, 6 * D_MODEL)
        pg = post_g[i].reshape(2, 1, D_MODEL)
        pb = post_b[i].reshape(2, 1, D_MODEL)
        if is_mlstm:
            xs = _mlstm_layer(xs, mod_l, m_w_in[j], m_w_conv[j], m_b_conv[j], m_w_q[j], m_w_k[j], m_w_v[j],
                              m_w_gate[j], m_b_gate[j], m_gn_w[j], m_skip[j], m_w_out[j], pg[0], pb[0])
            if not need_ctx:
                xs = xs[:, CTX_LEN:]
        else:
            xs = _fourier_layer(xs, mod_l, f_w_out[j], pg[0], pb[0], need_ctx, tables)
        xs = _mlp(xs, mod_l, mlp_w1[i].astype(BF16), mlp_w2[i].astype(BF16), pg[1], pb[1], need_ctx)
    return xs
```

```python
import functools
import math

import numpy as np
import jax
import jax.numpy as jnp
from jax import lax
from jax.experimental import pallas as pl
from jax.experimental.pallas import tpu as pltpu

D_MODEL = 1024
DEPTH = 4
GRID_W = 64
CTX_LEN = 256
N_MIXERS = 2
D_INNER = 2 * D_MODEL
M_HEADS = 4
M_HEAD_DIM = D_INNER // M_HEADS
CONV_K = 3
F_GROUPS = 4
F_GROUP_DIM = D_MODEL // F_GROUPS
D_FF = 4 * D_MODEL
ALPHA = float((2 * DEPTH) ** 0.25)
LN_EPS = 1e-5

LANES = 128
SUBLANES = 8
BF16_ROWS = 16
MXU_WIDTH = 256

TM = 256
N_GATES = 4 * M_HEADS
GATE_ROWS = SUBLANES
MOD_ROWS = SUBLANES
FFT_R = 8
FFT_C = 128
FFT_J = BF16_ROWS
FFT_CB = F_GROUP_DIM
GATE_CHUNKS = 3
N_ROWS = BF16_ROWS
STATE_ROWS = N_ROWS + M_HEAD_DIM
SCAN_HEADS = 2
DFT_UNROLL = 8
VMEM_LIMIT = 52 * 1024 * 1024
SEQ_DFT_VMEM_LIMIT = 58 * 1024 * 1024

F32 = jnp.float32
BF16 = jnp.bfloat16


def _params(*sem, vmem_limit=VMEM_LIMIT):
    return pltpu.CompilerParams(dimension_semantics=sem, vmem_limit_bytes=vmem_limit)


def _ln(x):
    mu = jnp.mean(x, axis=-1, keepdims=True)
    xc = x - mu
    var = jnp.mean(xc * xc, axis=-1, keepdims=True)
    return xc * lax.rsqrt(var + LN_EPS)


def _silu(x):
    t = 0.5 * x
    return t + t * jnp.tanh(t)


def _dot(a, b):
    return jnp.dot(a, b, preferred_element_type=F32)


def _dot_nt(a, b):
    return lax.dot_general(a, b, (((1,), (1,)), ((), ())), preferred_element_type=F32)


def _mod_chunk(m, j):
    return m[:, j * D_MODEL:(j + 1) * D_MODEL]


def _const_spec(shape):
    nd = len(shape)
    return pl.BlockSpec(shape, lambda *_: (0,) * nd, pipeline_mode=pl.Buffered(1))


def _mod_spec(n_batch, ctx_tile):
    if ctx_tile:
        return pl.BlockSpec((1, 1, 6 * D_MODEL), lambda b, t: (jnp.where(t == 0, n_batch, b), 0, 0))
    return pl.BlockSpec((1, 1, 6 * D_MODEL), lambda b, t: (b, 0, 0))


def _tile_spec(width, off=0):
    return pl.BlockSpec((1, TM, width), lambda b, t: (b, t + off, 0))


def _stream_specs(stream, n_sub):
    split = isinstance(stream, tuple)
    off = CTX_LEN // TM if split else 0

    def tile_map(r):
        return lambda b, t: (b, jnp.maximum(n_sub * t + r - off, 0), 0)

    specs = [pl.BlockSpec((1, TM, D_MODEL), tile_map(r)) for r in range(n_sub)]
    if split:
        return [pl.BlockSpec((1, CTX_LEN, D_MODEL), lambda b, t: (b, 0, 0))] + specs, [stream[0]] + [stream[1]] * n_sub
    return specs, [stream] * n_sub


def _tiles_per_step(rows):
    n_t = rows // TM
    return 3 if n_t % 3 == 0 else 2


def _tile_mod(mod_ref, ctx_row, r):
    m = mod_ref[pl.program_id(0)]
    if ctx_row is not None and r == 0:
        m = jnp.where(pl.program_id(1) == 0, mod_ref[ctx_row], m)
    return m


def _stream_shape(stream):
    if isinstance(stream, tuple):
        return stream[1].shape[0], stream[0].shape[1] + stream[1].shape[1]
    return stream.shape[0], stream.shape[1]


def _stream_tile(refs, n_sub, r):
    if len(refs) > n_sub:
        x = refs[1 + r][0]
        return jnp.where(pl.program_id(1) == 0, refs[0][0], x) if r == 0 else x
    return refs[r][0]


def _ada_kernel(s_ref, w_ref, b_ref, o_ref):
    s = _silu(s_ref[...])
    o_ref[0] = jnp.dot(s, w_ref[0], preferred_element_type=F32, precision=lax.Precision.HIGHEST) + b_ref[0]


def _ada_table(s_in, ada_w, ada_b):
    n_chunks = 6
    return pl.pallas_call(
        _ada_kernel,
        out_shape=jax.ShapeDtypeStruct((DEPTH, MOD_ROWS, 6 * D_MODEL), F32),
        grid=(DEPTH, n_chunks),
        in_specs=[pl.BlockSpec((MOD_ROWS, D_MODEL), lambda l, j: (0, 0)),
                  pl.BlockSpec((1, D_MODEL, D_MODEL), lambda l, j: (l, 0, j)),
                  pl.BlockSpec((1, 1, D_MODEL), lambda l, j: (l, 0, j))],
        out_specs=pl.BlockSpec((1, MOD_ROWS, D_MODEL), lambda l, j: (l, 0, j)),
        compiler_params=_params("parallel", "parallel"),
        name="ada_table",
    )(s_in, ada_w, ada_b.reshape(DEPTH, 1, 6 * D_MODEL))


def _in_kernel(n_stream, n_sub, ctx_row, *refs):
    mod_ref, w_ref, xm_ref, z_ref = refs[n_stream:]
    for r in range(n_sub):
        rs = slice(r * TM, (r + 1) * TM)
        x = _stream_tile(refs[:n_stream], n_sub, r)
        m = _tile_mod(mod_ref, ctx_row, r)
        h = (_ln(x) * (1.0 + _mod_chunk(m, 1)) + _mod_chunk(m, 0)).astype(BF16)
        for j in range(2):
            sl = slice(j * D_MODEL, (j + 1) * D_MODEL)
            xm_ref[0, rs, sl] = _dot(h, w_ref[:, sl]).astype(BF16)
            z_ref[0, rs, sl] = _dot(h, w_ref[:, D_INNER + j * D_MODEL:D_INNER + (j + 1) * D_MODEL]).astype(BF16)


def _in_proj(stream, mod_l, w_in):
    n_batch, rows = _stream_shape(stream)
    n_sub = _tiles_per_step(rows)
    specs, args = _stream_specs(stream, n_sub)
    out = jax.ShapeDtypeStruct((n_batch, rows, D_INNER), BF16)
    out_spec = pl.BlockSpec((1, n_sub * TM, D_INNER), lambda b, t: (b, t, 0))
    return pl.pallas_call(
        functools.partial(_in_kernel, len(args), n_sub, n_batch),
        out_shape=(out, out),
        grid=(n_batch, rows // (n_sub * TM)),
        in_specs=specs + [_const_spec((MOD_ROWS, 1, 6 * D_MODEL)), _const_spec((D_MODEL, 2 * D_INNER))],
        out_specs=(out_spec, out_spec),
        compiler_params=_params("parallel", "parallel"),
        name="mlstm_in_proj",
    )(*args, mod_l, w_in)


def _conv_qkv_kernel(n_sub, xm_ref, prev_ref, next_ref, wc_ref, bc_ref, wq_ref, wk_ref, wv_ref, wg_ref, bg_ref,
                     xc_ref, q_ref, k_ref, v_ref, pre_ref):
    n_t = pl.num_programs(1) * n_sub
    row = lax.broadcasted_iota(jnp.int32, (TM, 1), 0)
    for r in range(n_sub):
        t = pl.program_id(1) * n_sub + r
        rs = slice(r * TM, (r + 1) * TM)
        is_ctx = t == 0
        period = jnp.where(is_ctx, CTX_LEN - 1, GRID_W - 1)
        pos = row & period
        has_left = pos != 0
        has_right = pos != period
        prev_ok = t >= 2
        next_ok = jnp.logical_and(t >= 1, t <= n_t - 2)
        row_w = jnp.where(is_ctx, 0.0, 1.0)
        pre_t = jnp.zeros((N_GATES, TM), F32) + bg_ref[...]
        for h in range(M_HEADS):
            sl = slice(h * M_HEAD_DIM, (h + 1) * M_HEAD_DIM)
            xm = xm_ref[0, rs, sl]
            prev = prev_ref[0, :, sl] if r == 0 else xm_ref[0, r * TM - GRID_W:r * TM, sl]
            nxt = next_ref[0, :, sl] if r == n_sub - 1 else xm_ref[0, (r + 1) * TM:(r + 1) * TM + GRID_W, sl]
            e = jnp.concatenate([jnp.where(prev_ok, prev, 0), xm, jnp.where(next_ok, nxt, 0)], axis=0)
            p = []
            for dc in range(CONV_K):
                acc = None
                for dr in range(CONV_K):
                    w = wc_ref[CONV_K * dr + dc:CONV_K * dr + dc + 1, sl]
                    if dr != CONV_K // 2:
                        w = w * row_w
                    term = w.astype(BF16) * e[dr * GRID_W:dr * GRID_W + TM]
                    acc = term if acc is None else acc + term
                p.append(acc.astype(F32))
            y = (p[1] + bc_ref[:, sl]
                 + jnp.where(has_left, pltpu.roll(p[0], 1, axis=0), 0.0)
                 + jnp.where(has_right, pltpu.roll(p[2], TM - 1, axis=0), 0.0))
            xc = _silu(y).astype(BF16)
            xc_ref[0, rs, sl] = xc
            q_t = _dot_nt(wq_ref[h], xc).astype(BF16)
            k = (_dot(xc, wk_ref[h]) * (M_HEAD_DIM ** -0.5)).astype(BF16)
            v_t = _dot_nt(wv_ref[h], xm).astype(BF16)
            q_ref[0, r, sl, :] = q_t
            k_ref[0, rs, sl] = k
            v_ref[0, r, sl, :] = v_t
            pre_t = pre_t + _dot(wg_ref[h, 0], q_t) + _dot_nt(wg_ref[h, 1], k) + _dot(wg_ref[h, 2], v_t)
        pre_ref[0, :, rs] = pre_t


def _conv_qkv(xm, w_conv, b_conv, w_q, w_k, w_v, w_g, b_g):
    n_batch, rows, _ = xm.shape
    n_sub = _tiles_per_step(rows)
    n_halo = rows // GRID_W
    per = n_sub * TM // GRID_W
    wide = jax.ShapeDtypeStruct((n_batch, rows, D_INNER), BF16)
    wide_t = jax.ShapeDtypeStruct((n_batch, rows // TM, D_INNER, TM), BF16)
    row_spec = pl.BlockSpec((1, n_sub * TM, D_INNER), lambda b, t: (b, t, 0))
    col_spec = pl.BlockSpec((1, n_sub, D_INNER, TM), lambda b, t: (b, t, 0, 0))
    return pl.pallas_call(
        functools.partial(_conv_qkv_kernel, n_sub),
        out_shape=(wide, wide_t, wide, wide_t, jax.ShapeDtypeStruct((n_batch, N_GATES, rows), F32)),
        grid=(n_batch, rows // (n_sub * TM)),
        in_specs=[row_spec,
                  pl.BlockSpec((1, GRID_W, D_INNER), lambda b, t: (b, jnp.maximum(per * t - 1, 0), 0)),
                  pl.BlockSpec((1, GRID_W, D_INNER), lambda b, t: (b, jnp.minimum(per * t + per, n_halo - 1), 0)),
                  _const_spec((CONV_K * CONV_K, D_INNER)),
                  _const_spec((1, D_INNER)),
                  _const_spec((M_HEADS, M_HEAD_DIM, M_HEAD_DIM)),
                  _const_spec((M_HEADS, M_HEAD_DIM, M_HEAD_DIM)),
                  _const_spec((M_HEADS, M_HEAD_DIM, M_HEAD_DIM)),
                  _const_spec((M_HEADS, 3, N_GATES, M_HEAD_DIM)),
                  _const_spec((N_GATES, 1))],
        out_specs=(row_spec, col_spec, row_spec, col_spec,
                   pl.BlockSpec((1, N_GATES, n_sub * TM), lambda b, t: (b, 0, t))),
        compiler_params=_params("parallel", "parallel"),
        name="mlstm_conv_qkv",
    )(xm, xm, xm, w_conv, b_conv, w_q, w_k, w_v, w_g, b_g)


def _scan_lanes(x, combine, fill, reverse):
    n = x.shape[-1]
    lane = lax.broadcasted_iota(jnp.int32, x.shape, 1)
    s = 1
    while s < n:
        if reverse:
            sh = jnp.where(lane < n - s, pltpu.roll(x, n - s, axis=1), fill)
        else:
            sh = jnp.where(lane >= s, pltpu.roll(x, s, axis=1), fill)
        x = combine(x, sh)
        s *= 2
    return x


def _gate_kernel(pre_ref, o_ref):
    zeros = jnp.zeros((GATE_ROWS - 3, TM), F32)
    for c in range(GATE_CHUNKS):
        lanes = slice(c * TM, (c + 1) * TM)
        for d in range(2):
            blk = pre_ref[0, 2 * M_HEADS * d:2 * M_HEADS * (d + 1), lanes]
            log_f = jnp.minimum(blk, 0.0) - jnp.log1p(jnp.exp(-jnp.abs(blk)))
            b = pltpu.roll(_scan_lanes(log_f, jnp.add, 0.0, reverse=d == 1), M_HEADS, axis=0)
            g = blk - b
            big = _scan_lanes(g, jnp.maximum, -jnp.inf, reverse=d == 1)
            for h in range(M_HEADS):
                o_ref[0, d, h, :, lanes] = jnp.concatenate([b[h:h + 1], g[h:h + 1], big[h:h + 1], zeros], axis=0)


def _gate_scans(pre_t):
    n_batch, _, rows = pre_t.shape
    width = GATE_CHUNKS * TM
    return pl.pallas_call(
        _gate_kernel,
        out_shape=jax.ShapeDtypeStruct((n_batch, 2, M_HEADS, GATE_ROWS, rows), F32),
        grid=(n_batch, rows // width),
        in_specs=[pl.BlockSpec((1, N_GATES, width), lambda b, t: (b, 0, t))],
        out_specs=pl.BlockSpec((1, 2, M_HEADS, GATE_ROWS, width), lambda b, t: (b, 0, 0, 0, t)),
        compiler_params=_params("parallel", "parallel"),
        name="mlstm_gate_scans",
    )(pre_t)


def _scan_chunk(d, hh, q_ref, k_ref, v_ref, gate_ref, h_ref, cn_ref, cnb_ref, m_ref):
    sl = slice(hh * M_HEAD_DIM, (hh + 1) * M_HEAD_DIM)
    q_t = q_ref[0, 0, sl, :]
    k = k_ref[0, :, sl]
    v_t = v_ref[0, 0, sl, :]
    gates = gate_ref[0, 0, hh]
    b_row, g_row, big_row = gates[0:1], gates[1:2], gates[2:3]
    m_prev = m_ref[d, hh, 0:1, 0:1]
    last = TM - 1 if d == 0 else 0
    m_row = jnp.maximum(m_prev, big_row)
    m_last = m_row[:, last:last + 1]
    w_inter = jnp.exp(m_prev - m_row)
    den_floor = jnp.exp(-(b_row + m_row))
    w_key = jnp.exp(g_row - m_last).astype(BF16)
    g_col = jnp.transpose(gates)[:, 1:2]
    jj = lax.broadcasted_iota(jnp.int32, (TM, TM), 0)
    ii = lax.broadcasted_iota(jnp.int32, (TM, TM), 1)
    visible = jj <= ii if d == 0 else jj >= ii
    pad = jnp.zeros((N_ROWS - 1, TM), BF16)

    s_t = _dot(k, q_t) * jnp.where(visible, jnp.exp(g_col - m_row), 0.0)
    cnb_ref[d, hh, :, M_HEAD_DIM:] = jnp.concatenate([jnp.ones((1, TM), BF16), pad, v_t], axis=0)
    rhs = jnp.concatenate([q_t * w_inter.astype(BF16), s_t.astype(BF16)], axis=0)
    both = _dot(cnb_ref[d, hh], rhs)
    scale = 1.0 / jnp.maximum(jnp.abs(both[0:1]), den_floor)
    h_ref[0, :, sl] = jnp.transpose(scale * both[N_ROWS:]).astype(BF16)

    decay = jnp.exp(m_prev - m_last)
    vw = jnp.concatenate([w_key, pad, v_t * w_key], axis=0)
    for c0 in range(0, M_HEAD_DIM, MXU_WIDTH):
        cn = decay * cn_ref[d, hh, :, c0:c0 + MXU_WIDTH] + _dot(vw, k[:, c0:c0 + MXU_WIDTH])
        cn_ref[d, hh, :, c0:c0 + MXU_WIDTH] = cn
        cnb_ref[d, hh, :, c0:c0 + MXU_WIDTH] = cn.astype(BF16)
    m_ref[d, hh] = jnp.zeros(m_ref.shape[2:], F32) + (b_row[:, last:last + 1] + m_last)


def _scan_kernel(qf_ref, kf_ref, vf_ref, gf_ref, qb_ref, kb_ref, vb_ref, gb_ref, hf_ref, hb_ref,
                 cn_ref, cnb_ref, m_ref):
    @pl.when(pl.program_id(2) == 0)
    def _():
        cn_ref[...] = jnp.zeros_like(cn_ref)
        cnb_ref[...] = jnp.zeros_like(cnb_ref)
        m_ref[...] = jnp.zeros_like(m_ref)

    for hh in range(SCAN_HEADS):
        _scan_chunk(0, hh, qf_ref, kf_ref, vf_ref, gf_ref, hf_ref, cn_ref, cnb_ref, m_ref)
        _scan_chunk(1, hh, qb_ref, kb_ref, vb_ref, gb_ref, hb_ref, cn_ref, cnb_ref, m_ref)


def _mlstm_scan(q_t, k, v_t, gates):
    n_batch, rows, _ = k.shape
    n_t = rows // TM
    width = SCAN_HEADS * M_HEAD_DIM

    def bwd_tile(c):
        return jnp.where(c == 0, 0, n_t - c)

    kf_spec = pl.BlockSpec((1, TM, width), lambda b, h, c: (b, c, h))
    kb_spec = pl.BlockSpec((1, TM, width), lambda b, h, c: (b, bwd_tile(c), h))
    tf_spec = pl.BlockSpec((1, 1, width, TM), lambda b, h, c: (b, c, h, 0))
    tb_spec = pl.BlockSpec((1, 1, width, TM), lambda b, h, c: (b, bwd_tile(c), h, 0))
    gf_spec = pl.BlockSpec((1, 1, SCAN_HEADS, GATE_ROWS, TM), lambda b, h, c: (b, 0, h, 0, c))
    gb_spec = pl.BlockSpec((1, 1, SCAN_HEADS, GATE_ROWS, TM), lambda b, h, c: (b, 1, h, 0, bwd_tile(c)))
    out = jax.ShapeDtypeStruct((n_batch, rows, D_INNER), BF16)
    return pl.pallas_call(
        _scan_kernel,
        out_shape=(out, out),
        grid=(n_batch, M_HEADS // SCAN_HEADS, n_t),
        in_specs=[tf_spec, kf_spec, tf_spec, gf_spec, tb_spec, kb_spec, tb_spec, gb_spec],
        out_specs=(kf_spec, kb_spec),
        scratch_shapes=[pltpu.VMEM((2, SCAN_HEADS, STATE_ROWS, M_HEAD_DIM), F32),
                        pltpu.VMEM((2, SCAN_HEADS, STATE_ROWS, M_HEAD_DIM + TM), BF16),
                        pltpu.VMEM((2, SCAN_HEADS, SUBLANES, LANES), F32)],
        compiler_params=_params("parallel", "parallel", "arbitrary"),
        name="mlstm_scan",
    )(q_t, k, v_t, gates, q_t, k, v_t, gates)


def _post_norm(x, y, gate, pg, pb):
    return _ln(ALPHA * x + gate * y) * pg + pb


def _readout_kernel(n_stream, n_sub, ctx_row, hf_ref, hb_ref, xc_ref, z_ref, *refs):
    mod_ref, gn_ref, sk_ref, w_ref, pg_ref, pb_ref, o_ref = refs[n_stream:]
    for r in range(n_sub):
        rs = slice(r * TM, (r + 1) * TM)
        parts = []
        for h in range(M_HEADS):
            sl = slice(h * M_HEAD_DIM, (h + 1) * M_HEAD_DIM)
            hn = _ln(hf_ref[0, rs, sl].astype(F32) + hb_ref[0, rs, sl].astype(F32))
            a = hn * gn_ref[:, sl] + sk_ref[:, sl] * xc_ref[0, rs, sl].astype(F32)
            parts.append((a * _silu(z_ref[0, rs, sl]).astype(F32)).astype(BF16))
        y = _dot(jnp.concatenate(parts, axis=1), w_ref[...])
        m = _tile_mod(mod_ref, ctx_row, r)
        o_ref[0, rs, :] = _post_norm(_stream_tile(refs[:n_stream], n_sub, r), y, _mod_chunk(m, 2),
                                     pg_ref[...], pb_ref[...])


def _mlstm_readout(hf, hb, xc, z, stream, mod_l, gn_w, skip, w_out, pg, pb):
    n_batch, rows = _stream_shape(stream)
    n_sub = _tiles_per_step(rows)
    specs, args = _stream_specs(stream, n_sub)
    wide = pl.BlockSpec((1, n_sub * TM, D_INNER), lambda b, t: (b, t, 0))
    return pl.pallas_call(
        functools.partial(_readout_kernel, len(args), n_sub, n_batch),
        out_shape=jax.ShapeDtypeStruct((n_batch, rows, D_MODEL), F32),
        grid=(n_batch, rows // (n_sub * TM)),
        in_specs=[wide, wide, wide, wide] + specs
        + [_const_spec((MOD_ROWS, 1, 6 * D_MODEL)),
           _const_spec((1, D_INNER)), _const_spec((1, D_INNER)), _const_spec((D_INNER, D_MODEL)),
           _const_spec((1, D_MODEL)), _const_spec((1, D_MODEL))],
        out_specs=pl.BlockSpec((1, n_sub * TM, D_MODEL), lambda b, t: (b, t, 0)),
        compiler_params=_params("parallel", "parallel"),
        name="mlstm_readout",
    )(hf, hb, xc, z, *args, mod_l, gn_w, skip, w_out, pg, pb)


def _fourier_out_kernel(n_sub, ctx_row, *refs):
    mx_refs, refs = refs[:n_sub], refs[n_sub:]
    if ctx_row is not None:
        mc_ref, refs = refs[0], refs[1:]
    x_refs, (mod_ref, w_ref, pg_ref, pb_ref, o_ref) = refs[:n_sub], refs[n_sub:]
    for r in range(n_sub):
        a = jnp.concatenate([mx_refs[r][0, j] for j in range(D_MODEL // LANES)], axis=1)
        if ctx_row is not None and r == 0:
            a = jnp.where(pl.program_id(1) == 0, mc_ref[0], a)
        y = _dot(a.astype(BF16), w_ref[...])
        m = _tile_mod(mod_ref, ctx_row, r)
        o_ref[0, r * TM:(r + 1) * TM, :] = _post_norm(x_refs[r][0], y, _mod_chunk(m, 2), pg_ref[...], pb_ref[...])


def _fourier_out(mixed_x, mixed_c, xs, mod_l, w_out, pg, pb):
    n_batch = xs.shape[0]
    ctx_tile = mixed_c is not None
    ctx_tiles = CTX_LEN // TM
    n_t = mixed_x.shape[2] // TM + (ctx_tiles if ctx_tile else 0)
    n_sub = _tiles_per_step(n_t * TM)
    mixed_off = ctx_tiles if ctx_tile else 0
    x_off = 0 if ctx_tile else ctx_tiles

    def mixed_map(r):
        return lambda b, t: (b, 0, jnp.maximum(n_sub * t + r - mixed_off, 0), 0)

    def x_map(r):
        return lambda b, t: (b, n_sub * t + r + x_off, 0)

    in_specs = [pl.BlockSpec((1, D_MODEL // LANES, TM, LANES), mixed_map(r)) for r in range(n_sub)]
    args = [mixed_x] * n_sub
    if ctx_tile:
        in_specs.append(pl.BlockSpec((1, CTX_LEN, D_MODEL), lambda b, t: (b, 0, 0)))
        args.append(mixed_c)
    in_specs += [pl.BlockSpec((1, TM, D_MODEL), x_map(r)) for r in range(n_sub)]
    args += [xs] * n_sub
    in_specs += [_const_spec((MOD_ROWS, 1, 6 * D_MODEL)),
                 _const_spec((D_MODEL, D_MODEL)), _const_spec((1, D_MODEL)), _const_spec((1, D_MODEL))]
    return pl.pallas_call(
        functools.partial(_fourier_out_kernel, n_sub, n_batch if ctx_tile else None),
        out_shape=jax.ShapeDtypeStruct((n_batch, n_t * TM, D_MODEL), F32),
        grid=(n_batch, n_t // n_sub),
        in_specs=in_specs,
        out_specs=pl.BlockSpec((1, n_sub * TM, D_MODEL), lambda b, t: (b, t, 0)),
        compiler_params=_params("parallel", "parallel"),
        name="fourier_out",
    )(*args, mod_l, w_out, pg, pb)


def _mlp_kernel(n_sub, ctx_row, x_ref, mod_ref, w1_ref, w2_ref, pg_ref, pb_ref, o_ref):
    for r in range(n_sub):
        rs = slice(r * TM, (r + 1) * TM)
        x = x_ref[0, rs, :]
        m = _tile_mod(mod_ref, ctx_row, r)
        h = (_ln(x) * (1.0 + _mod_chunk(m, 4)) + _mod_chunk(m, 3)).astype(BF16)
        acc = jnp.zeros((TM, D_MODEL), F32)
        for j in range(D_FF // D_MODEL):
            sl = slice(j * D_MODEL, (j + 1) * D_MODEL)
            u = jnp.square(jnp.maximum(_dot(h, w1_ref[:, sl]), 0.0)).astype(BF16)
            acc = acc + _dot(u, w2_ref[sl, :])
        o_ref[0, rs, :] = _post_norm(x, acc, _mod_chunk(m, 5), pg_ref[...], pb_ref[...])


def _mlp(xs, mod_l, w1, w2, pg, pb, ctx_tile):
    n_batch, rows, _ = xs.shape
    n_sub = _tiles_per_step(rows)
    blk = pl.BlockSpec((1, n_sub * TM, D_MODEL), lambda b, t: (b, t, 0))
    return pl.pallas_call(
        functools.partial(_mlp_kernel, n_sub, n_batch if ctx_tile else None),
        out_shape=jax.ShapeDtypeStruct((n_batch, rows, D_MODEL), F32),
        grid=(n_batch, rows // (n_sub * TM)),
        in_specs=[blk, _const_spec((MOD_ROWS, 1, 6 * D_MODEL)),
                  _const_spec((D_MODEL, D_FF)), _const_spec((D_FF, D_MODEL)),
                  _const_spec((1, D_MODEL)), _const_spec((1, D_MODEL))],
        out_specs=blk,
        compiler_params=_params("parallel", "parallel"),
        name="mlp",
    )(xs, mod_l, w1, w2, pg, pb)


def _dft_tables():
    two_pi = 2.0 * np.pi
    c = np.arange(F_GROUP_DIM)
    ang = two_pi * ((c[:, None] * c[None, :]) % F_GROUP_DIM) / F_GROUP_DIM
    s_ch = 1.0 / math.sqrt(F_GROUP_DIM)
    f_ch = np.concatenate([np.cos(ang) * s_ch, -np.sin(ang) * s_ch], axis=1)

    seq = FFT_R * FFT_R * FFT_C
    r = np.arange(FFT_R)
    eye = np.eye(FFT_J)

    def tile_mix(w):
        wr, wi = np.kron(w.real, eye), np.kron(w.imag, eye)
        return np.block([[wr, -wi], [wi, wr]])

    w8 = np.exp(-2j * np.pi * ((r[:, None] * r[None, :]) % FFT_R) / FFT_R) / math.sqrt(FFT_R)
    k_a = tile_mix(w8)
    k_b = np.stack([tile_mix(w8 * np.exp(-2j * np.pi * (r[None, :] * a1) / (FFT_R * FFT_R)))
                    for a1 in range(FFT_R)])
    c = np.arange(FFT_C)
    t1 = np.arange(FFT_R * FFT_R)
    num = (c[None, :, None] * c[None, None, :] * (FFT_R * FFT_R) + t1[:, None, None] * c[None, None, :]) % seq
    ang = two_pi * num / seq
    f_c = np.concatenate([np.cos(ang), np.sin(ang)], axis=2) / math.sqrt(FFT_C)

    p = np.arange(CTX_LEN)
    ang = two_pi * ((p[:, None] * p[None, :]) % CTX_LEN) / CTX_LEN
    sc = 1.0 / math.sqrt(CTX_LEN)
    f_ctx = np.concatenate([np.cos(ang) * sc, np.sin(ang) * sc], axis=1)
    return tuple(jnp.asarray(a, dtype=F32).astype(BF16) for a in (f_ch, k_a, k_b, f_c, f_ctx))


def _cdft_kernel(ctx_tile, *refs):
    if ctx_tile:
        x_ref, mod_ref, f_ref, zrx_ref, zix_ref, zrc_ref, zic_ref = refs
    else:
        x_ref, mod_ref, f_ref, zrx_ref, zix_ref = refs
    m = mod_ref[0]
    h = (_ln(x_ref[0]) * (1.0 + _mod_chunk(m, 1)) + _mod_chunk(m, 0)).astype(BF16)
    f = f_ref[...]
    zr, zi = [], []
    for g in range(F_GROUPS):
        z = _dot(h[:, g * F_GROUP_DIM:(g + 1) * F_GROUP_DIM], f).astype(BF16)
        zr.append(z[:, :F_GROUP_DIM])
        zi.append(z[:, F_GROUP_DIM:])
    def store_latent():
        for g in range(F_GROUPS):
            zrx_ref[0, g] = zr[g]
            zix_ref[0, g] = zi[g]

    if ctx_tile:
        t = pl.program_id(1)

        @pl.when(t == 0)
        def _():
            zrc_ref[0] = jnp.concatenate(zr, axis=1)
            zic_ref[0] = jnp.concatenate(zi, axis=1)

        pl.when(t > 0)(store_latent)
    else:
        store_latent()


def _channel_dft(xs, mod_l, f_ch, ctx_tile):
    n_batch, rows, _ = xs.shape
    seq = rows - CTX_LEN
    off = CTX_LEN // TM
    zx = jax.ShapeDtypeStruct((n_batch, F_GROUPS, seq, F_GROUP_DIM), BF16)
    zc = jax.ShapeDtypeStruct((n_batch, CTX_LEN, D_MODEL), BF16)
    if ctx_tile:
        n_t = rows // TM
        x_out = pl.BlockSpec((1, F_GROUPS, TM, F_GROUP_DIM), lambda b, t: (b, 0, jnp.maximum(t - off, 0), 0))
        c_out = pl.BlockSpec((1, CTX_LEN, D_MODEL), lambda b, t: (b, 0, 0))
        out_shape, out_specs, x_in = (zx, zx, zc, zc), (x_out, x_out, c_out, c_out), _tile_spec(D_MODEL)
    else:
        n_t = seq // TM
        x_out = pl.BlockSpec((1, F_GROUPS, TM, F_GROUP_DIM), lambda b, t: (b, 0, t, 0))
        out_shape, out_specs, x_in = (zx, zx), (x_out, x_out), _tile_spec(D_MODEL, off)
    return pl.pallas_call(
        functools.partial(_cdft_kernel, ctx_tile),
        out_shape=out_shape,
        grid=(n_batch, n_t),
        in_specs=[x_in, _mod_spec(n_batch, ctx_tile), _const_spec((F_GROUP_DIM, 2 * F_GROUP_DIM))],
        out_specs=out_specs,
        compiler_params=_params("parallel", "arbitrary"),
        name="fourier_channel_dft",
    )(xs, mod_l, f_ch)


def _seq_dft_kernel(zr_ref, zi_ref, ka_ref, kb_ref, fc_ref, o_ref, ar_ref, ai_ref, br_ref, bi_ref):
    a_rows = FFT_R * FFT_C
    n_tiles = FFT_C // FFT_J

    def mix(load, store, base, step, k):
        offs = [base + i * step for i in range(FFT_R)]
        z = jnp.concatenate([load(0, o) for o in offs] + [load(1, o) for o in offs], axis=0)
        u = _dot(k, z).astype(BF16)
        for i, o in enumerate(offs):
            store(0, o, u[i * FFT_J:(i + 1) * FFT_J])
            store(1, o, u[(FFT_R + i) * FFT_J:(FFT_R + i + 1) * FFT_J])

    def load_in(p, o):
        return (zr_ref, zi_ref)[p][0, 0, pl.ds(o, FFT_J), :]

    def load_a(p, o):
        return (ar_ref, ai_ref)[p][pl.ds(o, FFT_J), :]

    def store_a(p, o, val):
        (ar_ref, ai_ref)[p][pl.ds(o, FFT_J), :] = val

    def store_b(p, o, val):
        (br_ref, bi_ref)[p][pl.ds(o, FFT_J), :] = val

    def stage_a(b, carry):
        for ct in range(n_tiles):
            mix(load_in, store_a, pl.multiple_of(b * FFT_C, FFT_C) + ct * FFT_J, a_rows, ka_ref[...])
        return carry

    lax.fori_loop(0, FFT_R, stage_a, 0)

    def stage_b(a1, carry):
        k = kb_ref[a1]
        for ct in range(n_tiles):
            mix(load_a, store_b, pl.multiple_of(a1 * a_rows, a_rows) + ct * FFT_J, FFT_C, k)
        return carry

    lax.fori_loop(0, FFT_R, stage_b, 0)

    def stage_c(t1, carry):
        a1, b1 = t1 & (FFT_R - 1), lax.shift_right_logical(t1, FFT_R.bit_length() - 1)
        rows = pl.ds(pl.multiple_of(a1 * a_rows + b1 * FFT_C, FFT_C), FFT_C)
        z = jnp.concatenate([br_ref[rows, :], bi_ref[rows, :]], axis=0)
        y = _dot(fc_ref[t1], z)
        for j in range(FFT_CB // LANES):
            o_ref[0, j, pl.ds(t1, FFT_C, stride=FFT_R * FFT_R), :] = y[:, j * LANES:(j + 1) * LANES]
        return carry

    lax.fori_loop(0, FFT_R * FFT_R, stage_c, 0, unroll=DFT_UNROLL)


def _seq_dft(zr, zi, k_a, k_b, f_c):
    n_batch, n_blocks, seq, _ = zr.shape
    blk = pl.BlockSpec((1, 1, seq, FFT_CB), lambda b, j: (b, j, 0, 0))
    return pl.pallas_call(
        _seq_dft_kernel,
        out_shape=jax.ShapeDtypeStruct((n_batch, D_MODEL // LANES, seq, LANES), F32),
        grid=(n_batch, n_blocks),
        in_specs=[blk, blk, _const_spec(k_a.shape), _const_spec(k_b.shape), _const_spec(f_c.shape)],
        out_specs=pl.BlockSpec((1, FFT_CB // LANES, seq, LANES), lambda b, j: (b, j, 0, 0)),
        scratch_shapes=[pltpu.VMEM((seq, FFT_CB), BF16)] * 4,
        compiler_params=_params("parallel", "parallel", vmem_limit=SEQ_DFT_VMEM_LIMIT),
        name="fourier_seq_dft",
    )(zr, zi, k_a, k_b, f_c)


def _ctx_dft_kernel(zr_ref, zi_ref, f_ref, o_ref):
    o_ref[0] = _dot(f_ref[...], jnp.concatenate([zr_ref[0], zi_ref[0]], axis=0))


def _ctx_dft(zr, zi, f_ctx):
    n_batch = zr.shape[0]
    blk = pl.BlockSpec((1, CTX_LEN, D_MODEL), lambda b: (b, 0, 0))
    return pl.pallas_call(
        _ctx_dft_kernel,
        out_shape=jax.ShapeDtypeStruct((n_batch, CTX_LEN, D_MODEL), F32),
        grid=(n_batch,),
        in_specs=[blk, blk, _const_spec((CTX_LEN, 2 * CTX_LEN))],
        out_specs=blk,
        compiler_params=_params("parallel"),
        name="fourier_ctx_dft",
    )(zr, zi, f_ctx)


def _mlstm_layer(xs, mod_l, w_in, w_conv, b_conv, w_q, w_k, w_v, w_gate, b_gate, gn_w, skip, w_out, pg, pb):
    xm, z = _in_proj(xs, mod_l, w_in.astype(BF16))
    wg = w_gate.reshape(3, M_HEADS, M_HEAD_DIM, N_GATES).transpose(1, 0, 3, 2)
    w_qt = jnp.swapaxes(w_q, 1, 2).astype(BF16)
    w_vt = jnp.swapaxes(w_v, 1, 2).astype(BF16)
    xc, q_t, k, v_t, pre_t = _conv_qkv(
        xm, w_conv.reshape(CONV_K * CONV_K, D_INNER), b_conv.reshape(1, D_INNER),
        w_qt, w_k.astype(BF16), w_vt, wg.astype(BF16), b_gate.reshape(N_GATES, 1))
    hf, hb = _mlstm_scan(q_t, k, v_t, _gate_scans(pre_t))
    return _mlstm_readout(hf, hb, xc, z, xs, mod_l, gn_w.reshape(1, D_INNER), skip.reshape(1, D_INNER),
                          w_out.astype(BF16), pg, pb)


def _fourier_layer(xs, mod_l, w_out, pg, pb, need_ctx, tables):
    f_ch, k_a, k_b, f_c, f_ctx = tables
    z = _channel_dft(xs, mod_l, f_ch, need_ctx)
    mixed_x = _seq_dft(z[0], z[1], k_a, k_b, f_c)
    mixed_c = _ctx_dft(z[2], z[3], f_ctx) if need_ctx else None
    return _fourier_out(mixed_x, mixed_c, xs, mod_l, w_out.astype(BF16), pg, pb)


def kernel(x, c, ctx, c_ctx, ada_w, ada_b, post_g, post_b, m_w_in, m_w_conv, m_b_conv, m_w_q, m_w_k, m_w_v, m_w_gate, m_b_gate, m_gn_w, m_skip, m_w_out, f_w_out, mlp_w1, mlp_w2):
    n_batch = x.shape[0]
    assert n_batch < MOD_ROWS and x.shape[1] == FFT_R * FFT_R * FFT_C and ctx.shape[1] == CTX_LEN == TM
    xs = (ctx, x)
    s_in = jnp.concatenate([c, c_ctx[None, :], jnp.zeros((MOD_ROWS - n_batch - 1, D_MODEL), F32)], axis=0)
    mod = _ada_table(s_in, ada_w, ada_b)
    tables = _dft_tables()
    for i in range(DEPTH):
        is_mlstm = i % N_MIXERS == 0
        j = i // N_MIXERS
        need_ctx = i < DEPTH - 1
        mod_l = mod[i].reshape(MOD_ROWS, 1, 6 * D_MODEL)
        pg = post_g[i].reshape(2, 1, D_MODEL)
        pb = post_b[i].reshape(2, 1, D_MODEL)
        if is_mlstm:
            xs = _mlstm_layer(xs, mod_l, m_w_in[j], m_w_conv[j], m_b_conv[j], m_w_q[j], m_w_k[j], m_w_v[j],
                              m_w_gate[j], m_b_gate[j], m_gn_w[j], m_skip[j], m_w_out[j], pg[0], pb[0])
            if not need_ctx:
                xs = xs[:, CTX_LEN:]
        else:
            xs = _fourier_layer(xs, mod_l, f_w_out[j], pg[0], pb[0], need_ctx, tables)
        xs = _mlp(xs, mod_l, mlp_w1[i].astype(BF16), mlp_w2[i].astype(BF16), pg[1], pb[1], need_ctx)
    return xs
```

```python
import functools
import math

import numpy as np
import jax
import jax.numpy as jnp
from jax import lax
from jax.experimental import pallas as pl
from jax.experimental.pallas import tpu as pltpu

D_MODEL = 1024
DEPTH = 4
GRID_W = 64
CTX_LEN = 256
N_MIXERS = 2
D_INNER = 2 * D_MODEL
M_HEADS = 4
M_HEAD_DIM = D_INNER // M_HEADS
CONV_K = 3
F_GROUPS = 4
F_GROUP_DIM = D_MODEL // F_GROUPS
D_FF = 4 * D_MODEL
ALPHA = float((2 * DEPTH) ** 0.25)
LN_EPS = 1e-5

LANES = 128
SUBLANES = 8
BF16_ROWS = 16
MXU_WIDTH = 256

TM = 256
N_GATES = 4 * M_HEADS
GATE_ROWS = SUBLANES
MOD_ROWS = SUBLANES
FFT_R = 8
FFT_C = 128
FFT_J = BF16_ROWS
FFT_CB = F_GROUP_DIM
GATE_CHUNKS = 3
N_ROWS = BF16_ROWS
STATE_ROWS = N_ROWS + M_HEAD_DIM
SCAN_HEADS = 2
DFT_UNROLL = 8
VMEM_LIMIT = 52 * 1024 * 1024
SEQ_DFT_VMEM_LIMIT = 58 * 1024 * 1024

F32 = jnp.float32
BF16 = jnp.bfloat16


def _params(*sem, vmem_limit=VMEM_LIMIT):
    return pltpu.CompilerParams(dimension_semantics=sem, vmem_limit_bytes=vmem_limit)


def _ln(x):
    mu = jnp.mean(x, axis=-1, keepdims=True)
    xc = x - mu
    var = jnp.mean(xc * xc, axis=-1, keepdims=True)
    return xc * lax.rsqrt(var + LN_EPS)


def _silu(x):
    t = 0.5 * x
    return t + t * jnp.tanh(t)


def _dot(a, b):
    return jnp.dot(a, b, preferred_element_type=F32)


def _dot_nt(a, b):
    return lax.dot_general(a, b, (((1,), (1,)), ((), ())), preferred_element_type=F32)


def _mod_chunk(m, j):
    return m[:, j * D_MODEL:(j + 1) * D_MODEL]


def _const_spec(shape):
    nd = len(shape)
    return pl.BlockSpec(shape, lambda *_: (0,) * nd, pipeline_mode=pl.Buffered(1))


def _stream_specs(stream, n_sub):
    split = isinstance(stream, tuple)
    off = CTX_LEN // TM if split else 0

    def tile_map(r):
        return lambda b, t: (b, jnp.maximum(n_sub * t + r - off, 0), 0)

    specs = [pl.BlockSpec((1, TM, D_MODEL), tile_map(r)) for r in range(n_sub)]
    if split:
        return [pl.BlockSpec((1, CTX_LEN, D_MODEL), lambda b, t: (b, 0, 0))] + specs, [stream[0]] + [stream[1]] * n_sub
    return specs, [stream] * n_sub


def _tiles_per_step(rows):
    n_t = rows // TM
    n_sub = 3 if n_t % 3 == 0 else 4
    assert rows == n_t * TM and n_t % n_sub == 0, rows
    return n_sub


def _tile_mod(mod_ref, ctx_row, r):
    m = mod_ref[pl.program_id(0)]
    if ctx_row is not None and r == 0:
        m = jnp.where(pl.program_id(1) == 0, mod_ref[ctx_row], m)
    return m


def _stream_shape(stream):
    if isinstance(stream, tuple):
        return stream[1].shape[0], stream[0].shape[1] + stream[1].shape[1]
    return stream.shape[0], stream.shape[1]


def _stream_tile(refs, n_sub, r):
    if len(refs) > n_sub:
        x = refs[1 + r][0]
        return jnp.where(pl.program_id(1) == 0, refs[0][0], x) if r == 0 else x
    return refs[r][0]


def _ada_kernel(s_ref, w_ref, b_ref, o_ref):
    s = _silu(s_ref[...])
    o_ref[0] = jnp.dot(s, w_ref[0], preferred_element_type=F32, precision=lax.Precision.HIGHEST) + b_ref[0]


def _ada_table(s_in, ada_w, ada_b):
    n_chunks = 6
    return pl.pallas_call(
        _ada_kernel,
        out_shape=jax.ShapeDtypeStruct((DEPTH, MOD_ROWS, 6 * D_MODEL), F32),
        grid=(DEPTH, n_chunks),
        in_specs=[pl.BlockSpec((MOD_ROWS, D_MODEL), lambda l, j: (0, 0)),
                  pl.BlockSpec((1, D_MODEL, D_MODEL), lambda l, j: (l, 0, j)),
                  pl.BlockSpec((1, 1, D_MODEL), lambda l, j: (l, 0, j))],
        out_specs=pl.BlockSpec((1, MOD_ROWS, D_MODEL), lambda l, j: (l, 0, j)),
        compiler_params=_params("parallel", "parallel"),
        name="ada_table",
    )(s_in, ada_w, ada_b.reshape(DEPTH, 1, 6 * D_MODEL))


def _in_kernel(n_stream, n_sub, ctx_row, *refs):
    mod_ref, w_ref, xm_ref, z_ref = refs[n_stream:]
    for r in range(n_sub):
        rs = slice(r * TM, (r + 1) * TM)
        x = _stream_tile(refs[:n_stream], n_sub, r)
        m = _tile_mod(mod_ref, ctx_row, r)
        h = (_ln(x) * (1.0 + _mod_chunk(m, 1)) + _mod_chunk(m, 0)).astype(BF16)
        for j in range(2):
            sl = slice(j * D_MODEL, (j + 1) * D_MODEL)
            xm_ref[0, rs, sl] = _dot(h, w_ref[:, sl]).astype(BF16)
            z_ref[0, rs, sl] = _dot(h, w_ref[:, D_INNER + j * D_MODEL:D_INNER + (j + 1) * D_MODEL]).astype(BF16)


def _in_proj(stream, mod_l, w_in):
    n_batch, rows = _stream_shape(stream)
    n_sub = _tiles_per_step(rows)
    specs, args = _stream_specs(stream, n_sub)
    out = jax.ShapeDtypeStruct((n_batch, rows, D_INNER), BF16)
    out_spec = pl.BlockSpec((1, n_sub * TM, D_INNER), lambda b, t: (b, t, 0))
    return pl.pallas_call(
        functools.partial(_in_kernel, len(args), n_sub, n_batch),
        out_shape=(out, out),
        grid=(n_batch, rows // (n_sub * TM)),
        in_specs=specs + [_const_spec((MOD_ROWS, 1, 6 * D_MODEL)), _const_spec((D_MODEL, 2 * D_INNER))],
        out_specs=(out_spec, out_spec),
        compiler_params=_params("parallel", "parallel"),
        name="mlstm_in_proj",
    )(*args, mod_l, w_in)


def _conv_qkv_kernel(n_sub, xm_ref, prev_ref, next_ref, wc_ref, bc_ref, wq_ref, wk_ref, wv_ref, wg_ref, bg_ref,
                     xc_ref, q_ref, k_ref, v_ref, pre_ref):
    n_t = pl.num_programs(1) * n_sub
    row = lax.broadcasted_iota(jnp.int32, (TM, 1), 0)
    for r in range(n_sub):
        t = pl.program_id(1) * n_sub + r
        rs = slice(r * TM, (r + 1) * TM)
        is_ctx = t == 0
        period = jnp.where(is_ctx, CTX_LEN - 1, GRID_W - 1)
        pos = row & period
        has_left = pos != 0
        has_right = pos != period
        prev_ok = t >= 2
        next_ok = jnp.logical_and(t >= 1, t <= n_t - 2)
        row_w = jnp.where(is_ctx, 0.0, 1.0)
        pre_t = jnp.zeros((N_GATES, TM), F32) + bg_ref[...]
        for h in range(M_HEADS):
            sl = slice(h * M_HEAD_DIM, (h + 1) * M_HEAD_DIM)
            xm = xm_ref[0, rs, sl]
            prev = prev_ref[0, :, sl] if r == 0 else xm_ref[0, r * TM - GRID_W:r * TM, sl]
            nxt = next_ref[0, :, sl] if r == n_sub - 1 else xm_ref[0, (r + 1) * TM:(r + 1) * TM + GRID_W, sl]
            e = jnp.concatenate([jnp.where(prev_ok, prev, 0), xm, jnp.where(next_ok, nxt, 0)], axis=0)
            p = []
            for dc in range(CONV_K):
                acc = None
                for dr in range(CONV_K):
                    w = wc_ref[CONV_K * dr + dc:CONV_K * dr + dc + 1, sl]
                    if dr != CONV_K // 2:
                        w = w * row_w
                    term = w.astype(BF16) * e[dr * GRID_W:dr * GRID_W + TM]
                    acc = term if acc is None else acc + term
                p.append(acc.astype(F32))
            y = (p[1] + bc_ref[:, sl]
                 + jnp.where(has_left, pltpu.roll(p[0], 1, axis=0), 0.0)
                 + jnp.where(has_right, pltpu.roll(p[2], TM - 1, axis=0), 0.0))
            xc = _silu(y).astype(BF16)
            xc_ref[0, rs, sl] = xc
            q_t = _dot_nt(wq_ref[h], xc).astype(BF16)
            k = (_dot(xc, wk_ref[h]) * (M_HEAD_DIM ** -0.5)).astype(BF16)
            v_t = _dot_nt(wv_ref[h], xm).astype(BF16)
            q_ref[0, r, sl, :] = q_t
            k_ref[0, rs, sl] = k
            v_ref[0, r, sl, :] = v_t
            pre_t = pre_t + _dot(wg_ref[h, 0], q_t) + _dot_nt(wg_ref[h, 1], k) + _dot(wg_ref[h, 2], v_t)
        pre_ref[0, :, rs] = pre_t


def _conv_qkv(xm, w_conv, b_conv, w_q, w_k, w_v, w_g, b_g):
    n_batch, rows, _ = xm.shape
    n_sub = _tiles_per_step(rows)
    n_halo = rows // GRID_W
    per = n_sub * TM // GRID_W
    wide = jax.ShapeDtypeStruct((n_batch, rows, D_INNER), BF16)
    wide_t = jax.ShapeDtypeStruct((n_batch, rows // TM, D_INNER, TM), BF16)
    row_spec = pl.BlockSpec((1, n_sub * TM, D_INNER), lambda b, t: (b, t, 0))
    col_spec = pl.BlockSpec((1, n_sub, D_INNER, TM), lambda b, t: (b, t, 0, 0))
    return pl.pallas_call(
        functools.partial(_conv_qkv_kernel, n_sub),
        out_shape=(wide, wide_t, wide, wide_t, jax.ShapeDtypeStruct((n_batch, N_GATES, rows), F32)),
        grid=(n_batch, rows // (n_sub * TM)),
        in_specs=[row_spec,
                  pl.BlockSpec((1, GRID_W, D_INNER), lambda b, t: (b, jnp.maximum(per * t - 1, 0), 0)),
                  pl.BlockSpec((1, GRID_W, D_INNER), lambda b, t: (b, jnp.minimum(per * t + per, n_halo - 1), 0)),
                  _const_spec((CONV_K * CONV_K, D_INNER)),
                  _const_spec((1, D_INNER)),
                  _const_spec((M_HEADS, M_HEAD_DIM, M_HEAD_DIM)),
                  _const_spec((M_HEADS, M_HEAD_DIM, M_HEAD_DIM)),
                  _const_spec((M_HEADS, M_HEAD_DIM, M_HEAD_DIM)),
                  _const_spec((M_HEADS, 3, N_GATES, M_HEAD_DIM)),
                  _const_spec((N_GATES, 1))],
        out_specs=(row_spec, col_spec, row_spec, col_spec,
                   pl.BlockSpec((1, N_GATES, n_sub * TM), lambda b, t: (b, 0, t))),
        compiler_params=_params("parallel", "parallel"),
        name="mlstm_conv_qkv",
    )(xm, xm, xm, w_conv, b_conv, w_q, w_k, w_v, w_g, b_g)


def _scan_lanes(x, combine, fill, reverse):
    n = x.shape[-1]
    lane = lax.broadcasted_iota(jnp.int32, x.shape, 1)
    s = 1
    while s < n:
        if reverse:
            sh = jnp.where(lane < n - s, pltpu.roll(x, n - s, axis=1), fill)
        else:
            sh = jnp.where(lane >= s, pltpu.roll(x, s, axis=1), fill)
        x = combine(x, sh)
        s *= 2
    return x


def _gate_kernel(pre_ref, o_ref):
    zeros = jnp.zeros((GATE_ROWS - 3, TM), F32)
    for c in range(GATE_CHUNKS):
        lanes = slice(c * TM, (c + 1) * TM)
        for d in range(2):
            blk = pre_ref[0, 2 * M_HEADS * d:2 * M_HEADS * (d + 1), lanes]
            log_f = jnp.minimum(blk, 0.0) - jnp.log1p(jnp.exp(-jnp.abs(blk)))
            b = pltpu.roll(_scan_lanes(log_f, jnp.add, 0.0, reverse=d == 1), M_HEADS, axis=0)
            g = blk - b
            big = _scan_lanes(g, jnp.maximum, -jnp.inf, reverse=d == 1)
            for h in range(M_HEADS):
                o_ref[0, d, h, :, lanes] = jnp.concatenate([b[h:h + 1], g[h:h + 1], big[h:h + 1], zeros], axis=0)


def _gate_scans(pre_t):
    n_batch, _, rows = pre_t.shape
    width = GATE_CHUNKS * TM
    return pl.pallas_call(
        _gate_kernel,
        out_shape=jax.ShapeDtypeStruct((n_batch, 2, M_HEADS, GATE_ROWS, rows), F32),
        grid=(n_batch, rows // width),
        in_specs=[pl.BlockSpec((1, N_GATES, width), lambda b, t: (b, 0, t))],
        out_specs=pl.BlockSpec((1, 2, M_HEADS, GATE_ROWS, width), lambda b, t: (b, 0, 0, 0, t)),
        compiler_params=_params("parallel", "parallel"),
        name="mlstm_gate_scans",
    )(pre_t)


def _scan_chunk(d, hh, q_ref, k_ref, v_ref, gate_ref, h_ref, cn_ref, cnb_ref, m_ref):
    sl = slice(hh * M_HEAD_DIM, (hh + 1) * M_HEAD_DIM)
    q_t = q_ref[0, 0, sl, :]
    k = k_ref[0, :, sl]
    v_t = v_ref[0, 0, sl, :]
    gates = gate_ref[0, 0, hh]
    b_row, g_row, big_row = gates[0:1], gates[1:2], gates[2:3]
    m_prev = m_ref[d, hh, 0:1, 0:1]
    last = TM - 1 if d == 0 else 0
    m_row = jnp.maximum(m_prev, big_row)
    m_last = m_row[:, last:last + 1]
    w_inter = jnp.exp(m_prev - m_row)
    den_floor = jnp.exp(-(b_row + m_row))
    w_key = jnp.exp(g_row - m_last).astype(BF16)
    g_col = jnp.transpose(gates)[:, 1:2]
    jj = lax.broadcasted_iota(jnp.int32, (TM, TM), 0)
    ii = lax.broadcasted_iota(jnp.int32, (TM, TM), 1)
    visible = jj <= ii if d == 0 else jj >= ii
    pad = jnp.zeros((N_ROWS - 1, TM), BF16)

    s_t = _dot(k, q_t) * jnp.where(visible, jnp.exp(g_col - m_row), 0.0)
    cnb_ref[d, hh, :, M_HEAD_DIM:] = jnp.concatenate([jnp.ones((1, TM), BF16), pad, v_t], axis=0)
    rhs = jnp.concatenate([q_t * w_inter.astype(BF16), s_t.astype(BF16)], axis=0)
    both = _dot(cnb_ref[d, hh], rhs)
    scale = 1.0 / jnp.maximum(jnp.abs(both[0:1]), den_floor)
    h_ref[0, :, sl] = jnp.transpose(scale * both[N_ROWS:]).astype(BF16)

    decay = jnp.exp(m_prev - m_last)
    vw = jnp.concatenate([w_key, pad, v_t * w_key], axis=0)
    for c0 in range(0, M_HEAD_DIM, MXU_WIDTH):
        cn = decay * cn_ref[d, hh, :, c0:c0 + MXU_WIDTH] + _dot(vw, k[:, c0:c0 + MXU_WIDTH])
        cn_ref[d, hh, :, c0:c0 + MXU_WIDTH] = cn
        cnb_ref[d, hh, :, c0:c0 + MXU_WIDTH] = cn.astype(BF16)
    m_ref[d, hh] = jnp.zeros(m_ref.shape[2:], F32) + (b_row[:, last:last + 1] + m_last)


def _scan_kernel(qf_ref, kf_ref, vf_ref, gf_ref, qb_ref, kb_ref, vb_ref, gb_ref, hf_ref, hb_ref,
                 cn_ref, cnb_ref, m_ref):
    @pl.when(pl.program_id(2) == 0)
    def _():
        cn_ref[...] = jnp.zeros_like(cn_ref)
        cnb_ref[...] = jnp.zeros_like(cnb_ref)
        m_ref[...] = jnp.zeros_like(m_ref)

    for hh in range(SCAN_HEADS):
        _scan_chunk(0, hh, qf_ref, kf_ref, vf_ref, gf_ref, hf_ref, cn_ref, cnb_ref, m_ref)
        _scan_chunk(1, hh, qb_ref, kb_ref, vb_ref, gb_ref, hb_ref, cn_ref, cnb_ref, m_ref)


def _mlstm_scan(q_t, k, v_t, gates):
    n_batch, rows, _ = k.shape
    n_t = rows // TM
    width = SCAN_HEADS * M_HEAD_DIM

    def bwd_tile(c):
        return jnp.where(c == 0, 0, n_t - c)

    kf_spec = pl.BlockSpec((1, TM, width), lambda b, h, c: (b, c, h))
    kb_spec = pl.BlockSpec((1, TM, width), lambda b, h, c: (b, bwd_tile(c), h))
    tf_spec = pl.BlockSpec((1, 1, width, TM), lambda b, h, c: (b, c, h, 0))
    tb_spec = pl.BlockSpec((1, 1, width, TM), lambda b, h, c: (b, bwd_tile(c), h, 0))
    gf_spec = pl.BlockSpec((1, 1, SCAN_HEADS, GATE_ROWS, TM), lambda b, h, c: (b, 0, h, 0, c))
    gb_spec = pl.BlockSpec((1, 1, SCAN_HEADS, GATE_ROWS, TM), lambda b, h, c: (b, 1, h, 0, bwd_tile(c)))
    out = jax.ShapeDtypeStruct((n_batch, rows, D_INNER), BF16)
    return pl.pallas_call(
        _scan_kernel,
        out_shape=(out, out),
        grid=(n_batch, M_HEADS // SCAN_HEADS, n_t),
        in_specs=[tf_spec, kf_spec, tf_spec, gf_spec, tb_spec, kb_spec, tb_spec, gb_spec],
        out_specs=(kf_spec, kb_spec),
        scratch_shapes=[pltpu.VMEM((2, SCAN_HEADS, STATE_ROWS, M_HEAD_DIM), F32),
                        pltpu.VMEM((2, SCAN_HEADS, STATE_ROWS, M_HEAD_DIM + TM), BF16),
                        pltpu.VMEM((2, SCAN_HEADS, SUBLANES, LANES), F32)],
        compiler_params=_params("parallel", "parallel", "arbitrary"),
        name="mlstm_scan",
    )(q_t, k, v_t, gates, q_t, k, v_t, gates)


def _post_norm(x, y, gate, pg, pb):
    return _ln(ALPHA * x + gate * y) * pg + pb


def _readout_kernel(n_stream, n_sub, ctx_row, hf_ref, hb_ref, xc_ref, z_ref, *refs):
    mod_ref, gn_ref, sk_ref, w_ref, pg_ref, pb_ref, o_ref = refs[n_stream:]
    for r in range(n_sub):
        rs = slice(r * TM, (r + 1) * TM)
        parts = []
        for h in range(M_HEADS):
            sl = slice(h * M_HEAD_DIM, (h + 1) * M_HEAD_DIM)
            hn = _ln(hf_ref[0, rs, sl].astype(F32) + hb_ref[0, rs, sl].astype(F32))
            a = hn * gn_ref[:, sl] + sk_ref[:, sl] * xc_ref[0, rs, sl].astype(F32)
            parts.append((a * _silu(z_ref[0, rs, sl]).astype(F32)).astype(BF16))
        y = _dot(jnp.concatenate(parts, axis=1), w_ref[...])
        m = _tile_mod(mod_ref, ctx_row, r)
        o_ref[0, rs, :] = _post_norm(_stream_tile(refs[:n_stream], n_sub, r), y, _mod_chunk(m, 2),
                                     pg_ref[...], pb_ref[...])


def _mlstm_readout(hf, hb, xc, z, stream, mod_l, gn_w, skip, w_out, pg, pb):
    n_batch, rows = _stream_shape(stream)
    n_sub = _tiles_per_step(rows)
    specs, args = _stream_specs(stream, n_sub)
    wide = pl.BlockSpec((1, n_sub * TM, D_INNER), lambda b, t: (b, t, 0))
    return pl.pallas_call(
        functools.partial(_readout_kernel, len(args), n_sub, n_batch),
        out_shape=jax.ShapeDtypeStruct((n_batch, rows, D_MODEL), F32),
        grid=(n_batch, rows // (n_sub * TM)),
        in_specs=[wide, wide, wide, wide] + specs
        + [_const_spec((MOD_ROWS, 1, 6 * D_MODEL)),
           _const_spec((1, D_INNER)), _const_spec((1, D_INNER)), _const_spec((D_INNER, D_MODEL)),
           _const_spec((1, D_MODEL)), _const_spec((1, D_MODEL))],
        out_specs=pl.BlockSpec((1, n_sub * TM, D_MODEL), lambda b, t: (b, t, 0)),
        compiler_params=_params("parallel", "parallel"),
        name="mlstm_readout",
    )(hf, hb, xc, z, *args, mod_l, gn_w, skip, w_out, pg, pb)


def _fourier_out_kernel(n_sub, ctx_row, *refs):
    mx_refs, refs = refs[:n_sub], refs[n_sub:]
    if ctx_row is not None:
        mc_ref, refs = refs[0], refs[1:]
    x_refs, (mod_ref, w_ref, pg_ref, pb_ref, o_ref) = refs[:n_sub], refs[n_sub:]
    for r in range(n_sub):
        a = jnp.concatenate([mx_refs[r][0, j] for j in range(D_MODEL // LANES)], axis=1)
        if ctx_row is not None and r == 0:
            a = jnp.where(pl.program_id(1) == 0, mc_ref[0], a)
        y = _dot(a.astype(BF16), w_ref[...])
        m = _tile_mod(mod_ref, ctx_row, r)
        o_ref[0, r * TM:(r + 1) * TM, :] = _post_norm(x_refs[r][0], y, _mod_chunk(m, 2), pg_ref[...], pb_ref[...])


def _fourier_out(mixed_x, mixed_c, xs, mod_l, w_out, pg, pb):
    n_batch = xs.shape[0]
    ctx_tile = mixed_c is not None
    ctx_tiles = CTX_LEN // TM
    n_t = mixed_x.shape[2] // TM + (ctx_tiles if ctx_tile else 0)
    n_sub = _tiles_per_step(n_t * TM)
    mixed_off = ctx_tiles if ctx_tile else 0
    x_off = 0 if ctx_tile else ctx_tiles

    def mixed_map(r):
        return lambda b, t: (b, 0, jnp.maximum(n_sub * t + r - mixed_off, 0), 0)

    def x_map(r):
        return lambda b, t: (b, n_sub * t + r + x_off, 0)

    in_specs = [pl.BlockSpec((1, D_MODEL // LANES, TM, LANES), mixed_map(r)) for r in range(n_sub)]
    args = [mixed_x] * n_sub
    if ctx_tile:
        in_specs.append(pl.BlockSpec((1, CTX_LEN, D_MODEL), lambda b, t: (b, 0, 0)))
        args.append(mixed_c)
    in_specs += [pl.BlockSpec((1, TM, D_MODEL), x_map(r)) for r in range(n_sub)]
    args += [xs] * n_sub
    in_specs += [_const_spec((MOD_ROWS, 1, 6 * D_MODEL)),
                 _const_spec((D_MODEL, D_MODEL)), _const_spec((1, D_MODEL)), _const_spec((1, D_MODEL))]
    return pl.pallas_call(
        functools.partial(_fourier_out_kernel, n_sub, n_batch if ctx_tile else None),
        out_shape=jax.ShapeDtypeStruct((n_batch, n_t * TM, D_MODEL), F32),
        grid=(n_batch, n_t // n_sub),
        in_specs=in_specs,
        out_specs=pl.BlockSpec((1, n_sub * TM, D_MODEL), lambda b, t: (b, t, 0)),
        compiler_params=_params("parallel", "parallel"),
        name="fourier_out",
    )(*args, mod_l, w_out, pg, pb)


def _channel_dft_tile(x, m, f_ref, zr_ref, zi_ref, rs):
    h = (_ln(x) * (1.0 + _mod_chunk(m, 1)) + _mod_chunk(m, 0)).astype(BF16)
    f = f_ref[...]
    for g in range(F_GROUPS):
        z = _dot(h[:, g * F_GROUP_DIM:(g + 1) * F_GROUP_DIM], f).astype(BF16)
        zr_ref[0, g, rs, :] = z[:, :F_GROUP_DIM]
        zi_ref[0, g, rs, :] = z[:, F_GROUP_DIM:]


def _mlp_kernel(n_sub, ctx_row, fourier_next, x_ref, mod_ref, w1_ref, w2_ref, pg_ref, pb_ref, *refs):
    if fourier_next:
        mod2_ref, f_ref, o_ref, zr_ref, zi_ref = refs
    else:
        o_ref, = refs
    for r in range(n_sub):
        rs = slice(r * TM, (r + 1) * TM)
        x = x_ref[0, rs, :]
        m = _tile_mod(mod_ref, ctx_row, r)
        h = (_ln(x) * (1.0 + _mod_chunk(m, 4)) + _mod_chunk(m, 3)).astype(BF16)
        acc = jnp.zeros((TM, D_MODEL), F32)
        for j in range(D_FF // D_MODEL):
            sl = slice(j * D_MODEL, (j + 1) * D_MODEL)
            u = jnp.square(jnp.maximum(_dot(h, w1_ref[:, sl]), 0.0)).astype(BF16)
            acc = acc + _dot(u, w2_ref[sl, :])
        y = _post_norm(x, acc, _mod_chunk(m, 5), pg_ref[...], pb_ref[...])
        o_ref[0, rs, :] = y
        if fourier_next:
            _channel_dft_tile(y, _tile_mod(mod2_ref, ctx_row, r), f_ref, zr_ref, zi_ref, rs)


def _mlp(xs, mod_l, w1, w2, pg, pb, ctx_tile, fourier_next=None):
    n_batch, rows, _ = xs.shape
    n_sub = _tiles_per_step(rows)
    blk = pl.BlockSpec((1, n_sub * TM, D_MODEL), lambda b, t: (b, t, 0))
    in_specs = [blk, _const_spec((MOD_ROWS, 1, 6 * D_MODEL)),
                _const_spec((D_MODEL, D_FF)), _const_spec((D_FF, D_MODEL)),
                _const_spec((1, D_MODEL)), _const_spec((1, D_MODEL))]
    args = [xs, mod_l, w1, w2, pg, pb]
    out_shape = jax.ShapeDtypeStruct((n_batch, rows, D_MODEL), F32)
    out_specs = blk
    if fourier_next is not None:
        in_specs += [_const_spec((MOD_ROWS, 1, 6 * D_MODEL)), _const_spec((F_GROUP_DIM, 2 * F_GROUP_DIM))]
        args += list(fourier_next)
        plane = jax.ShapeDtypeStruct((n_batch, F_GROUPS, rows, F_GROUP_DIM), BF16)
        plane_spec = pl.BlockSpec((1, F_GROUPS, n_sub * TM, F_GROUP_DIM), lambda b, t: (b, 0, t, 0))
        out_shape, out_specs = (out_shape, plane, plane), (blk, plane_spec, plane_spec)
    return pl.pallas_call(
        functools.partial(_mlp_kernel, n_sub, n_batch if ctx_tile else None, fourier_next is not None),
        out_shape=out_shape,
        grid=(n_batch, rows // (n_sub * TM)),
        in_specs=in_specs,
        out_specs=out_specs,
        compiler_params=_params("parallel", "parallel"),
        name="mlp",
    )(*args)


def _dft_tables():
    two_pi = 2.0 * np.pi
    c = np.arange(F_GROUP_DIM)
    ang = two_pi * ((c[:, None] * c[None, :]) % F_GROUP_DIM) / F_GROUP_DIM
    s_ch = 1.0 / math.sqrt(F_GROUP_DIM)
    f_ch = np.concatenate([np.cos(ang) * s_ch, -np.sin(ang) * s_ch], axis=1)

    seq = FFT_R * FFT_R * FFT_C
    r = np.arange(FFT_R)
    eye = np.eye(FFT_J)

    def tile_mix(w):
        wr, wi = np.kron(w.real, eye), np.kron(w.imag, eye)
        return np.block([[wr, -wi], [wi, wr]])

    w8 = np.exp(-2j * np.pi * ((r[:, None] * r[None, :]) % FFT_R) / FFT_R) / math.sqrt(FFT_R)
    k_a = tile_mix(w8)
    k_b = np.stack([tile_mix(w8 * np.exp(-2j * np.pi * (r[None, :] * a1) / (FFT_R * FFT_R)))
                    for a1 in range(FFT_R)])
    c = np.arange(FFT_C)
    t1 = np.arange(FFT_R * FFT_R)
    num = (c[None, :, None] * c[None, None, :] * (FFT_R * FFT_R) + t1[:, None, None] * c[None, None, :]) % seq
    ang = two_pi * num / seq
    f_c = np.concatenate([np.cos(ang), np.sin(ang)], axis=2) / math.sqrt(FFT_C)

    p = np.arange(CTX_LEN)
    ang = two_pi * ((p[:, None] * p[None, :]) % CTX_LEN) / CTX_LEN
    sc = 1.0 / math.sqrt(CTX_LEN)
    f_ctx = np.concatenate([np.cos(ang) * sc, np.sin(ang) * sc], axis=1)
    return tuple(jnp.asarray(a, dtype=F32).astype(BF16) for a in (f_ch, k_a, k_b, f_c, f_ctx))


def _seq_dft_kernel(zr_ref, zi_ref, ka_ref, kb_ref, fc_ref, o_ref, ar_ref, ai_ref, br_ref, bi_ref):
    a_rows = FFT_R * FFT_C
    n_tiles = FFT_C // FFT_J

    def mix(load, store, base, step, k):
        offs = [base + i * step for i in range(FFT_R)]
        z = jnp.concatenate([load(0, o) for o in offs] + [load(1, o) for o in offs], axis=0)
        u = _dot(k, z).astype(BF16)
        for i, o in enumerate(offs):
            store(0, o, u[i * FFT_J:(i + 1) * FFT_J])
            store(1, o, u[(FFT_R + i) * FFT_J:(FFT_R + i + 1) * FFT_J])

    def load_in(p, o):
        return (zr_ref, zi_ref)[p][0, 0, pl.ds(CTX_LEN + o, FFT_J), :]

    def load_a(p, o):
        return (ar_ref, ai_ref)[p][pl.ds(o, FFT_J), :]

    def store_a(p, o, val):
        (ar_ref, ai_ref)[p][pl.ds(o, FFT_J), :] = val

    def store_b(p, o, val):
        (br_ref, bi_ref)[p][pl.ds(o, FFT_J), :] = val

    def stage_a(b, carry):
        for ct in range(n_tiles):
            mix(load_in, store_a, pl.multiple_of(b * FFT_C, FFT_C) + ct * FFT_J, a_rows, ka_ref[...])
        return carry

    lax.fori_loop(0, FFT_R, stage_a, 0)

    def stage_b(a1, carry):
        k = kb_ref[a1]
        for ct in range(n_tiles):
            mix(load_a, store_b, pl.multiple_of(a1 * a_rows, a_rows) + ct * FFT_J, FFT_C, k)
        return carry

    lax.fori_loop(0, FFT_R, stage_b, 0)

    def stage_c(t1, carry):
        a1, b1 = t1 & (FFT_R - 1), lax.shift_right_logical(t1, FFT_R.bit_length() - 1)
        rows = pl.ds(pl.multiple_of(a1 * a_rows + b1 * FFT_C, FFT_C), FFT_C)
        z = jnp.concatenate([br_ref[rows, :], bi_ref[rows, :]], axis=0)
        y = _dot(fc_ref[t1], z)
        for j in range(FFT_CB // LANES):
            o_ref[0, j, pl.ds(t1, FFT_C, stride=FFT_R * FFT_R), :] = y[:, j * LANES:(j + 1) * LANES]
        return carry

    lax.fori_loop(0, FFT_R * FFT_R, stage_c, 0, unroll=DFT_UNROLL)


def _seq_dft(zr, zi, k_a, k_b, f_c):
    n_batch, n_blocks, rows, _ = zr.shape
    seq = rows - CTX_LEN
    blk = pl.BlockSpec((1, 1, rows, FFT_CB), lambda b, j: (b, j, 0, 0))
    return pl.pallas_call(
        _seq_dft_kernel,
        out_shape=jax.ShapeDtypeStruct((n_batch, D_MODEL // LANES, seq, LANES), F32),
        grid=(n_batch, n_blocks),
        in_specs=[blk, blk, _const_spec(k_a.shape), _const_spec(k_b.shape), _const_spec(f_c.shape)],
        out_specs=pl.BlockSpec((1, FFT_CB // LANES, seq, LANES), lambda b, j: (b, j, 0, 0)),
        scratch_shapes=[pltpu.VMEM((seq, FFT_CB), BF16)] * 4,
        compiler_params=_params("parallel", "parallel", vmem_limit=SEQ_DFT_VMEM_LIMIT),
        name="fourier_seq_dft",
    )(zr, zi, k_a, k_b, f_c)


def _ctx_dft_kernel(zr_ref, zi_ref, f_ref, o_ref):
    planes = [jnp.concatenate([z_ref[0, g] for g in range(F_GROUPS)], axis=1) for z_ref in (zr_ref, zi_ref)]
    o_ref[0] = _dot(f_ref[...], jnp.concatenate(planes, axis=0))


def _ctx_dft(zr, zi, f_ctx):
    n_batch = zr.shape[0]
    blk = pl.BlockSpec((1, CTX_LEN, D_MODEL), lambda b: (b, 0, 0))
    planes = pl.BlockSpec((1, F_GROUPS, CTX_LEN, F_GROUP_DIM), lambda b: (b, 0, 0, 0))
    return pl.pallas_call(
        _ctx_dft_kernel,
        out_shape=jax.ShapeDtypeStruct((n_batch, CTX_LEN, D_MODEL), F32),
        grid=(n_batch,),
        in_specs=[planes, planes, _const_spec((CTX_LEN, 2 * CTX_LEN))],
        out_specs=blk,
        compiler_params=_params("parallel"),
        name="fourier_ctx_dft",
    )(zr, zi, f_ctx)


def _mlstm_layer(xs, mod_l, w_in, w_conv, b_conv, w_q, w_k, w_v, w_gate, b_gate, gn_w, skip, w_out, pg, pb):
    xm, z = _in_proj(xs, mod_l, w_in.astype(BF16))
    wg = w_gate.reshape(3, M_HEADS, M_HEAD_DIM, N_GATES).transpose(1, 0, 3, 2)
    w_qt = jnp.swapaxes(w_q, 1, 2).astype(BF16)
    w_vt = jnp.swapaxes(w_v, 1, 2).astype(BF16)
    xc, q_t, k, v_t, pre_t = _conv_qkv(
        xm, w_conv.reshape(CONV_K * CONV_K, D_INNER), b_conv.reshape(1, D_INNER),
        w_qt, w_k.astype(BF16), w_vt, wg.astype(BF16), b_gate.reshape(N_GATES, 1))
    hf, hb = _mlstm_scan(q_t, k, v_t, _gate_scans(pre_t))
    return _mlstm_readout(hf, hb, xc, z, xs, mod_l, gn_w.reshape(1, D_INNER), skip.reshape(1, D_INNER),
                          w_out.astype(BF16), pg, pb)


def _fourier_layer(xs, zr, zi, mod_l, w_out, pg, pb, need_ctx, tables):
    _, k_a, k_b, f_c, f_ctx = tables
    mixed_x = _seq_dft(zr, zi, k_a, k_b, f_c)
    mixed_c = _ctx_dft(zr, zi, f_ctx) if need_ctx else None
    return _fourier_out(mixed_x, mixed_c, xs, mod_l, w_out.astype(BF16), pg, pb)


def kernel(x, c, ctx, c_ctx, ada_w, ada_b, post_g, post_b, m_w_in, m_w_conv, m_b_conv, m_w_q, m_w_k, m_w_v, m_w_gate, m_b_gate, m_gn_w, m_skip, m_w_out, f_w_out, mlp_w1, mlp_w2):
    n_batch = x.shape[0]
    assert n_batch < MOD_ROWS and x.shape[1] == FFT_R * FFT_R * FFT_C and ctx.shape[1] == CTX_LEN == TM
    xs = (ctx, x)
    s_in = jnp.concatenate([c, c_ctx[None, :], jnp.zeros((MOD_ROWS - n_batch - 1, D_MODEL), F32)], axis=0)
    mod = _ada_table(s_in, ada_w, ada_b)
    tables = _dft_tables()
    for i in range(DEPTH):
        is_mlstm = i % N_MIXERS == 0
        j = i // N_MIXERS
        need_ctx = i < DEPTH - 1
        mod_l = mod[i].reshape(MOD_ROWS, 1, 6 * D_MODEL)
        fourier_next = None
        if is_mlstm and i + 1 < DEPTH:
            fourier_next = (mod[i + 1].reshape(MOD_ROWS, 1, 6 * D_MODEL), tables[0])
        pg = post_g[i].reshape(2, 1, D_MODEL)
        pb = post_b[i].reshape(2, 1, D_MODEL)
        if is_mlstm:
            xs = _mlstm_layer(xs, mod_l, m_w_in[j], m_w_conv[j], m_b_conv[j], m_w_q[j], m_w_k[j], m_w_v[j],
                              m_w_gate[j], m_b_gate[j], m_gn_w[j], m_skip[j], m_w_out[j], pg[0], pb[0])
            if not need_ctx:
                xs = xs[:, CTX_LEN:]
        else:
            xs = _fourier_layer(xs, zr, zi, mod_l, f_w_out[j], pg[0], pb[0], need_ctx, tables)
        out = _mlp(xs, mod_l, mlp_w1[i].astype(BF16), mlp_w2[i].astype(BF16), pg[1], pb[1], need_ctx, fourier_next)
        xs, zr, zi = out if fourier_next is not None else (out, None, None)
    return xs
```

```python
import functools
import math

import numpy as np
import jax
import jax.numpy as jnp
from jax import lax
from jax.experimental import pallas as pl
from jax.experimental.pallas import tpu as pltpu

D_MODEL = 1024
DEPTH = 4
GRID_W = 64
CTX_LEN = 256
N_MIXERS = 2
D_INNER = 2 * D_MODEL
M_HEADS = 4
M_HEAD_DIM = D_INNER // M_HEADS
CONV_K = 3
F_GROUPS = 4
F_GROUP_DIM = D_MODEL // F_GROUPS
D_FF = 4 * D_MODEL
ALPHA = float((2 * DEPTH) ** 0.25)
LN_EPS = 1e-5
SHIFT1, SCALE1, GATE1, SHIFT2, SCALE2, GATE2 = range(6)
N_MOD = 6

LANES = 128
SUBLANES = 8
BF16_ROWS = 16
MXU_WIDTH = 256

TM = 256
N_GATES = 4 * M_HEADS
GATE_ROWS = SUBLANES
MOD_ROWS = SUBLANES
FFT_R = 8
FFT_C = 128
FFT_J = BF16_ROWS
FFT_CB = F_GROUP_DIM
GATE_CHUNKS = 3
N_ROWS = BF16_ROWS
STATE_ROWS = N_ROWS + M_HEAD_DIM
SCAN_HEADS = 2
DFT_UNROLL = 8
VMEM_LIMIT = 52 * 1024 * 1024
SEQ_DFT_VMEM_LIMIT = 58 * 1024 * 1024

F32 = jnp.float32
BF16 = jnp.bfloat16


def _params(*sem, vmem_limit=VMEM_LIMIT):
    return pltpu.CompilerParams(dimension_semantics=sem, vmem_limit_bytes=vmem_limit)


def _ln(x):
    mu = jnp.mean(x, axis=-1, keepdims=True)
    xc = x - mu
    var = jnp.mean(xc * xc, axis=-1, keepdims=True)
    return xc * lax.rsqrt(var + LN_EPS)


def _silu(x):
    t = 0.5 * x
    return t + t * jnp.tanh(t)


def _dot(a, b):
    return jnp.dot(a, b, preferred_element_type=F32)


def _dot_nt(a, b):
    return lax.dot_general(a, b, (((1,), (1,)), ((), ())), preferred_element_type=F32)


def _mod_chunk(m, j):
    return m[:, j * D_MODEL:(j + 1) * D_MODEL]


def _const_spec(shape):
    nd = len(shape)
    return pl.BlockSpec(shape, lambda *_: (0,) * nd, pipeline_mode=pl.Buffered(1))


def _stream_specs(stream, n_sub):
    split = isinstance(stream, tuple)
    off = CTX_LEN // TM if split else 0

    def tile_map(r):
        return lambda b, t: (b, jnp.maximum(n_sub * t + r - off, 0), 0)

    specs = [pl.BlockSpec((1, TM, D_MODEL), tile_map(r)) for r in range(n_sub)]
    if split:
        return [pl.BlockSpec((1, CTX_LEN, D_MODEL), lambda b, t: (b, 0, 0))] + specs, [stream[0]] + [stream[1]] * n_sub
    return specs, [stream] * n_sub


def _tiles_per_step(rows):
    n_t = rows // TM
    n_sub = 3 if n_t % 3 == 0 else 4
    assert rows == n_t * TM and n_t % n_sub == 0, rows
    return n_sub


def _tile_mod(mod_ref, ctx_row, r):
    m = mod_ref[pl.program_id(0)]
    if ctx_row is not None and r == 0:
        m = jnp.where(pl.program_id(1) == 0, mod_ref[ctx_row], m)
    return m


def _stream_shape(stream):
    if isinstance(stream, tuple):
        return stream[1].shape[0], stream[0].shape[1] + stream[1].shape[1]
    return stream.shape[0], stream.shape[1]


def _stream_tile(refs, n_sub, r):
    if len(refs) > n_sub:
        x = refs[1 + r][0]
        return jnp.where(pl.program_id(1) == 0, refs[0][0], x) if r == 0 else x
    return refs[r][0]


def _ada_kernel(s_ref, w_ref, b_ref, o_ref):
    s = _silu(s_ref[...])
    o_ref[0] = jnp.dot(s, w_ref[0], preferred_element_type=F32, precision=lax.Precision.HIGHEST) + b_ref[0]


def _ada_table(s_in, ada_w, ada_b):
    return pl.pallas_call(
        _ada_kernel,
        out_shape=jax.ShapeDtypeStruct((DEPTH, MOD_ROWS, N_MOD * D_MODEL), F32),
        grid=(DEPTH, N_MOD),
        in_specs=[pl.BlockSpec((MOD_ROWS, D_MODEL), lambda l, j: (0, 0)),
                  pl.BlockSpec((1, D_MODEL, D_MODEL), lambda l, j: (l, 0, j)),
                  pl.BlockSpec((1, 1, D_MODEL), lambda l, j: (l, 0, j))],
        out_specs=pl.BlockSpec((1, MOD_ROWS, D_MODEL), lambda l, j: (l, 0, j)),
        compiler_params=_params("parallel", "parallel"),
        name="ada_table",
    )(s_in, ada_w, ada_b.reshape(DEPTH, 1, N_MOD * D_MODEL))


def _in_kernel(n_stream, n_sub, ctx_row, *refs):
    mod_ref, w_ref, xm_ref, z_ref = refs[n_stream:]
    for r in range(n_sub):
        rs = slice(r * TM, (r + 1) * TM)
        x = _stream_tile(refs[:n_stream], n_sub, r)
        m = _tile_mod(mod_ref, ctx_row, r)
        h = (_ln(x) * (1.0 + _mod_chunk(m, SCALE1)) + _mod_chunk(m, SHIFT1)).astype(BF16)
        for j in range(2):
            sl = slice(j * D_MODEL, (j + 1) * D_MODEL)
            xm_ref[0, rs, sl] = _dot(h, w_ref[:, sl]).astype(BF16)
            z_ref[0, rs, sl] = _dot(h, w_ref[:, D_INNER + j * D_MODEL:D_INNER + (j + 1) * D_MODEL]).astype(BF16)


def _in_proj(stream, mod_l, w_in):
    n_batch, rows = _stream_shape(stream)
    n_sub = _tiles_per_step(rows)
    specs, args = _stream_specs(stream, n_sub)
    out = jax.ShapeDtypeStruct((n_batch, rows, D_INNER), BF16)
    out_spec = pl.BlockSpec((1, n_sub * TM, D_INNER), lambda b, t: (b, t, 0))
    return pl.pallas_call(
        functools.partial(_in_kernel, len(args), n_sub, n_batch),
        out_shape=(out, out),
        grid=(n_batch, rows // (n_sub * TM)),
        in_specs=specs + [_const_spec((MOD_ROWS, 1, N_MOD * D_MODEL)), _const_spec((D_MODEL, 2 * D_INNER))],
        out_specs=(out_spec, out_spec),
        compiler_params=_params("parallel", "parallel"),
        name="mlstm_in_proj",
    )(*args, mod_l, w_in)


def _conv_qkv_kernel(n_sub, xm_ref, prev_ref, next_ref, wc_ref, bc_ref, wq_ref, wk_ref, wv_ref, wg_ref, bg_ref,
                     xc_ref, q_ref, k_ref, v_ref, pre_ref):
    n_t = pl.num_programs(1) * n_sub
    row = lax.broadcasted_iota(jnp.int32, (TM, 1), 0)
    for r in range(n_sub):
        t = pl.program_id(1) * n_sub + r
        rs = slice(r * TM, (r + 1) * TM)
        is_ctx = t == 0
        period = jnp.where(is_ctx, CTX_LEN - 1, GRID_W - 1)
        pos = row & period
        has_left = pos != 0
        has_right = pos != period
        prev_ok = t >= 2
        next_ok = jnp.logical_and(t >= 1, t <= n_t - 2)
        row_w = jnp.where(is_ctx, 0.0, 1.0)
        pre_t = jnp.zeros((N_GATES, TM), F32) + bg_ref[...]
        for h in range(M_HEADS):
            sl = slice(h * M_HEAD_DIM, (h + 1) * M_HEAD_DIM)
            xm = xm_ref[0, rs, sl]
            prev = prev_ref[0, :, sl] if r == 0 else xm_ref[0, r * TM - GRID_W:r * TM, sl]
            nxt = next_ref[0, :, sl] if r == n_sub - 1 else xm_ref[0, (r + 1) * TM:(r + 1) * TM + GRID_W, sl]
            e = jnp.concatenate([jnp.where(prev_ok, prev, 0), xm, jnp.where(next_ok, nxt, 0)], axis=0)
            p = []
            for dc in range(CONV_K):
                acc = None
                for dr in range(CONV_K):
                    w = wc_ref[CONV_K * dr + dc:CONV_K * dr + dc + 1, sl]
                    if dr != CONV_K // 2:
                        w = w * row_w
                    term = w.astype(BF16) * e[dr * GRID_W:dr * GRID_W + TM]
                    acc = term if acc is None else acc + term
                p.append(acc.astype(F32))
            y = (p[1] + bc_ref[:, sl]
                 + jnp.where(has_left, pltpu.roll(p[0], 1, axis=0), 0.0)
                 + jnp.where(has_right, pltpu.roll(p[2], TM - 1, axis=0), 0.0))
            xc = _silu(y).astype(BF16)
            xc_ref[0, rs, sl] = xc
            q_t = _dot_nt(wq_ref[h], xc).astype(BF16)
            k = (_dot(xc, wk_ref[h]) * (M_HEAD_DIM ** -0.5)).astype(BF16)
            v_t = _dot_nt(wv_ref[h], xm).astype(BF16)
            q_ref[0, r, sl, :] = q_t
            k_ref[0, rs, sl] = k
            v_ref[0, r, sl, :] = v_t
            pre_t = pre_t + _dot(wg_ref[h, 0], q_t) + _dot_nt(wg_ref[h, 1], k) + _dot(wg_ref[h, 2], v_t)
        pre_ref[0, :, rs] = pre_t


def _conv_qkv(xm, w_conv, b_conv, w_q, w_k, w_v, w_g, b_g):
    n_batch, rows, _ = xm.shape
    n_sub = _tiles_per_step(rows)
    n_halo = rows // GRID_W
    per = n_sub * TM // GRID_W
    wide = jax.ShapeDtypeStruct((n_batch, rows, D_INNER), BF16)
    wide_t = jax.ShapeDtypeStruct((n_batch, rows // TM, D_INNER, TM), BF16)
    row_spec = pl.BlockSpec((1, n_sub * TM, D_INNER), lambda b, t: (b, t, 0))
    col_spec = pl.BlockSpec((1, n_sub, D_INNER, TM), lambda b, t: (b, t, 0, 0))
    return pl.pallas_call(
        functools.partial(_conv_qkv_kernel, n_sub),
        out_shape=(wide, wide_t, wide, wide_t, jax.ShapeDtypeStruct((n_batch, N_GATES, rows), F32)),
        grid=(n_batch, rows // (n_sub * TM)),
        in_specs=[row_spec,
                  pl.BlockSpec((1, GRID_W, D_INNER), lambda b, t: (b, jnp.maximum(per * t - 1, 0), 0)),
                  pl.BlockSpec((1, GRID_W, D_INNER), lambda b, t: (b, jnp.minimum(per * t + per, n_halo - 1), 0)),
                  _const_spec((CONV_K * CONV_K, D_INNER)),
                  _const_spec((1, D_INNER)),
                  _const_spec((M_HEADS, M_HEAD_DIM, M_HEAD_DIM)),
                  _const_spec((M_HEADS, M_HEAD_DIM, M_HEAD_DIM)),
                  _const_spec((M_HEADS, M_HEAD_DIM, M_HEAD_DIM)),
                  _const_spec((M_HEADS, 3, N_GATES, M_HEAD_DIM)),
                  _const_spec((N_GATES, 1))],
        out_specs=(row_spec, col_spec, row_spec, col_spec,
                   pl.BlockSpec((1, N_GATES, n_sub * TM), lambda b, t: (b, 0, t))),
        compiler_params=_params("parallel", "parallel"),
        name="mlstm_conv_qkv",
    )(xm, xm, xm, w_conv, b_conv, w_q, w_k, w_v, w_g, b_g)


def _scan_lanes(x, combine, fill, reverse):
    n = x.shape[-1]
    lane = lax.broadcasted_iota(jnp.int32, x.shape, 1)
    s = 1
    while s < n:
        if reverse:
            sh = jnp.where(lane < n - s, pltpu.roll(x, n - s, axis=1), fill)
        else:
            sh = jnp.where(lane >= s, pltpu.roll(x, s, axis=1), fill)
        x = combine(x, sh)
        s *= 2
    return x


def _gate_kernel(pre_ref, o_ref):
    zeros = jnp.zeros((GATE_ROWS - 3, TM), F32)
    for c in range(GATE_CHUNKS):
        lanes = slice(c * TM, (c + 1) * TM)
        for d in range(2):
            blk = pre_ref[0, 2 * M_HEADS * d:2 * M_HEADS * (d + 1), lanes]
            log_f = jnp.minimum(blk, 0.0) - jnp.log1p(jnp.exp(-jnp.abs(blk)))
            b = pltpu.roll(_scan_lanes(log_f, jnp.add, 0.0, reverse=d == 1), M_HEADS, axis=0)
            g = blk - b
            big = _scan_lanes(g, jnp.maximum, -jnp.inf, reverse=d == 1)
            for h in range(M_HEADS):
                o_ref[0, d, h, :, lanes] = jnp.concatenate([b[h:h + 1], g[h:h + 1], big[h:h + 1], zeros], axis=0)


def _gate_scans(pre_t):
    n_batch, _, rows = pre_t.shape
    width = GATE_CHUNKS * TM
    return pl.pallas_call(
        _gate_kernel,
        out_shape=jax.ShapeDtypeStruct((n_batch, 2, M_HEADS, GATE_ROWS, rows), F32),
        grid=(n_batch, rows // width),
        in_specs=[pl.BlockSpec((1, N_GATES, width), lambda b, t: (b, 0, t))],
        out_specs=pl.BlockSpec((1, 2, M_HEADS, GATE_ROWS, width), lambda b, t: (b, 0, 0, 0, t)),
        compiler_params=_params("parallel", "parallel"),
        name="mlstm_gate_scans",
    )(pre_t)


def _scan_chunk(d, hh, q_ref, k_ref, v_ref, gate_ref, h_ref, cn_ref, cnb_ref, m_ref):
    sl = slice(hh * M_HEAD_DIM, (hh + 1) * M_HEAD_DIM)
    q_t = q_ref[0, 0, sl, :]
    k = k_ref[0, :, sl]
    v_t = v_ref[0, 0, sl, :]
    gates = gate_ref[0, 0, hh]
    b_row, g_row, big_row = gates[0:1], gates[1:2], gates[2:3]
    m_prev = m_ref[d, hh, 0:1, 0:1]
    last = TM - 1 if d == 0 else 0
    m_row = jnp.maximum(m_prev, big_row)
    m_last = m_row[:, last:last + 1]
    w_inter = jnp.exp(m_prev - m_row)
    den_floor = jnp.exp(-(b_row + m_row))
    w_key = jnp.exp(g_row - m_last).astype(BF16)
    g_col = jnp.transpose(gates)[:, 1:2]
    jj = lax.broadcasted_iota(jnp.int32, (TM, TM), 0)
    ii = lax.broadcasted_iota(jnp.int32, (TM, TM), 1)
    visible = jj <= ii if d == 0 else jj >= ii
    pad = jnp.zeros((N_ROWS - 1, TM), BF16)

    s_t = _dot(k, q_t) * jnp.where(visible, jnp.exp(g_col - m_row), 0.0)
    cnb_ref[d, hh, :, M_HEAD_DIM:] = jnp.concatenate([jnp.ones((1, TM), BF16), pad, v_t], axis=0)
    rhs = jnp.concatenate([q_t * w_inter.astype(BF16), s_t.astype(BF16)], axis=0)
    both = _dot(cnb_ref[d, hh], rhs)
    scale = 1.0 / jnp.maximum(jnp.abs(both[0:1]), den_floor)
    h_ref[0, :, sl] = jnp.transpose(scale * both[N_ROWS:]).astype(BF16)

    decay = jnp.exp(m_prev - m_last)
    vw = jnp.concatenate([w_key, pad, v_t * w_key], axis=0)
    for c0 in range(0, M_HEAD_DIM, MXU_WIDTH):
        cn = decay * cn_ref[d, hh, :, c0:c0 + MXU_WIDTH] + _dot(vw, k[:, c0:c0 + MXU_WIDTH])
        cn_ref[d, hh, :, c0:c0 + MXU_WIDTH] = cn
        cnb_ref[d, hh, :, c0:c0 + MXU_WIDTH] = cn.astype(BF16)
    m_ref[d, hh] = jnp.zeros(m_ref.shape[2:], F32) + (b_row[:, last:last + 1] + m_last)


def _scan_kernel(qf_ref, kf_ref, vf_ref, gf_ref, qb_ref, kb_ref, vb_ref, gb_ref, hf_ref, hb_ref,
                 cn_ref, cnb_ref, m_ref):
    @pl.when(pl.program_id(2) == 0)
    def _():
        cn_ref[...] = jnp.zeros_like(cn_ref)
        cnb_ref[...] = jnp.zeros_like(cnb_ref)
        m_ref[...] = jnp.zeros_like(m_ref)

    for hh in range(SCAN_HEADS):
        _scan_chunk(0, hh, qf_ref, kf_ref, vf_ref, gf_ref, hf_ref, cn_ref, cnb_ref, m_ref)
        _scan_chunk(1, hh, qb_ref, kb_ref, vb_ref, gb_ref, hb_ref, cn_ref, cnb_ref, m_ref)


def _mlstm_scan(q_t, k, v_t, gates):
    n_batch, rows, _ = k.shape
    n_t = rows // TM
    width = SCAN_HEADS * M_HEAD_DIM

    def bwd_tile(c):
        return jnp.where(c == 0, 0, n_t - c)

    kf_spec = pl.BlockSpec((1, TM, width), lambda b, h, c: (b, c, h))
    kb_spec = pl.BlockSpec((1, TM, width), lambda b, h, c: (b, bwd_tile(c), h))
    tf_spec = pl.BlockSpec((1, 1, width, TM), lambda b, h, c: (b, c, h, 0))
    tb_spec = pl.BlockSpec((1, 1, width, TM), lambda b, h, c: (b, bwd_tile(c), h, 0))
    gf_spec = pl.BlockSpec((1, 1, SCAN_HEADS, GATE_ROWS, TM), lambda b, h, c: (b, 0, h, 0, c))
    gb_spec = pl.BlockSpec((1, 1, SCAN_HEADS, GATE_ROWS, TM), lambda b, h, c: (b, 1, h, 0, bwd_tile(c)))
    out = jax.ShapeDtypeStruct((n_batch, rows, D_INNER), BF16)
    return pl.pallas_call(
        _scan_kernel,
        out_shape=(out, out),
        grid=(n_batch, M_HEADS // SCAN_HEADS, n_t),
        in_specs=[tf_spec, kf_spec, tf_spec, gf_spec, tb_spec, kb_spec, tb_spec, gb_spec],
        out_specs=(kf_spec, kb_spec),
        scratch_shapes=[pltpu.VMEM((2, SCAN_HEADS, STATE_ROWS, M_HEAD_DIM), F32),
                        pltpu.VMEM((2, SCAN_HEADS, STATE_ROWS, M_HEAD_DIM + TM), BF16),
                        pltpu.VMEM((2, SCAN_HEADS, SUBLANES, LANES), F32)],
        compiler_params=_params("parallel", "parallel", "arbitrary"),
        name="mlstm_scan",
    )(q_t, k, v_t, gates, q_t, k, v_t, gates)


def _post_norm(x, y, gate, pg, pb):
    return _ln(ALPHA * x + gate * y) * pg + pb


def _readout_kernel(n_stream, n_sub, ctx_row, hf_ref, hb_ref, xc_ref, z_ref, *refs):
    mod_ref, gn_ref, sk_ref, w_ref, pg_ref, pb_ref, o_ref = refs[n_stream:]
    for r in range(n_sub):
        rs = slice(r * TM, (r + 1) * TM)
        parts = []
        for h in range(M_HEADS):
            sl = slice(h * M_HEAD_DIM, (h + 1) * M_HEAD_DIM)
            hn = _ln(hf_ref[0, rs, sl].astype(F32) + hb_ref[0, rs, sl].astype(F32))
            a = hn * gn_ref[:, sl] + sk_ref[:, sl] * xc_ref[0, rs, sl].astype(F32)
            parts.append((a * _silu(z_ref[0, rs, sl]).astype(F32)).astype(BF16))
        y = _dot(jnp.concatenate(parts, axis=1), w_ref[...])
        m = _tile_mod(mod_ref, ctx_row, r)
        o_ref[0, rs, :] = _post_norm(_stream_tile(refs[:n_stream], n_sub, r), y, _mod_chunk(m, GATE1),
                                     pg_ref[...], pb_ref[...])


def _mlstm_readout(hf, hb, xc, z, stream, mod_l, gn_w, skip, w_out, pg, pb):
    n_batch, rows = _stream_shape(stream)
    n_sub = _tiles_per_step(rows)
    specs, args = _stream_specs(stream, n_sub)
    wide = pl.BlockSpec((1, n_sub * TM, D_INNER), lambda b, t: (b, t, 0))
    return pl.pallas_call(
        functools.partial(_readout_kernel, len(args), n_sub, n_batch),
        out_shape=jax.ShapeDtypeStruct((n_batch, rows, D_MODEL), F32),
        grid=(n_batch, rows // (n_sub * TM)),
        in_specs=[wide, wide, wide, wide] + specs
        + [_const_spec((MOD_ROWS, 1, N_MOD * D_MODEL)),
           _const_spec((1, D_INNER)), _const_spec((1, D_INNER)), _const_spec((D_INNER, D_MODEL)),
           _const_spec((1, D_MODEL)), _const_spec((1, D_MODEL))],
        out_specs=pl.BlockSpec((1, n_sub * TM, D_MODEL), lambda b, t: (b, t, 0)),
        compiler_params=_params("parallel", "parallel"),
        name="mlstm_readout",
    )(hf, hb, xc, z, *args, mod_l, gn_w, skip, w_out, pg, pb)


def _fourier_out_kernel(n_sub, ctx_row, *refs):
    mx_refs, refs = refs[:n_sub], refs[n_sub:]
    if ctx_row is not None:
        mc_ref, refs = refs[0], refs[1:]
    x_refs, (mod_ref, w_ref, pg_ref, pb_ref, o_ref) = refs[:n_sub], refs[n_sub:]
    for r in range(n_sub):
        a = jnp.concatenate([mx_refs[r][0, j] for j in range(D_MODEL // LANES)], axis=1)
        if ctx_row is not None and r == 0:
            a = jnp.where(pl.program_id(1) == 0, mc_ref[0], a)
        y = _dot(a.astype(BF16), w_ref[...])
        m = _tile_mod(mod_ref, ctx_row, r)
        o_ref[0, r * TM:(r + 1) * TM, :] = _post_norm(x_refs[r][0], y, _mod_chunk(m, GATE1), pg_ref[...], pb_ref[...])


def _fourier_out(mixed_x, mixed_c, xs, mod_l, w_out, pg, pb):
    n_batch = xs.shape[0]
    ctx_tile = mixed_c is not None
    ctx_tiles = CTX_LEN // TM
    n_t = mixed_x.shape[2] // TM + (ctx_tiles if ctx_tile else 0)
    n_sub = _tiles_per_step(n_t * TM)
    mixed_off = ctx_tiles if ctx_tile else 0
    x_off = 0 if ctx_tile else ctx_tiles

    def mixed_map(r):
        return lambda b, t: (b, 0, jnp.maximum(n_sub * t + r - mixed_off, 0), 0)

    def x_map(r):
        return lambda b, t: (b, n_sub * t + r + x_off, 0)

    in_specs = [pl.BlockSpec((1, D_MODEL // LANES, TM, LANES), mixed_map(r)) for r in range(n_sub)]
    args = [mixed_x] * n_sub
    if ctx_tile:
        in_specs.append(pl.BlockSpec((1, CTX_LEN, D_MODEL), lambda b, t: (b, 0, 0)))
        args.append(mixed_c)
    in_specs += [pl.BlockSpec((1, TM, D_MODEL), x_map(r)) for r in range(n_sub)]
    args += [xs] * n_sub
    in_specs += [_const_spec((MOD_ROWS, 1, N_MOD * D_MODEL)),
                 _const_spec((D_MODEL, D_MODEL)), _const_spec((1, D_MODEL)), _const_spec((1, D_MODEL))]
    return pl.pallas_call(
        functools.partial(_fourier_out_kernel, n_sub, n_batch if ctx_tile else None),
        out_shape=jax.ShapeDtypeStruct((n_batch, n_t * TM, D_MODEL), F32),
        grid=(n_batch, n_t // n_sub),
        in_specs=in_specs,
        out_specs=pl.BlockSpec((1, n_sub * TM, D_MODEL), lambda b, t: (b, t, 0)),
        compiler_params=_params("parallel", "parallel"),
        name="fourier_out",
    )(*args, mod_l, w_out, pg, pb)


def _channel_dft_tile(x, m, f_ref, zr_ref, zi_ref, rs):
    h = (_ln(x) * (1.0 + _mod_chunk(m, SCALE1)) + _mod_chunk(m, SHIFT1)).astype(BF16)
    f = f_ref[...]
    for g in range(F_GROUPS):
        z = _dot(h[:, g * F_GROUP_DIM:(g + 1) * F_GROUP_DIM], f).astype(BF16)
        zr_ref[0, g, rs, :] = z[:, :F_GROUP_DIM]
        zi_ref[0, g, rs, :] = z[:, F_GROUP_DIM:]


def _mlp_kernel(n_sub, ctx_row, fourier_next, x_ref, mod_ref, w1_ref, w2_ref, pg_ref, pb_ref, *refs):
    if fourier_next:
        mod2_ref, f_ref, o_ref, zr_ref, zi_ref = refs
    else:
        o_ref, = refs
    for r in range(n_sub):
        rs = slice(r * TM, (r + 1) * TM)
        x = x_ref[0, rs, :]
        m = _tile_mod(mod_ref, ctx_row, r)
        h = (_ln(x) * (1.0 + _mod_chunk(m, SCALE2)) + _mod_chunk(m, SHIFT2)).astype(BF16)
        acc = jnp.zeros((TM, D_MODEL), F32)
        for j in range(D_FF // D_MODEL):
            sl = slice(j * D_MODEL, (j + 1) * D_MODEL)
            u = jnp.square(jnp.maximum(_dot(h, w1_ref[:, sl]), 0.0)).astype(BF16)
            acc = acc + _dot(u, w2_ref[sl, :])
        y = _post_norm(x, acc, _mod_chunk(m, GATE2), pg_ref[...], pb_ref[...])
        o_ref[0, rs, :] = y
        if fourier_next:
            _channel_dft_tile(y, _tile_mod(mod2_ref, ctx_row, r), f_ref, zr_ref, zi_ref, rs)


def _mlp(xs, mod_l, w1, w2, pg, pb, ctx_tile, fourier_next=None):
    n_batch, rows, _ = xs.shape
    n_sub = _tiles_per_step(rows)
    blk = pl.BlockSpec((1, n_sub * TM, D_MODEL), lambda b, t: (b, t, 0))
    in_specs = [blk, _const_spec((MOD_ROWS, 1, N_MOD * D_MODEL)),
                _const_spec((D_MODEL, D_FF)), _const_spec((D_FF, D_MODEL)),
                _const_spec((1, D_MODEL)), _const_spec((1, D_MODEL))]
    args = [xs, mod_l, w1, w2, pg, pb]
    out_shape = jax.ShapeDtypeStruct((n_batch, rows, D_MODEL), F32)
    out_specs = blk
    if fourier_next is not None:
        in_specs += [_const_spec((MOD_ROWS, 1, N_MOD * D_MODEL)), _const_spec((F_GROUP_DIM, 2 * F_GROUP_DIM))]
        args += list(fourier_next)
        plane = jax.ShapeDtypeStruct((n_batch, F_GROUPS, rows, F_GROUP_DIM), BF16)
        plane_spec = pl.BlockSpec((1, F_GROUPS, n_sub * TM, F_GROUP_DIM), lambda b, t: (b, 0, t, 0))
        out_shape, out_specs = (out_shape, plane, plane), (blk, plane_spec, plane_spec)
    return pl.pallas_call(
        functools.partial(_mlp_kernel, n_sub, n_batch if ctx_tile else None, fourier_next is not None),
        out_shape=out_shape,
        grid=(n_batch, rows // (n_sub * TM)),
        in_specs=in_specs,
        out_specs=out_specs,
        compiler_params=_params("parallel", "parallel"),
        name="mlp",
    )(*args)


def _dft_tables():
    two_pi = 2.0 * np.pi
    c = np.arange(F_GROUP_DIM)
    ang = two_pi * ((c[:, None] * c[None, :]) % F_GROUP_DIM) / F_GROUP_DIM
    s_ch = 1.0 / math.sqrt(F_GROUP_DIM)
    f_ch = np.concatenate([np.cos(ang) * s_ch, -np.sin(ang) * s_ch], axis=1)

    seq = FFT_R * FFT_R * FFT_C
    r = np.arange(FFT_R)
    eye = np.eye(FFT_J)

    def tile_mix(w):
        wr, wi = np.kron(w.real, eye), np.kron(w.imag, eye)
        return np.block([[wr, -wi], [wi, wr]])

    w8 = np.exp(-2j * np.pi * ((r[:, None] * r[None, :]) % FFT_R) / FFT_R) / math.sqrt(FFT_R)
    k_a = tile_mix(w8)
    k_b = np.stack([tile_mix(w8 * np.exp(-2j * np.pi * (r[None, :] * a1) / (FFT_R * FFT_R)))
                    for a1 in range(FFT_R)])
    c = np.arange(FFT_C)
    t1 = np.arange(FFT_R * FFT_R)
    num = (c[None, :, None] * c[None, None, :] * (FFT_R * FFT_R) + t1[:, None, None] * c[None, None, :]) % seq
    ang = two_pi * num / seq
    f_c = np.concatenate([np.cos(ang), np.sin(ang)], axis=2) / math.sqrt(FFT_C)

    p = np.arange(CTX_LEN)
    ang = two_pi * ((p[:, None] * p[None, :]) % CTX_LEN) / CTX_LEN
    sc = 1.0 / math.sqrt(CTX_LEN)
    f_ctx = np.concatenate([np.cos(ang) * sc, np.sin(ang) * sc], axis=1)
    return tuple(jnp.asarray(a, dtype=F32).astype(BF16) for a in (f_ch, k_a, k_b, f_c, f_ctx))


def _seq_dft_kernel(zr_ref, zi_ref, ka_ref, kb_ref, fc_ref, o_ref, ar_ref, ai_ref, br_ref, bi_ref):
    a_rows = FFT_R * FFT_C
    n_tiles = FFT_C // FFT_J

    def mix(load, store, base, step, k):
        offs = [base + i * step for i in range(FFT_R)]
        z = jnp.concatenate([load(0, o) for o in offs] + [load(1, o) for o in offs], axis=0)
        u = _dot(k, z).astype(BF16)
        for i, o in enumerate(offs):
            store(0, o, u[i * FFT_J:(i + 1) * FFT_J])
            store(1, o, u[(FFT_R + i) * FFT_J:(FFT_R + i + 1) * FFT_J])

    def load_in(p, o):
        return (zr_ref, zi_ref)[p][0, 0, pl.ds(CTX_LEN + o, FFT_J), :]

    def load_a(p, o):
        return (ar_ref, ai_ref)[p][pl.ds(o, FFT_J), :]

    def store_a(p, o, val):
        (ar_ref, ai_ref)[p][pl.ds(o, FFT_J), :] = val

    def store_b(p, o, val):
        (br_ref, bi_ref)[p][pl.ds(o, FFT_J), :] = val

    def stage_a(b, carry):
        for ct in range(n_tiles):
            mix(load_in, store_a, pl.multiple_of(b * FFT_C, FFT_C) + ct * FFT_J, a_rows, ka_ref[...])
        return carry

    lax.fori_loop(0, FFT_R, stage_a, 0)

    def stage_b(a1, carry):
        k = kb_ref[a1]
        for ct in range(n_tiles):
            mix(load_a, store_b, pl.multiple_of(a1 * a_rows, a_rows) + ct * FFT_J, FFT_C, k)
        return carry

    lax.fori_loop(0, FFT_R, stage_b, 0)

    def stage_c(t1, carry):
        a1, b1 = t1 & (FFT_R - 1), lax.shift_right_logical(t1, FFT_R.bit_length() - 1)
        rows = pl.ds(pl.multiple_of(a1 * a_rows + b1 * FFT_C, FFT_C), FFT_C)
        z = jnp.concatenate([br_ref[rows, :], bi_ref[rows, :]], axis=0)
        y = _dot(fc_ref[t1], z)
        for j in range(FFT_CB // LANES):
            o_ref[0, j, pl.ds(t1, FFT_C, stride=FFT_R * FFT_R), :] = y[:, j * LANES:(j + 1) * LANES]
        return carry

    lax.fori_loop(0, FFT_R * FFT_R, stage_c, 0, unroll=DFT_UNROLL)


def _seq_dft(zr, zi, k_a, k_b, f_c):
    n_batch, n_blocks, rows, _ = zr.shape
    seq = rows - CTX_LEN
    blk = pl.BlockSpec((1, 1, rows, FFT_CB), lambda b, j: (b, j, 0, 0))
    return pl.pallas_call(
        _seq_dft_kernel,
        out_shape=jax.ShapeDtypeStruct((n_batch, D_MODEL // LANES, seq, LANES), F32),
        grid=(n_batch, n_blocks),
        in_specs=[blk, blk, _const_spec(k_a.shape), _const_spec(k_b.shape), _const_spec(f_c.shape)],
        out_specs=pl.BlockSpec((1, FFT_CB // LANES, seq, LANES), lambda b, j: (b, j, 0, 0)),
        scratch_shapes=[pltpu.VMEM((seq, FFT_CB), BF16)] * 4,
        compiler_params=_params("parallel", "parallel", vmem_limit=SEQ_DFT_VMEM_LIMIT),
        name="fourier_seq_dft",
    )(zr, zi, k_a, k_b, f_c)


def _ctx_dft_kernel(zr_ref, zi_ref, f_ref, o_ref):
    planes = [jnp.concatenate([z_ref[0, g] for g in range(F_GROUPS)], axis=1) for z_ref in (zr_ref, zi_ref)]
    o_ref[0] = _dot(f_ref[...], jnp.concatenate(planes, axis=0))


def _ctx_dft(zr, zi, f_ctx):
    n_batch = zr.shape[0]
    blk = pl.BlockSpec((1, CTX_LEN, D_MODEL), lambda b: (b, 0, 0))
    planes = pl.BlockSpec((1, F_GROUPS, CTX_LEN, F_GROUP_DIM), lambda b: (b, 0, 0, 0))
    return pl.pallas_call(
        _ctx_dft_kernel,
        out_shape=jax.ShapeDtypeStruct((n_batch, CTX_LEN, D_MODEL), F32),
        grid=(n_batch,),
        in_specs=[planes, planes, _const_spec((CTX_LEN, 2 * CTX_LEN))],
        out_specs=blk,
        compiler_params=_params("parallel"),
        name="fourier_ctx_dft",
    )(zr, zi, f_ctx)


def _mlstm_layer(xs, mod_l, w_in, w_conv, b_conv, w_q, w_k, w_v, w_gate, b_gate, gn_w, skip, w_out, pg, pb):
    xm, z = _in_proj(xs, mod_l, w_in.astype(BF16))
    wg = w_gate.reshape(3, M_HEADS, M_HEAD_DIM, N_GATES).transpose(1, 0, 3, 2)
    w_qt = jnp.swapaxes(w_q, 1, 2).astype(BF16)
    w_vt = jnp.swapaxes(w_v, 1, 2).astype(BF16)
    xc, q_t, k, v_t, pre_t = _conv_qkv(
        xm, w_conv.reshape(CONV_K * CONV_K, D_INNER), b_conv.reshape(1, D_INNER),
        w_qt, w_k.astype(BF16), w_vt, wg.astype(BF16), b_gate.reshape(N_GATES, 1))
    hf, hb = _mlstm_scan(q_t, k, v_t, _gate_scans(pre_t))
    return _mlstm_readout(hf, hb, xc, z, xs, mod_l, gn_w.reshape(1, D_INNER), skip.reshape(1, D_INNER),
                          w_out.astype(BF16), pg, pb)


def _fourier_layer(xs, zr, zi, mod_l, w_out, pg, pb, need_ctx, tables):
    _, k_a, k_b, f_c, f_ctx = tables
    mixed_x = _seq_dft(zr, zi, k_a, k_b, f_c)
    mixed_c = _ctx_dft(zr, zi, f_ctx) if need_ctx else None
    return _fourier_out(mixed_x, mixed_c, xs, mod_l, w_out.astype(BF16), pg, pb)


def kernel(x, c, ctx, c_ctx, ada_w, ada_b, post_g, post_b, m_w_in, m_w_conv, m_b_conv, m_w_q, m_w_k, m_w_v, m_w_gate, m_b_gate, m_gn_w, m_skip, m_w_out, f_w_out, mlp_w1, mlp_w2):
    n_batch = x.shape[0]
    assert n_batch < MOD_ROWS and x.shape[1] == FFT_R * FFT_R * FFT_C and ctx.shape[1] == CTX_LEN == TM
    xs = (ctx, x)
    s_in = jnp.concatenate([c, c_ctx[None, :], jnp.zeros((MOD_ROWS - n_batch - 1, D_MODEL), F32)], axis=0)
    mod = _ada_table(s_in, ada_w, ada_b)
    tables = _dft_tables()
    for i in range(DEPTH):
        is_mlstm = i % N_MIXERS == 0
        j = i // N_MIXERS
        need_ctx = i < DEPTH - 1
        mod_l = mod[i].reshape(MOD_ROWS, 1, N_MOD * D_MODEL)
        fourier_next = None
        if is_mlstm and i + 1 < DEPTH:
            fourier_next = (mod[i + 1].reshape(MOD_ROWS, 1, N_MOD * D_MODEL), tables[0])
        pg = post_g[i].reshape(2, 1, D_MODEL)
        pb = post_b[i].reshape(2, 1, D_MODEL)
        if is_mlstm:
            xs = _mlstm_layer(xs, mod_l, m_w_in[j], m_w_conv[j], m_b_conv[j], m_w_q[j], m_w_k[j], m_w_v[j],
                              m_w_gate[j], m_b_gate[j], m_gn_w[j], m_skip[j], m_w_out[j], pg[0], pb[0])
            if not need_ctx:
                xs = xs[:, CTX_LEN:]
        else:
            xs = _fourier_layer(xs, zr, zi, mod_l, f_w_out[j], pg[0], pb[0], need_ctx, tables)
        out = _mlp(xs, mod_l, mlp_w1[i].astype(BF16), mlp_w2[i].astype(BF16), pg[1], pb[1], need_ctx, fourier_next)
        xs, zr, zi = out if fourier_next is not None else (out, None, None)
    return xs
```

```python
import functools
import math

import numpy as np
import jax
import jax.numpy as jnp
from jax import lax
from jax.experimental import pallas as pl
from jax.experimental.pallas import tpu as pltpu

D_MODEL = 1024
DEPTH = 4
GRID_W = 64
CTX_LEN = 256
N_MIXERS = 2
D_INNER = 2 * D_MODEL
M_HEADS = 4
M_HEAD_DIM = D_INNER // M_HEADS
CONV_K = 3
F_GROUPS = 4
F_GROUP_DIM = D_MODEL // F_GROUPS
D_FF = 4 * D_MODEL
ALPHA = float((2 * DEPTH) ** 0.25)
LN_EPS = 1e-5
SHIFT1, SCALE1, GATE1, SHIFT2, SCALE2, GATE2 = range(6)
N_MOD = 6

LANES = 128
SUBLANES = 8
BF16_ROWS = 16
MXU_WIDTH = 256

TM = 256
N_GATES = 4 * M_HEADS
GATE_ROWS = SUBLANES
MOD_ROWS = SUBLANES
FFT_R = 8
FFT_C = 128
FFT_J = BF16_ROWS
FFT_CB = F_GROUP_DIM
GATE_CHUNKS = 11
N_ROWS = BF16_ROWS
STATE_ROWS = N_ROWS + M_HEAD_DIM
SCAN_HEADS = 2
DFT_UNROLL = 8
VMEM_LIMIT = 52 * 1024 * 1024
SEQ_DFT_VMEM_LIMIT = 58 * 1024 * 1024

F32 = jnp.float32
BF16 = jnp.bfloat16


def _params(*sem, vmem_limit=VMEM_LIMIT):
    return pltpu.CompilerParams(dimension_semantics=sem, vmem_limit_bytes=vmem_limit)


def _ln(x):
    mu = jnp.mean(x, axis=-1, keepdims=True)
    xc = x - mu
    var = jnp.mean(xc * xc, axis=-1, keepdims=True)
    return xc * lax.rsqrt(var + LN_EPS)


def _silu(x):
    t = 0.5 * x
    return t + t * jnp.tanh(t)


def _dot(a, b):
    return jnp.dot(a, b, preferred_element_type=F32)


def _dot_nt(a, b):
    return lax.dot_general(a, b, (((1,), (1,)), ((), ())), preferred_element_type=F32)


def _mod_chunk(m, j):
    return m[:, j * D_MODEL:(j + 1) * D_MODEL]


def _const_spec(shape):
    nd = len(shape)
    return pl.BlockSpec(shape, lambda *_: (0,) * nd, pipeline_mode=pl.Buffered(1))


def _stream_specs(stream, n_sub):
    split = isinstance(stream, tuple)
    off = CTX_LEN // TM if split else 0

    def tile_map(r):
        return lambda b, t: (b, jnp.maximum(n_sub * t + r - off, 0), 0)

    specs = [pl.BlockSpec((1, TM, D_MODEL), tile_map(r)) for r in range(n_sub)]
    if split:
        return [pl.BlockSpec((1, CTX_LEN, D_MODEL), lambda b, t: (b, 0, 0))] + specs, [stream[0]] + [stream[1]] * n_sub
    return specs, [stream] * n_sub


def _tiles_per_step(rows):
    n_t = rows // TM
    n_sub = 3 if n_t % 3 == 0 else 4
    assert rows == n_t * TM and n_t % n_sub == 0, rows
    return n_sub


def _tile_mod(mod_ref, ctx_row, r):
    m = mod_ref[pl.program_id(0)]
    if ctx_row is not None and r == 0:
        m = jnp.where(pl.program_id(1) == 0, mod_ref[ctx_row], m)
    return m


def _stream_shape(stream):
    if isinstance(stream, tuple):
        return stream[1].shape[0], stream[0].shape[1] + stream[1].shape[1]
    return stream.shape[0], stream.shape[1]


def _stream_tile(refs, n_sub, r):
    if len(refs) > n_sub:
        x = refs[1 + r][0]
        return jnp.where(pl.program_id(1) == 0, refs[0][0], x) if r == 0 else x
    return refs[r][0]


def _ada_kernel(s_ref, w_ref, b_ref, o_ref):
    s = _silu(s_ref[...])
    o_ref[0] = jnp.dot(s, w_ref[0], preferred_element_type=F32, precision=lax.Precision.HIGHEST) + b_ref[0]


def _ada_table(s_in, ada_w, ada_b):
    return pl.pallas_call(
        _ada_kernel,
        out_shape=jax.ShapeDtypeStruct((DEPTH, MOD_ROWS, N_MOD * D_MODEL), F32),
        grid=(DEPTH, N_MOD),
        in_specs=[pl.BlockSpec((MOD_ROWS, D_MODEL), lambda l, j: (0, 0)),
                  pl.BlockSpec((1, D_MODEL, D_MODEL), lambda l, j: (l, 0, j)),
                  pl.BlockSpec((1, 1, D_MODEL), lambda l, j: (l, 0, j))],
        out_specs=pl.BlockSpec((1, MOD_ROWS, D_MODEL), lambda l, j: (l, 0, j)),
        compiler_params=_params("parallel", "parallel"),
        name="ada_table",
    )(s_in, ada_w, ada_b.reshape(DEPTH, 1, N_MOD * D_MODEL))


def _in_kernel(n_stream, n_sub, ctx_row, *refs):
    mod_ref, w_ref, xm_ref, z_ref = refs[n_stream:]
    for r in range(n_sub):
        rs = slice(r * TM, (r + 1) * TM)
        x = _stream_tile(refs[:n_stream], n_sub, r)
        m = _tile_mod(mod_ref, ctx_row, r)
        h = (_ln(x) * (1.0 + _mod_chunk(m, SCALE1)) + _mod_chunk(m, SHIFT1)).astype(BF16)
        for j in range(2):
            sl = slice(j * D_MODEL, (j + 1) * D_MODEL)
            xm_ref[0, rs, sl] = _dot(h, w_ref[:, sl]).astype(BF16)
            z_ref[0, rs, sl] = _dot(h, w_ref[:, D_INNER + j * D_MODEL:D_INNER + (j + 1) * D_MODEL]).astype(BF16)


def _in_proj(stream, mod_l, w_in):
    n_batch, rows = _stream_shape(stream)
    n_sub = _tiles_per_step(rows)
    specs, args = _stream_specs(stream, n_sub)
    out = jax.ShapeDtypeStruct((n_batch, rows, D_INNER), BF16)
    out_spec = pl.BlockSpec((1, n_sub * TM, D_INNER), lambda b, t: (b, t, 0))
    return pl.pallas_call(
        functools.partial(_in_kernel, len(args), n_sub, n_batch),
        out_shape=(out, out),
        grid=(n_batch, rows // (n_sub * TM)),
        in_specs=specs + [_const_spec((MOD_ROWS, 1, N_MOD * D_MODEL)), _const_spec((D_MODEL, 2 * D_INNER))],
        out_specs=(out_spec, out_spec),
        compiler_params=_params("parallel", "parallel"),
        name="mlstm_in_proj",
    )(*args, mod_l, w_in)


def _conv_qkv_kernel(n_sub, xm_ref, prev_ref, next_ref, wc_ref, bc_ref, wq_ref, wk_ref, wv_ref, wg_ref, bg_ref,
                     xc_ref, q_ref, k_ref, v_ref, pre_ref):
    n_t = pl.num_programs(1) * n_sub
    row = lax.broadcasted_iota(jnp.int32, (TM, 1), 0)
    for r in range(n_sub):
        t = pl.program_id(1) * n_sub + r
        rs = slice(r * TM, (r + 1) * TM)
        is_ctx = t == 0
        period = jnp.where(is_ctx, CTX_LEN - 1, GRID_W - 1)
        pos = row & period
        has_left = pos != 0
        has_right = pos != period
        prev_ok = t >= 2
        next_ok = jnp.logical_and(t >= 1, t <= n_t - 2)
        row_w = jnp.where(is_ctx, 0.0, 1.0)
        pre_t = jnp.zeros((N_GATES, TM), F32) + bg_ref[...]
        for h in range(M_HEADS):
            sl = slice(h * M_HEAD_DIM, (h + 1) * M_HEAD_DIM)
            xm = xm_ref[0, rs, sl]
            prev = prev_ref[0, :, sl] if r == 0 else xm_ref[0, r * TM - GRID_W:r * TM, sl]
            nxt = next_ref[0, :, sl] if r == n_sub - 1 else xm_ref[0, (r + 1) * TM:(r + 1) * TM + GRID_W, sl]
            e = jnp.concatenate([jnp.where(prev_ok, prev, 0), xm, jnp.where(next_ok, nxt, 0)], axis=0)
            p = []
            for dc in range(CONV_K):
                acc = None
                for dr in range(CONV_K):
                    w = wc_ref[CONV_K * dr + dc:CONV_K * dr + dc + 1, sl]
                    if dr != CONV_K // 2:
                        w = w * row_w
                    term = w.astype(BF16) * e[dr * GRID_W:dr * GRID_W + TM]
                    acc = term if acc is None else acc + term
                p.append(acc.astype(F32))
            y = (p[1] + bc_ref[:, sl]
                 + jnp.where(has_left, pltpu.roll(p[0], 1, axis=0), 0.0)
                 + jnp.where(has_right, pltpu.roll(p[2], TM - 1, axis=0), 0.0))
            xc = _silu(y).astype(BF16)
            xc_ref[0, rs, sl] = xc
            q_t = _dot_nt(wq_ref[h], xc).astype(BF16)
            k = (_dot(xc, wk_ref[h]) * (M_HEAD_DIM ** -0.5)).astype(BF16)
            v_t = _dot_nt(wv_ref[h], xm).astype(BF16)
            q_ref[0, r, sl, :] = q_t
            k_ref[0, rs, sl] = k
            v_ref[0, r, sl, :] = v_t
            pre_t = pre_t + _dot(wg_ref[h, 0], q_t) + _dot_nt(wg_ref[h, 1], k) + _dot(wg_ref[h, 2], v_t)
        pre_ref[0, :, rs] = pre_t


def _conv_qkv(xm, w_conv, b_conv, w_q, w_k, w_v, w_g, b_g):
    n_batch, rows, _ = xm.shape
    n_sub = _tiles_per_step(rows)
    n_halo = rows // GRID_W
    per = n_sub * TM // GRID_W
    wide = jax.ShapeDtypeStruct((n_batch, rows, D_INNER), BF16)
    wide_t = jax.ShapeDtypeStruct((n_batch, rows // TM, D_INNER, TM), BF16)
    row_spec = pl.BlockSpec((1, n_sub * TM, D_INNER), lambda b, t: (b, t, 0))
    col_spec = pl.BlockSpec((1, n_sub, D_INNER, TM), lambda b, t: (b, t, 0, 0))
    return pl.pallas_call(
        functools.partial(_conv_qkv_kernel, n_sub),
        out_shape=(wide, wide_t, wide, wide_t, jax.ShapeDtypeStruct((n_batch, N_GATES, rows), F32)),
        grid=(n_batch, rows // (n_sub * TM)),
        in_specs=[row_spec,
                  pl.BlockSpec((1, GRID_W, D_INNER), lambda b, t: (b, jnp.maximum(per * t - 1, 0), 0)),
                  pl.BlockSpec((1, GRID_W, D_INNER), lambda b, t: (b, jnp.minimum(per * t + per, n_halo - 1), 0)),
                  _const_spec((CONV_K * CONV_K, D_INNER)),
                  _const_spec((1, D_INNER)),
                  _const_spec((M_HEADS, M_HEAD_DIM, M_HEAD_DIM)),
                  _const_spec((M_HEADS, M_HEAD_DIM, M_HEAD_DIM)),
                  _const_spec((M_HEADS, M_HEAD_DIM, M_HEAD_DIM)),
                  _const_spec((M_HEADS, 3, N_GATES, M_HEAD_DIM)),
                  _const_spec((N_GATES, 1))],
        out_specs=(row_spec, col_spec, row_spec, col_spec,
                   pl.BlockSpec((1, N_GATES, n_sub * TM), lambda b, t: (b, 0, t))),
        compiler_params=_params("parallel", "parallel"),
        name="mlstm_conv_qkv",
    )(xm, xm, xm, w_conv, b_conv, w_q, w_k, w_v, w_g, b_g)


def _scan_lanes(x, combine, fill, reverse):
    n = x.shape[-1]
    lane = lax.broadcasted_iota(jnp.int32, x.shape, 1)
    s = 1
    while s < n:
        if reverse:
            sh = jnp.where(lane < n - s, pltpu.roll(x, n - s, axis=1), fill)
        else:
            sh = jnp.where(lane >= s, pltpu.roll(x, s, axis=1), fill)
        x = combine(x, sh)
        s *= 2
    return x


def _gate_kernel(pre_ref, o_ref):
    zeros = jnp.zeros((GATE_ROWS - 3, TM), F32)
    for c in range(GATE_CHUNKS):
        lanes = slice(c * TM, (c + 1) * TM)
        for d in range(2):
            blk = pre_ref[0, 2 * M_HEADS * d:2 * M_HEADS * (d + 1), lanes]
            log_f = jnp.minimum(blk, 0.0) - jnp.log1p(jnp.exp(-jnp.abs(blk)))
            b = pltpu.roll(_scan_lanes(log_f, jnp.add, 0.0, reverse=d == 1), M_HEADS, axis=0)
            g = blk - b
            big = _scan_lanes(g, jnp.maximum, -jnp.inf, reverse=d == 1)
            for h in range(M_HEADS):
                o_ref[0, d, h, :, lanes] = jnp.concatenate([b[h:h + 1], g[h:h + 1], big[h:h + 1], zeros], axis=0)


def _gate_scans(pre_t):
    n_batch, _, rows = pre_t.shape
    width = GATE_CHUNKS * TM
    return pl.pallas_call(
        _gate_kernel,
        out_shape=jax.ShapeDtypeStruct((n_batch, 2, M_HEADS, GATE_ROWS, rows), F32),
        grid=(n_batch, rows // width),
        in_specs=[pl.BlockSpec((1, N_GATES, width), lambda b, t: (b, 0, t))],
        out_specs=pl.BlockSpec((1, 2, M_HEADS, GATE_ROWS, width), lambda b, t: (b, 0, 0, 0, t)),
        compiler_params=_params("parallel", "parallel"),
        name="mlstm_gate_scans",
    )(pre_t)


def _scan_chunk(d, hh, q_ref, k_ref, v_ref, gate_ref, h_ref, cn_ref, cnb_ref, m_ref):
    sl = slice(hh * M_HEAD_DIM, (hh + 1) * M_HEAD_DIM)
    q_t = q_ref[0, 0, sl, :]
    k = k_ref[0, :, sl]
    v_t = v_ref[0, 0, sl, :]
    gates = gate_ref[0, 0, hh]
    b_row, g_row, big_row = gates[0:1], gates[1:2], gates[2:3]
    m_prev = m_ref[d, hh, 0:1, 0:1]
    last = TM - 1 if d == 0 else 0
    m_row = jnp.maximum(m_prev, big_row)
    m_last = m_row[:, last:last + 1]
    w_inter = jnp.exp(m_prev - m_row)
    den_floor = jnp.exp(-(b_row + m_row))
    w_key = jnp.exp(g_row - m_last).astype(BF16)
    g_col = jnp.transpose(gates)[:, 1:2]
    jj = lax.broadcasted_iota(jnp.int32, (TM, TM), 0)
    ii = lax.broadcasted_iota(jnp.int32, (TM, TM), 1)
    visible = jj <= ii if d == 0 else jj >= ii
    pad = jnp.zeros((N_ROWS - 1, TM), BF16)

    s_t = _dot(k, q_t) * jnp.where(visible, jnp.exp(g_col - m_row), 0.0)
    cnb_ref[d, hh, :, M_HEAD_DIM:] = jnp.concatenate([jnp.ones((1, TM), BF16), pad, v_t], axis=0)
    rhs = jnp.concatenate([q_t * w_inter.astype(BF16), s_t.astype(BF16)], axis=0)
    both = _dot(cnb_ref[d, hh], rhs)
    scale = 1.0 / jnp.maximum(jnp.abs(both[0:1]), den_floor)
    h_ref[0, :, sl] = jnp.transpose(scale * both[N_ROWS:]).astype(BF16)

    decay = jnp.exp(m_prev - m_last)
    vw = jnp.concatenate([w_key, pad, v_t * w_key], axis=0)
    for c0 in range(0, M_HEAD_DIM, MXU_WIDTH):
        cn = decay * cn_ref[d, hh, :, c0:c0 + MXU_WIDTH] + _dot(vw, k[:, c0:c0 + MXU_WIDTH])
        cn_ref[d, hh, :, c0:c0 + MXU_WIDTH] = cn
        cnb_ref[d, hh, :, c0:c0 + MXU_WIDTH] = cn.astype(BF16)
    m_ref[d, hh] = jnp.zeros(m_ref.shape[2:], F32) + (b_row[:, last:last + 1] + m_last)


def _scan_kernel(qf_ref, kf_ref, vf_ref, gf_ref, qb_ref, kb_ref, vb_ref, gb_ref, hf_ref, hb_ref,
                 cn_ref, cnb_ref, m_ref):
    @pl.when(pl.program_id(2) == 0)
    def _():
        cn_ref[...] = jnp.zeros_like(cn_ref)
        cnb_ref[...] = jnp.zeros_like(cnb_ref)
        m_ref[...] = jnp.zeros_like(m_ref)

    for hh in range(SCAN_HEADS):
        _scan_chunk(0, hh, qf_ref, kf_ref, vf_ref, gf_ref, hf_ref, cn_ref, cnb_ref, m_ref)
        _scan_chunk(1, hh, qb_ref, kb_ref, vb_ref, gb_ref, hb_ref, cn_ref, cnb_ref, m_ref)


def _mlstm_scan(q_t, k, v_t, gates):
    n_batch, rows, _ = k.shape
    n_t = rows // TM
    width = SCAN_HEADS * M_HEAD_DIM

    def bwd_tile(c):
        return jnp.where(c == 0, 0, n_t - c)

    kf_spec = pl.BlockSpec((1, TM, width), lambda b, h, c: (b, c, h))
    kb_spec = pl.BlockSpec((1, TM, width), lambda b, h, c: (b, bwd_tile(c), h))
    tf_spec = pl.BlockSpec((1, 1, width, TM), lambda b, h, c: (b, c, h, 0))
    tb_spec = pl.BlockSpec((1, 1, width, TM), lambda b, h, c: (b, bwd_tile(c), h, 0))
    gf_spec = pl.BlockSpec((1, 1, SCAN_HEADS, GATE_ROWS, TM), lambda b, h, c: (b, 0, h, 0, c))
    gb_spec = pl.BlockSpec((1, 1, SCAN_HEADS, GATE_ROWS, TM), lambda b, h, c: (b, 1, h, 0, bwd_tile(c)))
    out = jax.ShapeDtypeStruct((n_batch, rows, D_INNER), BF16)
    return pl.pallas_call(
        _scan_kernel,
        out_shape=(out, out),
        grid=(n_batch, M_HEADS // SCAN_HEADS, n_t),
        in_specs=[tf_spec, kf_spec, tf_spec, gf_spec, tb_spec, kb_spec, tb_spec, gb_spec],
        out_specs=(kf_spec, kb_spec),
        scratch_shapes=[pltpu.VMEM((2, SCAN_HEADS, STATE_ROWS, M_HEAD_DIM), F32),
                        pltpu.VMEM((2, SCAN_HEADS, STATE_ROWS, M_HEAD_DIM + TM), BF16),
                        pltpu.VMEM((2, SCAN_HEADS, SUBLANES, LANES), F32)],
        compiler_params=_params("parallel", "parallel", "arbitrary"),
        name="mlstm_scan",
    )(q_t, k, v_t, gates, q_t, k, v_t, gates)


def _post_norm(x, y, gate, pg, pb):
    return _ln(ALPHA * x + gate * y) * pg + pb


def _readout_kernel(n_stream, n_sub, ctx_row, hf_ref, hb_ref, xc_ref, z_ref, *refs):
    mod_ref, gn_ref, sk_ref, w_ref, pg_ref, pb_ref, o_ref = refs[n_stream:]
    for r in range(n_sub):
        rs = slice(r * TM, (r + 1) * TM)
        parts = []
        for h in range(M_HEADS):
            sl = slice(h * M_HEAD_DIM, (h + 1) * M_HEAD_DIM)
            hn = _ln(hf_ref[0, rs, sl].astype(F32) + hb_ref[0, rs, sl].astype(F32))
            a = hn * gn_ref[:, sl] + sk_ref[:, sl] * xc_ref[0, rs, sl].astype(F32)
            parts.append((a * _silu(z_ref[0, rs, sl]).astype(F32)).astype(BF16))
        y = _dot(jnp.concatenate(parts, axis=1), w_ref[...])
        m = _tile_mod(mod_ref, ctx_row, r)
        o_ref[0, rs, :] = _post_norm(_stream_tile(refs[:n_stream], n_sub, r), y, _mod_chunk(m, GATE1),
                                     pg_ref[...], pb_ref[...])


def _mlstm_readout(hf, hb, xc, z, stream, mod_l, gn_w, skip, w_out, pg, pb):
    n_batch, rows = _stream_shape(stream)
    n_sub = _tiles_per_step(rows)
    specs, args = _stream_specs(stream, n_sub)
    wide = pl.BlockSpec((1, n_sub * TM, D_INNER), lambda b, t: (b, t, 0))
    return pl.pallas_call(
        functools.partial(_readout_kernel, len(args), n_sub, n_batch),
        out_shape=jax.ShapeDtypeStruct((n_batch, rows, D_MODEL), F32),
        grid=(n_batch, rows // (n_sub * TM)),
        in_specs=[wide, wide, wide, wide] + specs
        + [_const_spec((MOD_ROWS, 1, N_MOD * D_MODEL)),
           _const_spec((1, D_INNER)), _const_spec((1, D_INNER)), _const_spec((D_INNER, D_MODEL)),
           _const_spec((1, D_MODEL)), _const_spec((1, D_MODEL))],
        out_specs=pl.BlockSpec((1, n_sub * TM, D_MODEL), lambda b, t: (b, t, 0)),
        compiler_params=_params("parallel", "parallel"),
        name="mlstm_readout",
    )(hf, hb, xc, z, *args, mod_l, gn_w, skip, w_out, pg, pb)


def _fourier_out_kernel(n_sub, ctx_row, *refs):
    mx_refs, refs = refs[:n_sub], refs[n_sub:]
    if ctx_row is not None:
        mc_ref, refs = refs[0], refs[1:]
    x_refs, (mod_ref, w_ref, pg_ref, pb_ref, o_ref) = refs[:n_sub], refs[n_sub:]
    for r in range(n_sub):
        a = jnp.concatenate([mx_refs[r][0, j] for j in range(D_MODEL // LANES)], axis=1)
        if ctx_row is not None and r == 0:
            a = jnp.where(pl.program_id(1) == 0, mc_ref[0], a)
        y = _dot(a.astype(BF16), w_ref[...])
        m = _tile_mod(mod_ref, ctx_row, r)
        o_ref[0, r * TM:(r + 1) * TM, :] = _post_norm(x_refs[r][0], y, _mod_chunk(m, GATE1), pg_ref[...], pb_ref[...])


def _fourier_out(mixed_x, mixed_c, xs, mod_l, w_out, pg, pb):
    n_batch = xs.shape[0]
    ctx_tile = mixed_c is not None
    ctx_tiles = CTX_LEN // TM
    n_t = mixed_x.shape[2] // TM + (ctx_tiles if ctx_tile else 0)
    n_sub = _tiles_per_step(n_t * TM)
    mixed_off = ctx_tiles if ctx_tile else 0
    x_off = 0 if ctx_tile else ctx_tiles

    def mixed_map(r):
        return lambda b, t: (b, 0, jnp.maximum(n_sub * t + r - mixed_off, 0), 0)

    def x_map(r):
        return lambda b, t: (b, n_sub * t + r + x_off, 0)

    in_specs = [pl.BlockSpec((1, D_MODEL // LANES, TM, LANES), mixed_map(r)) for r in range(n_sub)]
    args = [mixed_x] * n_sub
    if ctx_tile:
        in_specs.append(pl.BlockSpec((1, CTX_LEN, D_MODEL), lambda b, t: (b, 0, 0)))
        args.append(mixed_c)
    in_specs += [pl.BlockSpec((1, TM, D_MODEL), x_map(r)) for r in range(n_sub)]
    args += [xs] * n_sub
    in_specs += [_const_spec((MOD_ROWS, 1, N_MOD * D_MODEL)),
                 _const_spec((D_MODEL, D_MODEL)), _const_spec((1, D_MODEL)), _const_spec((1, D_MODEL))]
    return pl.pallas_call(
        functools.partial(_fourier_out_kernel, n_sub, n_batch if ctx_tile else None),
        out_shape=jax.ShapeDtypeStruct((n_batch, n_t * TM, D_MODEL), F32),
        grid=(n_batch, n_t // n_sub),
        in_specs=in_specs,
        out_specs=pl.BlockSpec((1, n_sub * TM, D_MODEL), lambda b, t: (b, t, 0)),
        compiler_params=_params("parallel", "parallel"),
        name="fourier_out",
    )(*args, mod_l, w_out, pg, pb)


def _channel_dft_tile(x, m, f_ref, zr_ref, zi_ref, rs):
    h = (_ln(x) * (1.0 + _mod_chunk(m, SCALE1)) + _mod_chunk(m, SHIFT1)).astype(BF16)
    f = f_ref[...]
    for g in range(F_GROUPS):
        z = _dot(h[:, g * F_GROUP_DIM:(g + 1) * F_GROUP_DIM], f).astype(BF16)
        zr_ref[0, g, rs, :] = z[:, :F_GROUP_DIM]
        zi_ref[0, g, rs, :] = z[:, F_GROUP_DIM:]


def _mlp_kernel(n_sub, ctx_row, fourier_next, x_ref, mod_ref, w1_ref, w2_ref, pg_ref, pb_ref, *refs):
    if fourier_next:
        mod2_ref, f_ref, o_ref, zr_ref, zi_ref = refs
    else:
        o_ref, = refs
    for r in range(n_sub):
        rs = slice(r * TM, (r + 1) * TM)
        x = x_ref[0, rs, :]
        m = _tile_mod(mod_ref, ctx_row, r)
        h = (_ln(x) * (1.0 + _mod_chunk(m, SCALE2)) + _mod_chunk(m, SHIFT2)).astype(BF16)
        acc = jnp.zeros((TM, D_MODEL), F32)
        for j in range(D_FF // D_MODEL):
            sl = slice(j * D_MODEL, (j + 1) * D_MODEL)
            u = jnp.square(jnp.maximum(_dot(h, w1_ref[:, sl]), 0.0)).astype(BF16)
            acc = acc + _dot(u, w2_ref[sl, :])
        y = _post_norm(x, acc, _mod_chunk(m, GATE2), pg_ref[...], pb_ref[...])
        o_ref[0, rs, :] = y
        if fourier_next:
            _channel_dft_tile(y, _tile_mod(mod2_ref, ctx_row, r), f_ref, zr_ref, zi_ref, rs)


def _mlp(xs, mod_l, w1, w2, pg, pb, ctx_tile, fourier_next=None):
    n_batch, rows, _ = xs.shape
    n_sub = _tiles_per_step(rows)
    blk = pl.BlockSpec((1, n_sub * TM, D_MODEL), lambda b, t: (b, t, 0))
    in_specs = [blk, _const_spec((MOD_ROWS, 1, N_MOD * D_MODEL)),
                _const_spec((D_MODEL, D_FF)), _const_spec((D_FF, D_MODEL)),
                _const_spec((1, D_MODEL)), _const_spec((1, D_MODEL))]
    args = [xs, mod_l, w1, w2, pg, pb]
    out_shape = jax.ShapeDtypeStruct((n_batch, rows, D_MODEL), F32)
    out_specs = blk
    if fourier_next is not None:
        in_specs += [_const_spec((MOD_ROWS, 1, N_MOD * D_MODEL)), _const_spec((F_GROUP_DIM, 2 * F_GROUP_DIM))]
        args += list(fourier_next)
        plane = jax.ShapeDtypeStruct((n_batch, F_GROUPS, rows, F_GROUP_DIM), BF16)
        plane_spec = pl.BlockSpec((1, F_GROUPS, n_sub * TM, F_GROUP_DIM), lambda b, t: (b, 0, t, 0))
        out_shape, out_specs = (out_shape, plane, plane), (blk, plane_spec, plane_spec)
    return pl.pallas_call(
        functools.partial(_mlp_kernel, n_sub, n_batch if ctx_tile else None, fourier_next is not None),
        out_shape=out_shape,
        grid=(n_batch, rows // (n_sub * TM)),
        in_specs=in_specs,
        out_specs=out_specs,
        compiler_params=_params("parallel", "parallel"),
        name="mlp",
    )(*args)


def _dft_tables():
    two_pi = 2.0 * np.pi
    c = np.arange(F_GROUP_DIM)
    ang = two_pi * ((c[:, None] * c[None, :]) % F_GROUP_DIM) / F_GROUP_DIM
    s_ch = 1.0 / math.sqrt(F_GROUP_DIM)
    f_ch = np.concatenate([np.cos(ang) * s_ch, -np.sin(ang) * s_ch], axis=1)

    seq = FFT_R * FFT_R * FFT_C
    r = np.arange(FFT_R)
    eye = np.eye(FFT_J)

    def tile_mix(w):
        wr, wi = np.kron(w.real, eye), np.kron(w.imag, eye)
        return np.block([[wr, -wi], [wi, wr]])

    w8 = np.exp(-2j * np.pi * ((r[:, None] * r[None, :]) % FFT_R) / FFT_R) / math.sqrt(FFT_R)
    k_a = tile_mix(w8)
    k_b = np.stack([tile_mix(w8 * np.exp(-2j * np.pi * (r[None, :] * a1) / (FFT_R * FFT_R)))
                    for a1 in range(FFT_R)])
    c = np.arange(FFT_C)
    t1 = np.arange(FFT_R * FFT_R)
    num = (c[None, :, None] * c[None, None, :] * (FFT_R * FFT_R) + t1[:, None, None] * c[None, None, :]) % seq
    ang = two_pi * num / seq
    f_c = np.concatenate([np.cos(ang), np.sin(ang)], axis=2) / math.sqrt(FFT_C)

    p = np.arange(CTX_LEN)
    ang = two_pi * ((p[:, None] * p[None, :]) % CTX_LEN) / CTX_LEN
    sc = 1.0 / math.sqrt(CTX_LEN)
    f_ctx = np.concatenate([np.cos(ang) * sc, np.sin(ang) * sc], axis=1)
    return tuple(jnp.asarray(a, dtype=F32).astype(BF16) for a in (f_ch, k_a, k_b, f_c, f_ctx))


def _seq_dft_kernel(zr_ref, zi_ref, ka_ref, kb_ref, fc_ref, o_ref, ar_ref, ai_ref, br_ref, bi_ref):
    a_rows = FFT_R * FFT_C
    n_tiles = FFT_C // FFT_J

    def mix(load, store, base, step, k):
        offs = [base + i * step for i in range(FFT_R)]
        z = jnp.concatenate([load(0, o) for o in offs] + [load(1, o) for o in offs], axis=0)
        u = _dot(k, z).astype(BF16)
        for i, o in enumerate(offs):
            store(0, o, u[i * FFT_J:(i + 1) * FFT_J])
            store(1, o, u[(FFT_R + i) * FFT_J:(FFT_R + i + 1) * FFT_J])

    def load_in(p, o):
        return (zr_ref, zi_ref)[p][0, 0, pl.ds(CTX_LEN + o, FFT_J), :]

    def load_a(p, o):
        return (ar_ref, ai_ref)[p][pl.ds(o, FFT_J), :]

    def store_a(p, o, val):
        (ar_ref, ai_ref)[p][pl.ds(o, FFT_J), :] = val

    def store_b(p, o, val):
        (br_ref, bi_ref)[p][pl.ds(o, FFT_J), :] = val

    def stage_a(b, carry):
        for ct in range(n_tiles):
            mix(load_in, store_a, pl.multiple_of(b * FFT_C, FFT_C) + ct * FFT_J, a_rows, ka_ref[...])
        return carry

    lax.fori_loop(0, FFT_R, stage_a, 0)

    def stage_b(a1, carry):
        k = kb_ref[a1]
        for ct in range(n_tiles):
            mix(load_a, store_b, pl.multiple_of(a1 * a_rows, a_rows) + ct * FFT_J, FFT_C, k)
        return carry

    lax.fori_loop(0, FFT_R, stage_b, 0)

    def stage_c(t1, carry):
        a1, b1 = t1 & (FFT_R - 1), lax.shift_right_logical(t1, FFT_R.bit_length() - 1)
        rows = pl.ds(pl.multiple_of(a1 * a_rows + b1 * FFT_C, FFT_C), FFT_C)
        z = jnp.concatenate([br_ref[rows, :], bi_ref[rows, :]], axis=0)
        y = _dot(fc_ref[t1], z)
        for j in range(FFT_CB // LANES):
            o_ref[0, j, pl.ds(t1, FFT_C, stride=FFT_R * FFT_R), :] = y[:, j * LANES:(j + 1) * LANES]
        return carry

    lax.fori_loop(0, FFT_R * FFT_R, stage_c, 0, unroll=DFT_UNROLL)


def _seq_dft(zr, zi, k_a, k_b, f_c):
    n_batch, n_blocks, rows, _ = zr.shape
    seq = rows - CTX_LEN
    blk = pl.BlockSpec((1, 1, rows, FFT_CB), lambda b, j: (b, j, 0, 0))
    return pl.pallas_call(
        _seq_dft_kernel,
        out_shape=jax.ShapeDtypeStruct((n_batch, D_MODEL // LANES, seq, LANES), F32),
        grid=(n_batch, n_blocks),
        in_specs=[blk, blk, _const_spec(k_a.shape), _const_spec(k_b.shape), _const_spec(f_c.shape)],
        out_specs=pl.BlockSpec((1, FFT_CB // LANES, seq, LANES), lambda b, j: (b, j, 0, 0)),
        scratch_shapes=[pltpu.VMEM((seq, FFT_CB), BF16)] * 4,
        compiler_params=_params("parallel", "parallel", vmem_limit=SEQ_DFT_VMEM_LIMIT),
        name="fourier_seq_dft",
    )(zr, zi, k_a, k_b, f_c)


def _ctx_dft_kernel(zr_ref, zi_ref, f_ref, o_ref):
    planes = [jnp.concatenate([z_ref[0, g] for g in range(F_GROUPS)], axis=1) for z_ref in (zr_ref, zi_ref)]
    o_ref[0] = _dot(f_ref[...], jnp.concatenate(planes, axis=0))


def _ctx_dft(zr, zi, f_ctx):
    n_batch = zr.shape[0]
    blk = pl.BlockSpec((1, CTX_LEN, D_MODEL), lambda b: (b, 0, 0))
    planes = pl.BlockSpec((1, F_GROUPS, CTX_LEN, F_GROUP_DIM), lambda b: (b, 0, 0, 0))
    return pl.pallas_call(
        _ctx_dft_kernel,
        out_shape=jax.ShapeDtypeStruct((n_batch, CTX_LEN, D_MODEL), F32),
        grid=(n_batch,),
        in_specs=[planes, planes, _const_spec((CTX_LEN, 2 * CTX_LEN))],
        out_specs=blk,
        compiler_params=_params("parallel"),
        name="fourier_ctx_dft",
    )(zr, zi, f_ctx)


def _mlstm_layer(xs, mod_l, w_in, w_conv, b_conv, w_q, w_k, w_v, w_gate, b_gate, gn_w, skip, w_out, pg, pb):
    xm, z = _in_proj(xs, mod_l, w_in.astype(BF16))
    wg = w_gate.reshape(3, M_HEADS, M_HEAD_DIM, N_GATES).transpose(1, 0, 3, 2)
    w_qt = jnp.swapaxes(w_q, 1, 2).astype(BF16)
    w_vt = jnp.swapaxes(w_v, 1, 2).astype(BF16)
    xc, q_t, k, v_t, pre_t = _conv_qkv(
        xm, w_conv.reshape(CONV_K * CONV_K, D_INNER), b_conv.reshape(1, D_INNER),
        w_qt, w_k.astype(BF16), w_vt, wg.astype(BF16), b_gate.reshape(N_GATES, 1))
    hf, hb = _mlstm_scan(q_t, k, v_t, _gate_scans(pre_t))
    return _mlstm_readout(hf, hb, xc, z, xs, mod_l, gn_w.reshape(1, D_INNER), skip.reshape(1, D_INNER),
                          w_out.astype(BF16), pg, pb)


def _fourier_layer(xs, zr, zi, mod_l, w_out, pg, pb, need_ctx, tables):
    _, k_a, k_b, f_c, f_ctx = tables
    mixed_x = _seq_dft(zr, zi, k_a, k_b, f_c)
    mixed_c = _ctx_dft(zr, zi, f_ctx) if need_ctx else None
    return _fourier_out(mixed_x, mixed_c, xs, mod_l, w_out.astype(BF16), pg, pb)


def kernel(x, c, ctx, c_ctx, ada_w, ada_b, post_g, post_b, m_w_in, m_w_conv, m_b_conv, m_w_q, m_w_k, m_w_v, m_w_gate, m_b_gate, m_gn_w, m_skip, m_w_out, f_w_out, mlp_w1, mlp_w2):
    n_batch = x.shape[0]
    assert n_batch < MOD_ROWS and x.shape[1] == FFT_R * FFT_R * FFT_C and ctx.shape[1] == CTX_LEN == TM
    xs = (ctx, x)
    s_in = jnp.concatenate([c, c_ctx[None, :], jnp.zeros((MOD_ROWS - n_batch - 1, D_MODEL), F32)], axis=0)
    mod = _ada_table(s_in, ada_w, ada_b)
    tables = _dft_tables()
    for i in range(DEPTH):
        is_mlstm = i % N_MIXERS == 0
        j = i // N_MIXERS
        need_ctx = i < DEPTH - 1
        mod_l = mod[i].reshape(MOD_ROWS, 1, N_MOD * D_MODEL)
        fourier_next = None
        if is_mlstm and i + 1 < DEPTH:
            fourier_next = (mod[i + 1].reshape(MOD_ROWS, 1, N_MOD * D_MODEL), tables[0])
        pg = post_g[i].reshape(2, 1, D_MODEL)
        pb = post_b[i].reshape(2, 1, D_MODEL)
        if is_mlstm:
            xs = _mlstm_layer(xs, mod_l, m_w_in[j], m_w_conv[j], m_b_conv[j], m_w_q[j], m_w_k[j], m_w_v[j],
                              m_w_gate[j], m_b_gate[j], m_gn_w[j], m_skip[j], m_w_out[j], pg[0], pb[0])
            if not need_ctx:
                xs = xs[:, CTX_LEN:]
        else:
            xs = _fourier_layer(xs, zr, zi, mod_l, f_w_out[j], pg[0], pb[0], need_ctx, tables)
        out = _mlp(xs, mod_l, mlp_w1[i].astype(BF16), mlp_w2[i].astype(BF16), pg[1], pb[1], need_ctx, fourier_next)
        xs, zr, zi = out if fourier_next is not None else (out, None, None)
    return xs
```
